```python
import jax, jax.numpy as jnp
from jax import lax
import numpy as np

D_MODEL = 2048
BATCH = 4
SEQ = 2048
DEPTH = 1

MIX_WIDTH = D_MODEL
CONV_WIDTH = MIX_WIDTH // 4
ATTN_WIDTH = MIX_WIDTH - CONV_WIDTH
HEAD_DIM = 64
N_ATTN_HEADS = ATTN_WIDTH // HEAD_DIM
CONV_K = 3
ROT_DIM = HEAD_DIM // 4
ROPE_THETA = 500000.0
DILATED_PATTERNS = ((128, 1), (512, 4), (2048, 16))
BLOCK = 128
N_EXPERTS = 32
TOP_K = 4
D_EXPERT = D_MODEL
SWIGLU_LIMIT = 7.0
SWIGLU_ALPHA = 1.702
EXPERT_BLOCK = 128
EPS = 1e-5
IN_COLS = 3 * CONV_WIDTH + 3 * ATTN_WIDTH

kernel_name = "hybrid_conv_dilated_attn_moe"


def rmsnorm(x, g):
    xf = x.astype(jnp.float32)
    y = xf * lax.rsqrt(jnp.mean(xf * xf, axis=-1, keepdims=True) + EPS)
    return (y * g.astype(jnp.float32)).astype(x.dtype)


def partial_rotary(t):
    S_ = t.shape[1]
    inv_freq = ROPE_THETA ** (-jnp.arange(0, ROT_DIM, 2, dtype=jnp.float32) / ROT_DIM)
    ang = jnp.arange(S_, dtype=jnp.float32)[:, None] * inv_freq[None, :]
    cos = jnp.cos(ang)[None, :, None, :]
    sin = jnp.sin(ang)[None, :, None, :]
    rot = t[..., :ROT_DIM].astype(jnp.float32)
    x1, x2 = rot[..., :ROT_DIM // 2], rot[..., ROT_DIM // 2:]
    rotated = jnp.concatenate([x1 * cos - x2 * sin, x2 * cos + x1 * sin], axis=-1)
    return jnp.concatenate([rotated.astype(t.dtype), t[..., ROT_DIM:]], axis=-1)


def causal_short_conv(u, w):
    C = u.shape[-1]
    return lax.conv_general_dilated(
        u, w[:, None, :].astype(u.dtype), window_strides=(1,), padding=[(CONV_K - 1, 0)],
        dimension_numbers=("NWC", "WIO", "NWC"), feature_group_count=C)


def banded_causal_attention(q, k, v, window):
    B_, L, G, Dh = q.shape
    nblk = -(-L // BLOCK)
    pad = nblk * BLOCK - L
    padcfg = ((0, 0), (0, pad), (0, 0), (0, 0))
    qb = jnp.pad(q, padcfg).reshape(B_, nblk, BLOCK, G, Dh)
    kb = jnp.pad(k, padcfg).reshape(B_, nblk, BLOCK, G, Dh)
    vb = jnp.pad(v, padcfg).reshape(B_, nblk, BLOCK, G, Dh)
    zero = jnp.zeros_like(kb[:, :1])
    kk = jnp.concatenate([jnp.concatenate([zero, kb[:, :-1]], axis=1), kb], axis=2)
    vv = jnp.concatenate([jnp.concatenate([zero, vb[:, :-1]], axis=1), vb], axis=2)
    s = jnp.einsum("bnqgd,bnkgd->bngqk", qb, kk).astype(jnp.float32) * (Dh ** -0.5)
    qpos = BLOCK + jnp.arange(BLOCK)
    kpos = jnp.arange(2 * BLOCK)
    dist = qpos[:, None] - kpos[None, :]
    band = (dist >= 0) & (dist <= window)
    key_abs = jnp.arange(nblk)[:, None, None] * BLOCK - BLOCK + kpos[None, None, :]
    mask = band[None] & (key_abs >= 0)
    s = jnp.where(mask[None, :, None], s, -jnp.inf)
    m = jnp.max(s, axis=-1, keepdims=True)
    p = jnp.exp(s - m)
    l = jnp.sum(p, axis=-1, keepdims=True)
    o = jnp.einsum("bngqk,bnkgd->bnqgd", p, vv.astype(jnp.float32)) / jnp.swapaxes(l, 2, 3)
    lse = jnp.swapaxes((m + jnp.log(l))[..., 0], 2, 3)
    o = o.reshape(B_, nblk * BLOCK, G, Dh)[:, :L]
    lse = lse.reshape(B_, nblk * BLOCK, G)[:, :L]
    return o, lse


def dilated_branch(q, k, v, window, dilation):
    B_, S_, H, Dh = q.shape
    L = S_ // dilation
    to_cls = lambda t: t.reshape(B_, L, dilation * H, Dh)
    o, lse = banded_causal_attention(to_cls(q), to_cls(k), to_cls(v), window // dilation)
    return o.reshape(B_, S_, H, Dh), lse.reshape(B_, S_, H)


def dilated_attention(q, k, v):
    outs, lses = [], []
    for window, dilation in DILATED_PATTERNS:
        o, lse = dilated_branch(q, k, v, window, dilation)
        outs.append(o)
        lses.append(lse)
    wts = jax.nn.softmax(jnp.stack(lses, axis=0), axis=0)
    o = jnp.sum(wts[..., None] * jnp.stack(outs, axis=0), axis=0)
    return o.astype(q.dtype)


def clamped_swiglu_expert(xb, w_gu, b_gu, w_dn, b_dn):
    gu = xb @ w_gu + b_gu
    gate, up = gu[:, :D_EXPERT], gu[:, D_EXPERT:]
    gate = jnp.minimum(gate, SWIGLU_LIMIT)
    up = jnp.clip(up, -SWIGLU_LIMIT, SWIGLU_LIMIT)
    act = (up + 1.0) * (gate * jax.nn.sigmoid(SWIGLU_ALPHA * gate))
    return act @ w_dn + b_dn


def moe_ffn(h, w_router, b_router, w_gate_up, b_gate_up, w_down, b_down):
    B_, S_, D = h.shape
    N = B_ * S_
    t = h.reshape(N, D)
    logits = (t @ w_router + b_router).astype(jnp.float32)
    top_val, top_idx = lax.top_k(logits, TOP_K)
    gates = jax.nn.softmax(top_val, axis=-1)
    A = N * TOP_K
    flat_e = top_idx.reshape(A)
    order = jnp.argsort(flat_e)
    sorted_e = flat_e[order]
    sorted_tok = order // TOP_K
    counts = jnp.bincount(flat_e, length=N_EXPERTS)
    starts = jnp.cumsum(counts) - counts
    padded = ((counts + EXPERT_BLOCK - 1) // EXPERT_BLOCK) * EXPERT_BLOCK
    pad_ends = jnp.cumsum(padded)
    pad_starts = pad_ends - padded
    n_blocks = -(-A // EXPERT_BLOCK) + N_EXPERTS
    dest = pad_starts[sorted_e] + (jnp.arange(A) - starts[sorted_e])
    buf = jnp.zeros((n_blocks * EXPERT_BLOCK, D), t.dtype).at[dest].set(t[sorted_tok])
    block_e = jnp.minimum(
        jnp.searchsorted(pad_ends, jnp.arange(n_blocks) * EXPERT_BLOCK, side="right"), N_EXPERTS - 1)

    def run_block(args):
        xb, e = args
        return clamped_swiglu_expert(xb, w_gate_up[e], b_gate_up[e], w_down[e], b_down[e])

    y = lax.map(run_block, (buf.reshape(n_blocks, EXPERT_BLOCK, D), block_e)).reshape(-1, D)
    y_assign = y[dest] * gates.reshape(A)[order][:, None].astype(y.dtype)
    out = jax.ops.segment_sum(y_assign, sorted_tok, num_segments=N)
    return out.reshape(B_, S_, D)


def hybrid_layer(x, mix_norm_g, w_in, conv_w, conv_norm_g, attn_norm_g, w_out,
                 ffn_norm_g, w_router, b_router, w_gate_up, b_gate_up, w_down, b_down):
    B_, S_, _ = x.shape
    h = rmsnorm(x, mix_norm_g)
    proj = h @ w_in
    c0, c1, c2 = CONV_WIDTH, 2 * CONV_WIDTH, 3 * CONV_WIDTH
    conv_x, conv_b, conv_c = proj[..., :c0], proj[..., c0:c1], proj[..., c1:c2]
    q = proj[..., c2:c2 + ATTN_WIDTH].reshape(B_, S_, N_ATTN_HEADS, HEAD_DIM)
    k = proj[..., c2 + ATTN_WIDTH:c2 + 2 * ATTN_WIDTH].reshape(B_, S_, N_ATTN_HEADS, HEAD_DIM)
    v = proj[..., c2 + 2 * ATTN_WIDTH:].reshape(B_, S_, N_ATTN_HEADS, HEAD_DIM)
    y_conv = conv_b * causal_short_conv(conv_c * conv_x, conv_w)
    y_attn = dilated_attention(partial_rotary(q), partial_rotary(k), v).reshape(B_, S_, ATTN_WIDTH)
    mixed = jnp.concatenate([rmsnorm(y_conv, conv_norm_g), rmsnorm(y_attn, attn_norm_g)], axis=-1)
    x = x + mixed @ w_out
    x = x + moe_ffn(rmsnorm(x, ffn_norm_g), w_router, b_router, w_gate_up, b_gate_up, w_down, b_down)
    return x


def setup_inputs(seed: int = 0) -> dict:
    key = jax.random.key(seed)
    ks = jax.random.split(key, 16)
    nrm = lambda k, shape, scale: jax.random.normal(k, shape, jnp.float32) * scale
    gain = lambda k, shape: 1.0 + 0.02 * jax.random.normal(k, shape, jnp.float32)
    return {
        "x": nrm(ks[0], (BATCH, SEQ, D_MODEL), 1.0),
        "mix_norm_g": gain(ks[1], (DEPTH, D_MODEL)),
        "w_in": nrm(ks[2], (DEPTH, D_MODEL, IN_COLS), D_MODEL ** -0.5),
        "conv_w": nrm(ks[3], (DEPTH, CONV_K, CONV_WIDTH), CONV_K ** -0.5),
        "conv_norm_g": gain(ks[4], (DEPTH, CONV_WIDTH)),
        "attn_norm_g": gain(ks[5], (DEPTH, ATTN_WIDTH)),
        "w_out": nrm(ks[6], (DEPTH, MIX_WIDTH, D_MODEL), MIX_WIDTH ** -0.5),
        "ffn_norm_g": gain(ks[7], (DEPTH, D_MODEL)),
        "w_router": nrm(ks[8], (DEPTH, D_MODEL, N_EXPERTS), D_MODEL ** -0.5),
        "b_router": nrm(ks[9], (DEPTH, N_EXPERTS), 0.01),
        "w_gate_up": nrm(ks[10], (DEPTH, N_EXPERTS, D_MODEL, 2 * D_EXPERT), D_MODEL ** -0.5),
        "b_gate_up": nrm(ks[11], (DEPTH, N_EXPERTS, 2 * D_EXPERT), 0.02),
        "w_down": nrm(ks[12], (DEPTH, N_EXPERTS, D_EXPERT, D_MODEL), D_EXPERT ** -0.5),
        "b_down": nrm(ks[13], (DEPTH, N_EXPERTS, D_MODEL), 0.02),
        "final_norm_g": gain(ks[14], (D_MODEL,)),
    }


def reference(x, mix_norm_g, w_in, conv_w, conv_norm_g, attn_norm_g, w_out, ffn_norm_g,
              w_router, b_router, w_gate_up, b_gate_up, w_down, b_down, final_norm_g):
    for layer in range(DEPTH):
        x = hybrid_layer(x, mix_norm_g[layer], w_in[layer], conv_w[layer], conv_norm_g[layer],
                         attn_norm_g[layer], w_out[layer], ffn_norm_g[layer], w_router[layer],
                         b_router[layer], w_gate_up[layer], b_gate_up[layer], w_down[layer],
                         b_down[layer])
    return rmsnorm(x, final_norm_g)
```

```python
import functools

import jax
import jax.numpy as jnp
from jax import lax
from jax.experimental import pallas as pl
from jax.experimental.pallas import tpu as pltpu

HEAD_DIM = 64
CONV_K = 3
ROT_DIM = HEAD_DIM // 4
ROPE_THETA = 500000.0
DILATIONS = (1, 4, 16)
ATTN_BLOCK = 128
TOP_K = 4
SWIGLU_LIMIT = 7.0
SWIGLU_ALPHA = 1.702
EPS = 1e-5

LANES = 128
SUBLANES = 8

MOE_ROW_GRAN = 256
MOE_ITEM_ROWS = 1024
MOE_COL_TILE = 256
BF16 = jnp.bfloat16
F32 = jnp.float32


def _rms(x, g):
    return x * lax.rsqrt(jnp.mean(x * x, axis=-1, keepdims=True) + EPS) * g


def _in_proj_kernel(x_ref, g_ref, w_ref, o_ref, xn_ref):
    @pl.when(pl.program_id(1) == 0)
    def _():
        xn_ref[...] = _rms(x_ref[...], g_ref[...]).astype(BF16)

    o_ref[...] = jnp.dot(xn_ref[...], w_ref[...], preferred_element_type=F32)


def _in_proj(x2, g, w_bf16, tm=512, tn=512):
    n, d = x2.shape
    cols = w_bf16.shape[1]
    return pl.pallas_call(
        _in_proj_kernel,
        grid=(n // tm, cols // tn),
        in_specs=[
            pl.BlockSpec((tm, d), lambda i, j: (i, 0)),
            pl.BlockSpec((1, d), lambda i, j: (0, 0)),
            pl.BlockSpec((d, tn), lambda i, j: (0, j)),
        ],
        out_specs=pl.BlockSpec((tm, tn), lambda i, j: (i, j)),
        out_shape=jax.ShapeDtypeStruct((n, cols), F32),
        scratch_shapes=[pltpu.VMEM((tm, d), BF16)],
        compiler_params=pltpu.CompilerParams(
            dimension_semantics=("arbitrary", "arbitrary"),
            vmem_limit_bytes=40 * 1024 * 1024),
        name="in_proj",
    )(x2, g, w_bf16)


def _attn_kernel(q_ref, k_ref, v_ref, cos_ref, sa_ref, sb_ref, o_ref,
                 qn, kn, q4, k4, v4, q16, k16, v16,
                 o1, l1, o4c, l4c, o16c, l16c, o4, l4, o16, l16, bias_s, *, seq):
    nblk_total = seq // ATTN_BLOCK
    lane = lax.broadcasted_iota(jnp.int32, (1, LANES), 1)
    head0 = lane < HEAD_DIM

    qi = lax.broadcasted_iota(jnp.int32, (ATTN_BLOCK, 2 * ATTN_BLOCK), 0)
    kj = lax.broadcasted_iota(jnp.int32, (ATTN_BLOCK, 2 * ATTN_BLOCK), 1)
    band = (kj >= qi) & (kj <= qi + ATTN_BLOCK)
    neg = jnp.float32(-jnp.inf)
    bias_s[0] = jnp.where(band, 0.0, neg)
    bias_s[1] = jnp.where(band & (kj >= ATTN_BLOCK), 0.0, neg)

    def rot(x):
        return (x * cos_ref[...] + pltpu.roll(x, LANES - ROT_DIM // 2, 1) * sa_ref[...]
                + pltpu.roll(x, ROT_DIM // 2, 1) * sb_ref[...])

    qn[...] = rot(q_ref[...]) * (HEAD_DIM ** -0.5)
    kn[...] = rot(k_ref[...])

    for d, (qc, kc, vc) in ((4, (q4, k4, v4)), (16, (q16, k16, v16))):
        ln = seq // d
        for r in range(d):
            qc[r * ln:(r + 1) * ln, :] = qn[pl.ds(r, ln, stride=d), :]
            kc[r * ln:(r + 1) * ln, :] = kn[pl.ds(r, ln, stride=d), :]
            vc[r * ln:(r + 1) * ln, :] = v_ref[pl.ds(r, ln, stride=d), :]

    def run_branch(qsrc, ksrc, vsrc, o_dst, l_dst, blocks_per_class):
        def body(b, carry):
            cur = pl.ds(pl.multiple_of(b * ATTN_BLOCK, ATTN_BLOCK), ATTN_BLOCK)
            prv = pl.ds(pl.multiple_of(jnp.maximum(b - 1, 0) * ATTN_BLOCK, ATTN_BLOCK), ATTN_BLOCK)
            first = jnp.where(jnp.asarray(b % blocks_per_class == 0), 1, 0)
            bias = bias_s[first]
            qb = qsrc[cur, :]
            kk = jnp.concatenate([ksrc[prv, :], ksrc[cur, :]], axis=0).astype(BF16)
            vv = jnp.concatenate([vsrc[prv, :], vsrc[cur, :]], axis=0)
            res, mx = [], []
            for h in range(2):
                mine = head0 if h == 0 else jnp.logical_not(head0)
                qh = jnp.where(mine, qb, 0.0).astype(BF16)
                s = lax.dot_general(qh, kk, (((1,), (1,)), ((), ())), preferred_element_type=F32)
                s = s + bias
                m = jnp.max(s, axis=-1, keepdims=True)
                p = jnp.exp(s - m).astype(BF16)
                vh = jnp.where(mine, vv, 1.0).astype(BF16)
                res.append(jnp.dot(p, vh, preferred_element_type=F32))
                mx.append(m)
            acc = jnp.where(head0, res[0], res[1])
            den = pltpu.roll(jnp.where(head0, res[1], res[0]), HEAD_DIM, 1)
            mm = jnp.where(head0, mx[0], mx[1])
            o_dst[cur, :] = acc / den
            l_dst[cur, :] = mm + jnp.log(den)
            return carry

        lax.fori_loop(0, nblk_total, body, 0)

    run_branch(qn, kn, v_ref, o1, l1, nblk_total)
    run_branch(q4, k4, v4, o4c, l4c, nblk_total // 4)
    run_branch(q16, k16, v16, o16c, l16c, nblk_total // 16)

    for d, oc, lc, on, lnat in ((4, o4c, l4c, o4, l4), (16, o16c, l16c, o16, l16)):
        ln = seq // d
        for r in range(d):
            on[pl.ds(r, ln, stride=d), :] = oc[r * ln:(r + 1) * ln, :]
            lnat[pl.ds(r, ln, stride=d), :] = lc[r * ln:(r + 1) * ln, :]

    la, lb, lc_ = l1[...], l4[...], l16[...]
    mx = jnp.maximum(jnp.maximum(la, lb), lc_)
    ea, eb, ec = jnp.exp(la - mx), jnp.exp(lb - mx), jnp.exp(lc_ - mx)
    o_ref[...] = (ea * o1[...] + eb * o4[...] + ec * o16[...]) / (ea + eb + ec)


def _rotary_tables(seq):
    half = ROT_DIM // 2
    inv_freq = ROPE_THETA ** (-jnp.arange(0, ROT_DIM, 2, dtype=F32) / ROT_DIM)
    ang = jnp.arange(seq, dtype=F32)[:, None] * inv_freq[None, :]
    cos, sin = jnp.cos(ang), jnp.sin(ang)
    pos = jnp.arange(LANES) % HEAD_DIM
    fidx = pos % half
    in_lo = pos < half
    in_hi = (pos >= half) & (pos < ROT_DIM)
    cos_t = jnp.where((in_lo | in_hi)[None, :], cos[:, fidx], 1.0)
    sa = jnp.where(in_lo[None, :], -sin[:, fidx], 0.0)
    sb = jnp.where(in_hi[None, :], sin[:, fidx], 0.0)
    return cos_t.astype(F32), sa.astype(F32), sb.astype(F32)


def _attention(proj3, conv_w_cols, attn_w_cols):
    b, seq, _ = proj3.shape
    assert seq % (ATTN_BLOCK * DILATIONS[-1]) == 0
    n_pairs = attn_w_cols // LANES
    qoff = 3 * conv_w_cols // LANES
    cos_t, sa, sb = _rotary_tables(seq)
    blk = lambda off: pl.BlockSpec((None, seq, LANES), lambda i, j: (i, 0, off + j))
    tab = pl.BlockSpec((seq, LANES), lambda i, j: (0, 0))
    big = pltpu.VMEM((seq, LANES), F32)
    return pl.pallas_call(
        functools.partial(_attn_kernel, seq=seq),
        grid=(b, n_pairs),
        in_specs=[blk(qoff), blk(qoff + n_pairs), blk(qoff + 2 * n_pairs), tab, tab, tab],
        out_specs=pl.BlockSpec((None, seq, LANES), lambda i, j: (i, 0, j)),
        out_shape=jax.ShapeDtypeStruct((b, seq, attn_w_cols), F32),
        scratch_shapes=[big] * 18 + [pltpu.VMEM((2, ATTN_BLOCK, 2 * ATTN_BLOCK), F32)],
        compiler_params=pltpu.CompilerParams(
            dimension_semantics=("arbitrary", "arbitrary"),
            vmem_limit_bytes=48 * 1024 * 1024),
        name="attention",
    )(proj3, proj3, proj3, cos_t, sa, sb)


def _out_proj_kernel(cx_ref, cb_ref, cc_ref, hx_ref, hc_ref, ya_ref, x_ref,
                     cw_ref, cg_ref, ag_ref, wo_ref, fg_ref, wr_ref, br_ref,
                     x1_ref, h2_ref, idx_ref, gate_ref, rank_ref, cnt_ref, carry,
                     *, tile, n_experts):
    first_tile_of_seq = pl.program_id(1) == 0
    first_step = (pl.program_id(0) == 0) & first_tile_of_seq

    @pl.when(first_step)
    def _():
        carry[...] = jnp.zeros_like(carry)

    u = cc_ref[...] * cx_ref[...]
    uh = jnp.where(first_tile_of_seq, 0.0, hc_ref[...] * hx_ref[...])
    row8 = lax.broadcasted_iota(jnp.int32, (SUBLANES, 1), 0)

    def shifted(k):
        r = pltpu.roll(u, k, 0)
        top = jnp.where(row8 < k, pltpu.roll(uh, k, 0), r[:SUBLANES])
        return jnp.concatenate([top, r[SUBLANES:]], axis=0)

    conv = cw_ref[2:3, :] * u + cw_ref[1:2, :] * shifted(1) + cw_ref[0:1, :] * shifted(2)
    y_conv = cb_ref[...] * conv

    mixed = jnp.concatenate(
        [_rms(y_conv, cg_ref[...]), _rms(ya_ref[...], ag_ref[...])], axis=-1).astype(BF16)
    x1 = x_ref[...] + jnp.dot(mixed, wo_ref[...], preferred_element_type=F32)
    x1_ref[...] = x1
    h2 = _rms(x1, fg_ref[...])
    d = h2.shape[-1]
    chunks = d // LANES
    for c in range(chunks):
        h2_ref[pl.ds(c, tile, stride=chunks), :] = h2[:, c * LANES:(c + 1) * LANES]

    logits = jnp.dot(h2, wr_ref[...], preferred_element_type=F32,
                     precision=lax.Precision.HIGHEST) + br_ref[...]
    eio = lax.broadcasted_iota(jnp.int32, (tile, n_experts), 1).astype(F32)
    work = logits
    vals, idxs = [], []
    for _ in range(TOP_K):
        m = jnp.max(work, axis=1, keepdims=True)
        ik = jnp.min(jnp.where(work == m, eio, float(n_experts)), axis=1, keepdims=True)
        vals.append(m)
        idxs.append(ik)
        work = jnp.where(eio == ik, -jnp.inf, work)
    exps = [jnp.exp(v - vals[0]) for v in vals]
    tot = exps[0] + exps[1] + exps[2] + exps[3]

    onehot = jnp.zeros((tile, n_experts), F32)
    for ik in idxs:
        onehot = onehot + (eio == ik).astype(F32)
    ri = lax.broadcasted_iota(jnp.int32, (tile, tile), 0)
    ci = lax.broadcasted_iota(jnp.int32, (tile, tile), 1)
    tri = (ci < ri).astype(BF16)
    before = jnp.dot(tri, onehot.astype(BF16), preferred_element_type=F32) + carry[...]
    carry[...] = carry[...] + jnp.sum(onehot, axis=0, keepdims=True)
    cnt_ref[...] = carry[...].astype(jnp.int32)

    lio = lax.broadcasted_iota(jnp.int32, (tile, LANES), 1)
    idx_out = jnp.zeros((tile, LANES), jnp.int32)
    gate_out = jnp.zeros((tile, LANES), F32)
    rank_out = jnp.zeros((tile, LANES), jnp.int32)
    for k in range(TOP_K):
        rk = jnp.sum(jnp.where(eio == idxs[k], before, 0.0), axis=1, keepdims=True)
        idx_out = jnp.where(lio == k, idxs[k].astype(jnp.int32), idx_out)
        gate_out = jnp.where(lio == k, exps[k] / tot, gate_out)
        rank_out = jnp.where(lio == k, rk.astype(jnp.int32), rank_out)
    idx_ref[...] = idx_out
    gate_ref[...] = gate_out
    rank_ref[...] = rank_out


def _out_proj(proj3, y_attn, x3, conv_w, conv_g, attn_g, w_out_bf16, ffn_g, w_router, b_router,
              tile=256):
    b, seq, d = x3.shape
    cw_cols = conv_w.shape[1]
    aw_cols = y_attn.shape[2]
    n_experts = w_router.shape[1]
    chunks = d // LANES
    tiles = seq // tile
    n = b * seq
    halo_blocks = tile // SUBLANES
    row_blk = lambda width, col: pl.BlockSpec((None, tile, width), lambda i, j: (i, j, col))
    halo = lambda col: pl.BlockSpec(
        (None, SUBLANES, cw_cols), lambda i, j: (i, jnp.maximum(j * halo_blocks - 1, 0), col))
    const = lambda shape: pl.BlockSpec(shape, lambda i, j: (0,) * len(shape))
    flat = lambda width: pl.BlockSpec((tile, width), lambda i, j: (i * tiles + j, 0))
    outs = pl.pallas_call(
        functools.partial(_out_proj_kernel, tile=tile, n_experts=n_experts),
        grid=(b, tiles),
        in_specs=[
            row_blk(cw_cols, 0), row_blk(cw_cols, 1), row_blk(cw_cols, 2), halo(0), halo(2),
            row_blk(aw_cols, 0), row_blk(d, 0),
            const((CONV_K, cw_cols)), const((1, cw_cols)), const((1, aw_cols)),
            const((d, d)), const((1, d)), const((d, n_experts)), const((1, n_experts)),
        ],
        out_specs=[
            row_blk(d, 0),
            pl.BlockSpec((tile * chunks, LANES), lambda i, j: (i * tiles + j, 0)),
            flat(LANES), flat(LANES), flat(LANES),
            const((1, n_experts)),
        ],
        out_shape=[
            jax.ShapeDtypeStruct((b, seq, d), F32),
            jax.ShapeDtypeStruct((n * chunks, LANES), F32),
            jax.ShapeDtypeStruct((n, LANES), jnp.int32),
            jax.ShapeDtypeStruct((n, LANES), F32),
            jax.ShapeDtypeStruct((n, LANES), jnp.int32),
            jax.ShapeDtypeStruct((1, n_experts), jnp.int32),
        ],
        scratch_shapes=[pltpu.VMEM((1, n_experts), F32)],
        compiler_params=pltpu.CompilerParams(
            dimension_semantics=("arbitrary", "arbitrary"),
            vmem_limit_bytes=48 * 1024 * 1024),
        name="out_proj",
    )(proj3, proj3, proj3, proj3, proj3, y_attn, x3, conv_w, conv_g, attn_g, w_out_bf16,
      ffn_g, w_router, b_router)
    return outs


def _row_copy(src_hbm, src_row, dst_hbm, dst_row, chunks, sem):
    return pltpu.make_async_copy(
        src_hbm.at[pl.ds(pl.multiple_of(src_row * chunks, chunks), chunks), :],
        dst_hbm.at[pl.ds(pl.multiple_of(dst_row * chunks, chunks), chunks), :],
        sem)


def _dispatch_kernel(dest_ref, h2_hbm, xs_in_hbm, xs_hbm, sem, *, tile, chunks):
    del xs_in_hbm
    base = pl.program_id(0) * tile

    def issue(a, carry):
        _row_copy(h2_hbm, base + a // TOP_K, xs_hbm, dest_ref[0, 0, a], chunks, sem).start()
        return carry

    lax.fori_loop(0, tile * TOP_K, issue, 0)

    def drain(a, carry):
        _row_copy(h2_hbm, 0, xs_hbm, 0, chunks, sem).wait()
        return carry

    lax.fori_loop(0, tile * TOP_K, drain, 0)


def _dispatch(dest, h2_tm, n_rows, chunks, tile=512):
    steps = dest.shape[0] // (tile * TOP_K)
    dest3 = dest.reshape(steps, 1, tile * TOP_K)
    xs0 = jnp.zeros((n_rows * chunks, LANES), F32)
    return pl.pallas_call(
        functools.partial(_dispatch_kernel, tile=tile, chunks=chunks),
        grid=(steps,),
        in_specs=[
            pl.BlockSpec((1, 1, tile * TOP_K), lambda i: (i, 0, 0), memory_space=pltpu.SMEM),
            pl.BlockSpec(memory_space=pl.ANY),
            pl.BlockSpec(memory_space=pl.ANY),
        ],
        out_specs=pl.BlockSpec(memory_space=pl.ANY),
        out_shape=jax.ShapeDtypeStruct(xs0.shape, F32),
        scratch_shapes=[pltpu.SemaphoreType.DMA(())],
        input_output_aliases={2: 0},
        compiler_params=pltpu.CompilerParams(dimension_semantics=("arbitrary",)),
        name="dispatch",
    )(dest3, h2_tm, xs0)


def _moe_kernel(item_e, item_row0, item_nblk, used_blocks, xs_hbm, wg_ref, wu_ref, bg_ref,
                bu_ref, wd_ref, bd_ref, ys_hbm, x_in, xb, yacc, wgb, wub, wdb, stage, sem_in,
                sem_out, *, chunks, n_col_steps, n_row_blocks):
    del item_e
    s = pl.program_id(0)
    j = pl.program_id(1)
    nb = item_nblk[s]
    row0 = item_row0[s]
    gran = MOE_ROW_GRAN

    def ys_block(blk):
        return ys_hbm.at[pl.ds(pl.multiple_of(blk * (gran * chunks), gran * chunks),
                               gran * chunks), :]

    @pl.when((s == pl.num_programs(0) - 1) & (j == n_col_steps - 1))
    def _():
        stage[...] = jnp.zeros_like(stage)

        def fill(blk, carry):
            cp = pltpu.make_async_copy(stage, ys_block(blk), sem_out)
            cp.start()
            cp.wait()
            return carry

        lax.fori_loop(used_blocks[0], n_row_blocks, fill, 0)

    @pl.when(nb > 0)
    def _():
        @pl.when(j == 0)
        def _():
            def load(sb, carry):
                cp = pltpu.make_async_copy(
                    xs_hbm.at[pl.ds(pl.multiple_of((row0 + sb * gran) * chunks, gran * chunks),
                                    gran * chunks), :], x_in, sem_in)
                cp.start()
                cp.wait()
                rows = pl.ds(pl.multiple_of(sb * gran, gran), gran)
                for c in range(chunks):
                    xb[rows, c * LANES:(c + 1) * LANES] = (
                        x_in[pl.ds(c, gran, stride=chunks), :].astype(BF16))
                return carry

            lax.fori_loop(0, nb, load, 0)

        wgb[...] = wg_ref[...].astype(BF16)
        wub[...] = wu_ref[...].astype(BF16)
        wdb[...] = wd_ref[...].astype(BF16)

        def block(sb, carry):
            rows = pl.ds(pl.multiple_of(sb * gran, gran), gran)
            x = xb[rows, :]
            gate = jnp.dot(x, wgb[...], preferred_element_type=F32) + bg_ref[...]
            up = jnp.dot(x, wub[...], preferred_element_type=F32) + bu_ref[...]
            gate = jnp.minimum(gate, SWIGLU_LIMIT)
            up = jnp.clip(up, -SWIGLU_LIMIT, SWIGLU_LIMIT)
            act = (up + 1.0) * (gate * jax.nn.sigmoid(SWIGLU_ALPHA * gate))
            part = jnp.dot(act.astype(BF16), wdb[...], preferred_element_type=F32)

            @pl.when(j == 0)
            def _():
                yacc[rows, :] = part + bd_ref[...]

            @pl.when(j > 0)
            def _():
                yacc[rows, :] = yacc[rows, :] + part

            return carry

        lax.fori_loop(0, nb, block, 0)

        @pl.when(j == n_col_steps - 1)
        def _():
            def store(sb, carry):
                rows = pl.ds(pl.multiple_of(sb * gran, gran), gran)
                for c in range(chunks):
                    stage[pl.ds(c, gran, stride=chunks), :] = yacc[rows, c * LANES:(c + 1) * LANES]
                cp = pltpu.make_async_copy(
                    stage,
                    ys_hbm.at[pl.ds(pl.multiple_of((row0 + sb * gran) * chunks, gran * chunks),
                                    gran * chunks), :], sem_out)
                cp.start()
                cp.wait()
                return carry

            lax.fori_loop(0, nb, store, 0)


def _moe(item_e, item_row0, item_nblk, used_blocks, xs, w_gate_up, b_gate_up, w_down, b_down,
         n_rows):
    n_experts, d, two_de = w_gate_up.shape
    de = two_de // 2
    chunks = d // LANES
    tn = MOE_COL_TILE
    n_col_steps = de // tn
    n_items = item_e.shape[0]
    gran = MOE_ROW_GRAN

    def col(j, nblk):
        return jnp.where(nblk > 0, j, n_col_steps - 1)

    bgu3 = b_gate_up.reshape(n_experts, 1, two_de)
    bd3 = b_down.reshape(n_experts, 1, d)
    grid_spec = pltpu.PrefetchScalarGridSpec(
        num_scalar_prefetch=4,
        grid=(n_items, n_col_steps),
        in_specs=[
            pl.BlockSpec(memory_space=pl.ANY),
            pl.BlockSpec((None, d, tn), lambda s, j, e, r, nb, ub: (e[s], 0, col(j, nb[s]))),
            pl.BlockSpec((None, d, tn),
                         lambda s, j, e, r, nb, ub: (e[s], 0, n_col_steps + col(j, nb[s]))),
            pl.BlockSpec((None, 1, tn), lambda s, j, e, r, nb, ub: (e[s], 0, col(j, nb[s]))),
            pl.BlockSpec((None, 1, tn),
                         lambda s, j, e, r, nb, ub: (e[s], 0, n_col_steps + col(j, nb[s]))),
            pl.BlockSpec((None, tn, d), lambda s, j, e, r, nb, ub: (e[s], col(j, nb[s]), 0)),
            pl.BlockSpec((None, 1, d), lambda s, j, e, r, nb, ub: (e[s], 0, 0)),
        ],
        out_specs=pl.BlockSpec(memory_space=pl.ANY),
        scratch_shapes=[
            pltpu.VMEM((gran * chunks, LANES), F32),
            pltpu.VMEM((MOE_ITEM_ROWS, d), BF16),
            pltpu.VMEM((MOE_ITEM_ROWS, d), F32),
            pltpu.VMEM((d, tn), BF16),
            pltpu.VMEM((d, tn), BF16),
            pltpu.VMEM((tn, d), BF16),
            pltpu.VMEM((gran * chunks, LANES), F32),
            pltpu.SemaphoreType.DMA(()),
            pltpu.SemaphoreType.DMA(()),
        ],
    )
    return pl.pallas_call(
        functools.partial(_moe_kernel, chunks=chunks, n_col_steps=n_col_steps,
                          n_row_blocks=n_rows // gran),
        grid_spec=grid_spec,
        out_shape=jax.ShapeDtypeStruct((n_rows * chunks, LANES), F32),
        compiler_params=pltpu.CompilerParams(
            dimension_semantics=("arbitrary", "arbitrary"),
            vmem_limit_bytes=56 * 1024 * 1024),
        name="moe",
    )(item_e, item_row0, item_nblk, used_blocks, xs, w_gate_up, w_gate_up, bgu3, bgu3, w_down,
      bd3)


def _combine_kernel(dest_ref, ys_hbm, x1_ref, gate_ref, fg_ref, o_ref, gbuf, sem,
                    *, tile, chunks, final_norm):
    def issue(a, carry):
        pltpu.make_async_copy(
            ys_hbm.at[pl.ds(pl.multiple_of(dest_ref[0, 0, a] * chunks, chunks), chunks), :],
            gbuf.at[pl.ds(pl.multiple_of(a * chunks, chunks), chunks), :], sem).start()
        return carry

    lax.fori_loop(0, tile * TOP_K, issue, 0)

    def drain(a, carry):
        pltpu.make_async_copy(ys_hbm.at[pl.ds(0, chunks), :], gbuf.at[pl.ds(0, chunks), :],
                              sem).wait()
        return carry

    lax.fori_loop(0, tile * TOP_K, drain, 0)

    gates = gate_ref[...]
    gk = [gates[:, k:k + 1] for k in range(TOP_K)]
    cols = []
    for c in range(chunks):
        acc = x1_ref[:, c * LANES:(c + 1) * LANES]
        for k in range(TOP_K):
            acc = acc + gk[k] * gbuf[pl.ds(k * chunks + c, tile, stride=TOP_K * chunks), :]
        cols.append(acc)
    out = jnp.concatenate(cols, axis=-1)
    if final_norm:
        out = _rms(out, fg_ref[...])
    o_ref[...] = out


def _combine(dest, ys, x1, gates, final_g, final_norm, tile=128):
    n, d = x1.shape
    chunks = d // LANES
    steps = n // tile
    dest3 = dest.reshape(steps, 1, tile * TOP_K)
    return pl.pallas_call(
        functools.partial(_combine_kernel, tile=tile, chunks=chunks, final_norm=final_norm),
        grid=(steps,),
        in_specs=[
            pl.BlockSpec((1, 1, tile * TOP_K), lambda i: (i, 0, 0), memory_space=pltpu.SMEM),
            pl.BlockSpec(memory_space=pl.ANY),
            pl.BlockSpec((tile, d), lambda i: (i, 0)),
            pl.BlockSpec((tile, LANES), lambda i: (i, 0)),
            pl.BlockSpec((1, d), lambda i: (0, 0)),
        ],
        out_specs=pl.BlockSpec((tile, d), lambda i: (i, 0)),
        out_shape=jax.ShapeDtypeStruct((n, d), F32),
        scratch_shapes=[pltpu.VMEM((tile * TOP_K * chunks, LANES), F32),
                        pltpu.SemaphoreType.DMA(())],
        compiler_params=pltpu.CompilerParams(dimension_semantics=("arbitrary",)),
        name="combine",
    )(dest3, ys, x1, gates, final_g)


def _routing_tables(counts, idx, rank, n_rows_cap):
    n_experts = counts.shape[0]
    gran, item_rows = MOE_ROW_GRAN, MOE_ITEM_ROWS
    padded = ((counts + gran - 1) // gran) * gran
    ends = jnp.cumsum(padded)
    offs = ends - padded
    dest = offs[idx] + rank
    items_per = (padded + item_rows - 1) // item_rows
    item_ends = jnp.cumsum(items_per)
    item_starts = item_ends - items_per
    n_items = n_experts + n_rows_cap // item_rows
    slot = jnp.arange(n_items, dtype=jnp.int32)
    total = item_ends[-1]
    live = slot < total
    e_of = jnp.minimum(jnp.searchsorted(item_ends, jnp.minimum(slot, total - 1), side="right"),
                       n_experts - 1).astype(jnp.int32)
    local = slot - item_starts[e_of]
    row0 = offs[e_of] + local * item_rows
    rows = jnp.clip(padded[e_of] - local * item_rows, 0, item_rows)
    nblk = jnp.where(live, rows // gran, 0).astype(jnp.int32)
    row0 = jnp.where(live, row0, 0).astype(jnp.int32)
    used_blocks = (ends[-1:] // gran).astype(jnp.int32)
    return dest.astype(jnp.int32), e_of, row0, nblk, used_blocks


def _layer(x3, mix_g, w_in, conv_w, conv_g, attn_g, w_out, ffn_g, w_router, b_router,
           w_gate_up, b_gate_up, w_down, b_down, final_g, final_norm):
    b, seq, d = x3.shape
    n = b * seq
    cw_cols = conv_w.shape[1]
    aw_cols = attn_g.shape[0]
    n_experts = w_router.shape[1]
    chunks = d // LANES

    proj = _in_proj(x3.reshape(n, d), mix_g.reshape(1, d), w_in.astype(BF16))
    proj3 = proj.reshape(b, seq, -1)
    y_attn = _attention(proj3, cw_cols, aw_cols)
    x1, h2_tm, idx, gates, rank, counts = _out_proj(
        proj3, y_attn, x3, conv_w, conv_g.reshape(1, -1), attn_g.reshape(1, -1),
        w_out.astype(BF16), ffn_g.reshape(1, d), w_router, b_router.reshape(1, -1))

    n_rows = n * TOP_K + n_experts * MOE_ROW_GRAN
    dest, item_e, item_row0, item_nblk, used_blocks = _routing_tables(
        counts[0], idx[:, :TOP_K], rank[:, :TOP_K], n_rows)
    dest_flat = dest.reshape(n * TOP_K)
    xs = _dispatch(dest_flat, h2_tm, n_rows, chunks)
    ys = _moe(item_e, item_row0, item_nblk, used_blocks, xs, w_gate_up, b_gate_up, w_down,
              b_down, n_rows)
    out = _combine(dest_flat, ys, x1.reshape(n, d), gates, final_g.reshape(1, d), final_norm)
    return out.reshape(b, seq, d)


def kernel(x, mix_norm_g, w_in, conv_w, conv_norm_g, attn_norm_g, w_out, ffn_norm_g, w_router,
           b_router, w_gate_up, b_gate_up, w_down, b_down, final_norm_g):
    depth = w_in.shape[0]
    for layer in range(depth):
        x = _layer(x, mix_norm_g[layer], w_in[layer], conv_w[layer], conv_norm_g[layer],
                   attn_norm_g[layer], w_out[layer], ffn_norm_g[layer], w_router[layer],
                   b_router[layer], w_gate_up[layer], b_gate_up[layer], w_down[layer],
                   b_down[layer], final_norm_g, layer == depth - 1)
    return x
```

```python
import functools

import jax
import jax.numpy as jnp
from jax import lax
from jax.experimental import pallas as pl
from jax.experimental.pallas import tpu as pltpu

HEAD_DIM = 64
CONV_K = 3
ROT_DIM = HEAD_DIM // 4
ROPE_THETA = 500000.0
DILATIONS = (1, 4, 16)
ATTN_BLOCK = 128
TOP_K = 4
SWIGLU_LIMIT = 7.0
SWIGLU_ALPHA = 1.702
EPS = 1e-5

LANES = 128
SUBLANES = 8

MOE_ROW_GRAN = 256
MOE_ITEM_ROWS = 1536
MOE_COL_TILE = 256
BF16 = jnp.bfloat16
F32 = jnp.float32


def _rms(x, g):
    return x * lax.rsqrt(jnp.mean(x * x, axis=-1, keepdims=True) + EPS) * g


def _in_proj_kernel(x_ref, g_ref, w_ref, o_ref, xn_ref):
    @pl.when(pl.program_id(1) == 0)
    def _():
        xn_ref[...] = _rms(x_ref[...], g_ref[...]).astype(BF16)

    o_ref[...] = jnp.dot(xn_ref[...], w_ref[...], preferred_element_type=F32)


def _in_proj(x2, g, w_bf16, tm=512, tn=512):
    n, d = x2.shape
    cols = w_bf16.shape[1]
    return pl.pallas_call(
        _in_proj_kernel,
        grid=(n // tm, cols // tn),
        in_specs=[
            pl.BlockSpec((tm, d), lambda i, j: (i, 0)),
            pl.BlockSpec((1, d), lambda i, j: (0, 0)),
            pl.BlockSpec((d, tn), lambda i, j: (0, j)),
        ],
        out_specs=pl.BlockSpec((tm, tn), lambda i, j: (i, j)),
        out_shape=jax.ShapeDtypeStruct((n, cols), F32),
        scratch_shapes=[pltpu.VMEM((tm, d), BF16)],
        compiler_params=pltpu.CompilerParams(
            dimension_semantics=("arbitrary", "arbitrary"),
            vmem_limit_bytes=40 * 1024 * 1024),
        name="in_proj",
    )(x2, g, w_bf16)


def _attn_kernel(q_ref, k_ref, v_ref, cos_ref, sa_ref, sb_ref, o_ref,
                 qn, kn, q4, k4, v4, q16, k16, v16,
                 o1, l1, o4c, l4c, o16c, l16c, o4, l4, o16, l16, bias_s, *, seq):
    nblk_total = seq // ATTN_BLOCK
    lane = lax.broadcasted_iota(jnp.int32, (1, LANES), 1)
    head0 = lane < HEAD_DIM

    qi = lax.broadcasted_iota(jnp.int32, (ATTN_BLOCK, 2 * ATTN_BLOCK), 0)
    kj = lax.broadcasted_iota(jnp.int32, (ATTN_BLOCK, 2 * ATTN_BLOCK), 1)
    band = (kj >= qi) & (kj <= qi + ATTN_BLOCK)
    neg = jnp.float32(-jnp.inf)
    bias_s[0] = jnp.where(band, 0.0, neg)
    bias_s[1] = jnp.where(band & (kj >= ATTN_BLOCK), 0.0, neg)

    def rot(x):
        return (x * cos_ref[...] + pltpu.roll(x, LANES - ROT_DIM // 2, 1) * sa_ref[...]
                + pltpu.roll(x, ROT_DIM // 2, 1) * sb_ref[...])

    qn[...] = rot(q_ref[...]) * (HEAD_DIM ** -0.5)
    kn[...] = rot(k_ref[...])

    for d, (qc, kc, vc) in ((4, (q4, k4, v4)), (16, (q16, k16, v16))):
        ln = seq // d
        for r in range(d):
            qc[r * ln:(r + 1) * ln, :] = qn[pl.ds(r, ln, stride=d), :]
            kc[r * ln:(r + 1) * ln, :] = kn[pl.ds(r, ln, stride=d), :]
            vc[r * ln:(r + 1) * ln, :] = v_ref[pl.ds(r, ln, stride=d), :]

    unroll = 4

    def run_branch(qsrc, ksrc, vsrc, o_dst, l_dst, blocks_per_class):
        def attend(b, first):
            cur = pl.ds(pl.multiple_of(b * ATTN_BLOCK, ATTN_BLOCK), ATTN_BLOCK)
            qb = qsrc[cur, :]
            prv = pl.ds(pl.multiple_of(jnp.maximum(b - 1, 0) * ATTN_BLOCK, ATTN_BLOCK),
                        ATTN_BLOCK)
            bias = bias_s[int(first)] if isinstance(first, bool) else bias_s[first]
            kk = jnp.concatenate([ksrc[prv, :], ksrc[cur, :]], axis=0).astype(BF16)
            vv = jnp.concatenate([vsrc[prv, :], vsrc[cur, :]], axis=0)
            res, mx = [], []
            for h in range(2):
                mine = head0 if h == 0 else jnp.logical_not(head0)
                qh = jnp.where(mine, qb, 0.0).astype(BF16)
                s = lax.dot_general(qh, kk, (((1,), (1,)), ((), ())), preferred_element_type=F32)
                s = s + bias
                m = jnp.max(s, axis=-1, keepdims=True)
                p = jnp.exp(s - m).astype(BF16)
                vh = jnp.where(mine, vv, 1.0).astype(BF16)
                res.append(jnp.dot(p, vh, preferred_element_type=F32))
                mx.append(m)
            acc = jnp.where(head0, res[0], res[1])
            den = pltpu.roll(jnp.where(head0, res[1], res[0]), HEAD_DIM, 1)
            mm = jnp.where(head0, mx[0], mx[1])
            o_dst[cur, :] = acc / den
            l_dst[cur, :] = mm + jnp.log(den)

        def body(i, carry):
            for u in range(unroll):
                b = i * unroll + u
                if blocks_per_class == 1 or (u == 0 and blocks_per_class == unroll):
                    first = True
                elif u % blocks_per_class != 0:
                    first = False
                else:
                    first = jnp.where(jnp.asarray(b % blocks_per_class == 0), 1, 0)
                attend(b, first)
            return carry

        lax.fori_loop(0, nblk_total // unroll, body, 0)

    run_branch(qn, kn, v_ref, o1, l1, nblk_total)
    run_branch(q4, k4, v4, o4c, l4c, nblk_total // 4)
    run_branch(q16, k16, v16, o16c, l16c, nblk_total // 16)

    for d, oc, lc, on, lnat in ((4, o4c, l4c, o4, l4), (16, o16c, l16c, o16, l16)):
        ln = seq // d
        for r in range(d):
            on[pl.ds(r, ln, stride=d), :] = oc[r * ln:(r + 1) * ln, :]
            lnat[pl.ds(r, ln, stride=d), :] = lc[r * ln:(r + 1) * ln, :]

    la, lb, lc_ = l1[...], l4[...], l16[...]
    mx = jnp.maximum(jnp.maximum(la, lb), lc_)
    ea, eb, ec = jnp.exp(la - mx), jnp.exp(lb - mx), jnp.exp(lc_ - mx)
    o_ref[...] = (ea * o1[...] + eb * o4[...] + ec * o16[...]) / (ea + eb + ec)


def _rotary_tables(seq):
    half = ROT_DIM // 2
    inv_freq = ROPE_THETA ** (-jnp.arange(0, ROT_DIM, 2, dtype=F32) / ROT_DIM)
    ang = jnp.arange(seq, dtype=F32)[:, None] * inv_freq[None, :]
    cos, sin = jnp.cos(ang), jnp.sin(ang)
    pos = jnp.arange(LANES) % HEAD_DIM
    fidx = pos % half
    in_lo = pos < half
    in_hi = (pos >= half) & (pos < ROT_DIM)
    cos_t = jnp.where((in_lo | in_hi)[None, :], cos[:, fidx], 1.0)
    sa = jnp.where(in_lo[None, :], -sin[:, fidx], 0.0)
    sb = jnp.where(in_hi[None, :], sin[:, fidx], 0.0)
    return cos_t.astype(F32), sa.astype(F32), sb.astype(F32)


def _attention(proj3, conv_w_cols, attn_w_cols):
    b, seq, _ = proj3.shape
    assert seq % (ATTN_BLOCK * DILATIONS[-1]) == 0
    n_pairs = attn_w_cols // LANES
    qoff = 3 * conv_w_cols // LANES
    cos_t, sa, sb = _rotary_tables(seq)
    blk = lambda off: pl.BlockSpec((None, seq, LANES), lambda i, j: (i, 0, off + j))
    tab = pl.BlockSpec((seq, LANES), lambda i, j: (0, 0))
    big = pltpu.VMEM((seq, LANES), F32)
    return pl.pallas_call(
        functools.partial(_attn_kernel, seq=seq),
        grid=(b, n_pairs),
        in_specs=[blk(qoff), blk(qoff + n_pairs), blk(qoff + 2 * n_pairs), tab, tab, tab],
        out_specs=pl.BlockSpec((None, seq, LANES), lambda i, j: (i, 0, j)),
        out_shape=jax.ShapeDtypeStruct((b, seq, attn_w_cols), F32),
        scratch_shapes=[big] * 18 + [pltpu.VMEM((2, ATTN_BLOCK, 2 * ATTN_BLOCK), F32)],
        compiler_params=pltpu.CompilerParams(
            dimension_semantics=("arbitrary", "arbitrary"),
            vmem_limit_bytes=48 * 1024 * 1024),
        name="attention",
    )(proj3, proj3, proj3, cos_t, sa, sb)


def _out_proj_kernel(cx_ref, cb_ref, cc_ref, hx_ref, hc_ref, ya_ref, x_ref,
                     cw_ref, cg_ref, ag_ref, wo_ref, fg_ref, wr_ref, br_ref,
                     x1_ref, h2_ref, idx_ref, gate_ref, rank_ref, cnt_ref, carry,
                     *, tile, n_experts):
    first_tile_of_seq = pl.program_id(1) == 0
    first_step = (pl.program_id(0) == 0) & first_tile_of_seq

    @pl.when(first_step)
    def _():
        carry[...] = jnp.zeros_like(carry)

    u = cc_ref[...] * cx_ref[...]
    uh = jnp.where(first_tile_of_seq, 0.0, hc_ref[...] * hx_ref[...])
    row8 = lax.broadcasted_iota(jnp.int32, (SUBLANES, 1), 0)

    def shifted(k):
        r = pltpu.roll(u, k, 0)
        top = jnp.where(row8 < k, pltpu.roll(uh, k, 0), r[:SUBLANES])
        return jnp.concatenate([top, r[SUBLANES:]], axis=0)

    conv = cw_ref[2:3, :] * u + cw_ref[1:2, :] * shifted(1) + cw_ref[0:1, :] * shifted(2)
    y_conv = cb_ref[...] * conv

    mixed = jnp.concatenate(
        [_rms(y_conv, cg_ref[...]), _rms(ya_ref[...], ag_ref[...])], axis=-1).astype(BF16)
    x1 = x_ref[...] + jnp.dot(mixed, wo_ref[...], preferred_element_type=F32)
    x1_ref[...] = x1
    h2 = _rms(x1, fg_ref[...])
    d = h2.shape[-1]
    chunks = d // LANES
    for c in range(chunks):
        h2_ref[pl.ds(c, tile, stride=chunks), :] = h2[:, c * LANES:(c + 1) * LANES]

    logits = jnp.dot(h2, wr_ref[...], preferred_element_type=F32,
                     precision=lax.Precision.HIGHEST) + br_ref[...]
    eio = lax.broadcasted_iota(jnp.int32, (tile, n_experts), 1).astype(F32)
    work = logits
    vals, idxs = [], []
    for _ in range(TOP_K):
        m = jnp.max(work, axis=1, keepdims=True)
        ik = jnp.min(jnp.where(work == m, eio, float(n_experts)), axis=1, keepdims=True)
        vals.append(m)
        idxs.append(ik)
        work = jnp.where(eio == ik, -jnp.inf, work)
    exps = [jnp.exp(v - vals[0]) for v in vals]
    tot = exps[0] + exps[1] + exps[2] + exps[3]

    onehot = jnp.zeros((tile, n_experts), F32)
    for ik in idxs:
        onehot = onehot + (eio == ik).astype(F32)
    ri = lax.broadcasted_iota(jnp.int32, (tile, tile), 0)
    ci = lax.broadcasted_iota(jnp.int32, (tile, tile), 1)
    tri = (ci < ri).astype(BF16)
    before = jnp.dot(tri, onehot.astype(BF16), preferred_element_type=F32) + carry[...]
    carry[...] = carry[...] + jnp.sum(onehot, axis=0, keepdims=True)
    cnt_ref[...] = carry[...].astype(jnp.int32)

    lio = lax.broadcasted_iota(jnp.int32, (tile, LANES), 1)
    idx_out = jnp.zeros((tile, LANES), jnp.int32)
    gate_out = jnp.zeros((tile, LANES), F32)
    rank_out = jnp.zeros((tile, LANES), jnp.int32)
    for k in range(TOP_K):
        rk = jnp.sum(jnp.where(eio == idxs[k], before, 0.0), axis=1, keepdims=True)
        idx_out = jnp.where(lio == k, idxs[k].astype(jnp.int32), idx_out)
        gate_out = jnp.where(lio == k, exps[k] / tot, gate_out)
        rank_out = jnp.where(lio == k, rk.astype(jnp.int32), rank_out)
    idx_ref[...] = idx_out
    gate_ref[...] = gate_out
    rank_ref[...] = rank_out


def _out_proj(proj3, y_attn, x3, conv_w, conv_g, attn_g, w_out_bf16, ffn_g, w_router, b_router,
              tile=256):
    b, seq, d = x3.shape
    cw_cols = conv_w.shape[1]
    aw_cols = y_attn.shape[2]
    n_experts = w_router.shape[1]
    chunks = d // LANES
    tiles = seq // tile
    n = b * seq
    halo_blocks = tile // SUBLANES
    row_blk = lambda width, col: pl.BlockSpec((None, tile, width), lambda i, j: (i, j, col))
    halo = lambda col: pl.BlockSpec(
        (None, SUBLANES, cw_cols), lambda i, j: (i, jnp.maximum(j * halo_blocks - 1, 0), col))
    const = lambda shape: pl.BlockSpec(shape, lambda i, j: (0,) * len(shape))
    flat = lambda width: pl.BlockSpec((tile, width), lambda i, j: (i * tiles + j, 0))
    outs = pl.pallas_call(
        functools.partial(_out_proj_kernel, tile=tile, n_experts=n_experts),
        grid=(b, tiles),
        in_specs=[
            row_blk(cw_cols, 0), row_blk(cw_cols, 1), row_blk(cw_cols, 2), halo(0), halo(2),
            row_blk(aw_cols, 0), row_blk(d, 0),
            const((CONV_K, cw_cols)), const((1, cw_cols)), const((1, aw_cols)),
            const((d, d)), const((1, d)), const((d, n_experts)), const((1, n_experts)),
        ],
        out_specs=[
            row_blk(d, 0),
            pl.BlockSpec((tile * chunks, LANES), lambda i, j: (i * tiles + j, 0)),
            flat(LANES), flat(LANES), flat(LANES),
            const((1, n_experts)),
        ],
        out_shape=[
            jax.ShapeDtypeStruct((b, seq, d), F32),
            jax.ShapeDtypeStruct((n * chunks, LANES), F32),
            jax.ShapeDtypeStruct((n, LANES), jnp.int32),
            jax.ShapeDtypeStruct((n, LANES), F32),
            jax.ShapeDtypeStruct((n, LANES), jnp.int32),
            jax.ShapeDtypeStruct((1, n_experts), jnp.int32),
        ],
        scratch_shapes=[pltpu.VMEM((1, n_experts), F32)],
        compiler_params=pltpu.CompilerParams(
            dimension_semantics=("arbitrary", "arbitrary"),
            vmem_limit_bytes=48 * 1024 * 1024),
        name="out_proj",
    )(proj3, proj3, proj3, proj3, proj3, y_attn, x3, conv_w, conv_g, attn_g, w_out_bf16,
      ffn_g, w_router, b_router)
    return outs


def _row_copy(src_hbm, src_row, dst_hbm, dst_row, chunks, sem):
    return pltpu.make_async_copy(
        src_hbm.at[pl.ds(pl.multiple_of(src_row * chunks, chunks), chunks), :],
        dst_hbm.at[pl.ds(pl.multiple_of(dst_row * chunks, chunks), chunks), :],
        sem)


DMA_ISSUE_UNROLL = 2


def _dispatch_kernel(dest_ref, h2_ref, xs_in_hbm, xs_hbm, sem, *, tile, chunks):
    del xs_in_hbm

    def issue(g, carry):
        for u in range(DMA_ISSUE_UNROLL):
            t = g * DMA_ISSUE_UNROLL + u
            for k in range(TOP_K):
                _row_copy(h2_ref, t, xs_hbm, dest_ref[0, 0, t * TOP_K + k], chunks, sem).start()
        return carry

    lax.fori_loop(0, tile // DMA_ISSUE_UNROLL, issue, 0)
    rows = tile * TOP_K * chunks
    pltpu.make_async_copy(xs_hbm.at[pl.ds(0, rows), :], xs_hbm.at[pl.ds(0, rows), :], sem).wait()


def _dispatch(dest, h2_tm, n_rows, chunks, tile=256):
    steps = dest.shape[0] // (tile * TOP_K)
    dest3 = dest.reshape(steps, 1, tile * TOP_K)
    xs0 = jnp.zeros((n_rows * chunks, LANES), F32)
    return pl.pallas_call(
        functools.partial(_dispatch_kernel, tile=tile, chunks=chunks),
        grid=(steps,),
        in_specs=[
            pl.BlockSpec((1, 1, tile * TOP_K), lambda i: (i, 0, 0), memory_space=pltpu.SMEM),
            pl.BlockSpec((tile * chunks, LANES), lambda i: (i, 0)),
            pl.BlockSpec(memory_space=pl.ANY),
        ],
        out_specs=pl.BlockSpec(memory_space=pl.ANY),
        out_shape=jax.ShapeDtypeStruct(xs0.shape, F32),
        scratch_shapes=[pltpu.SemaphoreType.DMA(())],
        input_output_aliases={2: 0},
        compiler_params=pltpu.CompilerParams(dimension_semantics=("arbitrary",)),
        name="dispatch",
    )(dest3, h2_tm, xs0)


def _moe_kernel(item_e, item_row0, item_nblk, used_blocks, xs_hbm, wg_ref, wu_ref, bg_ref,
                bu_ref, wd_ref, bd_ref, ys_hbm, x_in, xb, yacc, wgb, wub, wdb, stage, sem_in,
                sem_out, *, chunks, n_col_steps, n_row_blocks):
    del item_e
    s = pl.program_id(0)
    j = pl.program_id(1)
    nb = item_nblk[s]
    row0 = item_row0[s]
    gran = MOE_ROW_GRAN

    def ys_block(blk):
        return ys_hbm.at[pl.ds(pl.multiple_of(blk * (gran * chunks), gran * chunks),
                               gran * chunks), :]

    @pl.when((s == pl.num_programs(0) - 1) & (j == n_col_steps - 1))
    def _():
        stage[0] = jnp.zeros(stage.shape[1:], stage.dtype)

        def fill(blk, carry):
            cp = pltpu.make_async_copy(stage.at[0], ys_block(blk), sem_out.at[0])
            cp.start()
            cp.wait()
            return carry

        lax.fori_loop(used_blocks[0], n_row_blocks, fill, 0)

    @pl.when(nb > 0)
    def _():
        @pl.when(j == 0)
        def _():
            def x_copy(sb, slot):
                return pltpu.make_async_copy(
                    xs_hbm.at[pl.ds(pl.multiple_of((row0 + sb * gran) * chunks, gran * chunks),
                                    gran * chunks), :], x_in.at[slot], sem_in.at[slot])

            x_copy(0, 0).start()

            def load(sb, carry):
                slot = sb % 2

                @pl.when(sb + 1 < nb)
                def _():
                    x_copy(sb + 1, 1 - slot).start()

                x_copy(sb, slot).wait()
                rows = pl.ds(pl.multiple_of(sb * gran, gran), gran)
                for c in range(chunks):
                    xb[rows, c * LANES:(c + 1) * LANES] = (
                        x_in[slot, pl.ds(c, gran, stride=chunks), :].astype(BF16))
                yacc[rows, :] = jnp.broadcast_to(bd_ref[...], (gran, yacc.shape[1]))
                return carry

            lax.fori_loop(0, nb, load, 0)

        wgb[...] = wg_ref[...].astype(BF16)
        wub[...] = wu_ref[...].astype(BF16)
        wdb[...] = wd_ref[...].astype(BF16)

        def block_rows(sb):
            return pl.ds(pl.multiple_of(sb * gran, gran), gran)

        def partial_out(sb):
            x = xb[block_rows(sb), :]
            gate = jnp.dot(x, wgb[...], preferred_element_type=F32) + bg_ref[...]
            up = jnp.dot(x, wub[...], preferred_element_type=F32) + bu_ref[...]
            gate = jnp.minimum(gate, SWIGLU_LIMIT)
            up = jnp.clip(up, -SWIGLU_LIMIT, SWIGLU_LIMIT)
            act = (up + 1.0) * (gate * jax.nn.sigmoid(SWIGLU_ALPHA * gate))
            return jnp.dot(act.astype(BF16), wdb[...], preferred_element_type=F32)

        def pair(i, carry):
            pa = partial_out(2 * i)
            pb = partial_out(2 * i + 1)
            yacc[block_rows(2 * i), :] += pa
            yacc[block_rows(2 * i + 1), :] += pb
            return carry

        lax.fori_loop(0, nb // 2, pair, 0)

        @pl.when(nb % 2 == 1)
        def _():
            yacc[block_rows(nb - 1), :] += partial_out(nb - 1)

        @pl.when(j == n_col_steps - 1)
        def _():
            def y_copy(sb, slot):
                return pltpu.make_async_copy(stage.at[slot], ys_block(row0 // gran + sb),
                                             sem_out.at[slot])

            def store(sb, carry):
                slot = sb % 2

                @pl.when(sb >= 2)
                def _():
                    y_copy(sb - 2, slot).wait()

                for c in range(chunks):
                    stage[slot, pl.ds(c, gran, stride=chunks), :] = (
                        yacc[block_rows(sb), c * LANES:(c + 1) * LANES])
                y_copy(sb, slot).start()
                return carry

            lax.fori_loop(0, nb, store, 0)

            @pl.when(nb >= 2)
            def _():
                y_copy(nb - 2, nb % 2).wait()

            y_copy(nb - 1, (nb - 1) % 2).wait()


def _moe(item_e, item_row0, item_nblk, used_blocks, xs, w_gate_up, b_gate_up, w_down, b_down,
         n_rows):
    n_experts, d, two_de = w_gate_up.shape
    de = two_de // 2
    chunks = d // LANES
    tn = MOE_COL_TILE
    n_col_steps = de // tn
    n_items = item_e.shape[0]
    gran = MOE_ROW_GRAN

    def col(j, nblk):
        return jnp.where(nblk > 0, j, n_col_steps - 1)

    bgu3 = b_gate_up.reshape(n_experts, 1, two_de)
    bd3 = b_down.reshape(n_experts, 1, d)
    grid_spec = pltpu.PrefetchScalarGridSpec(
        num_scalar_prefetch=4,
        grid=(n_items, n_col_steps),
        in_specs=[
            pl.BlockSpec(memory_space=pl.ANY),
            pl.BlockSpec((None, d, tn), lambda s, j, e, r, nb, ub: (e[s], 0, col(j, nb[s]))),
            pl.BlockSpec((None, d, tn),
                         lambda s, j, e, r, nb, ub: (e[s], 0, n_col_steps + col(j, nb[s]))),
            pl.BlockSpec((None, 1, tn), lambda s, j, e, r, nb, ub: (e[s], 0, col(j, nb[s]))),
            pl.BlockSpec((None, 1, tn),
                         lambda s, j, e, r, nb, ub: (e[s], 0, n_col_steps + col(j, nb[s]))),
            pl.BlockSpec((None, tn, d), lambda s, j, e, r, nb, ub: (e[s], col(j, nb[s]), 0)),
            pl.BlockSpec((None, 1, d), lambda s, j, e, r, nb, ub: (e[s], 0, 0)),
        ],
        out_specs=pl.BlockSpec(memory_space=pl.ANY),
        scratch_shapes=[
            pltpu.VMEM((2, gran * chunks, LANES), F32),
            pltpu.VMEM((MOE_ITEM_ROWS, d), BF16),
            pltpu.VMEM((MOE_ITEM_ROWS, d), F32),
            pltpu.VMEM((d, tn), BF16),
            pltpu.VMEM((d, tn), BF16),
            pltpu.VMEM((tn, d), BF16),
            pltpu.VMEM((2, gran * chunks, LANES), F32),
            pltpu.SemaphoreType.DMA((2,)),
            pltpu.SemaphoreType.DMA((2,)),
        ],
    )
    return pl.pallas_call(
        functools.partial(_moe_kernel, chunks=chunks, n_col_steps=n_col_steps,
                          n_row_blocks=n_rows // gran),
        grid_spec=grid_spec,
        out_shape=jax.ShapeDtypeStruct((n_rows * chunks, LANES), F32),
        compiler_params=pltpu.CompilerParams(
            dimension_semantics=("arbitrary", "arbitrary"),
            vmem_limit_bytes=56 * 1024 * 1024),
        name="moe",
    )(item_e, item_row0, item_nblk, used_blocks, xs, w_gate_up, w_gate_up, bgu3, bgu3, w_down,
      bd3)


def _combine_kernel(dest_ref, ys_hbm, x1_ref, gate_ref, fg_ref, o_ref, gbuf, sem,
                    *, tile, chunks, final_norm):
    i = pl.program_id(0)
    slot = i % 2
    per_tile = tile * TOP_K

    def gather(step, dst_slot):
        base = step * per_tile

        def issue(g, carry):
            for u in range(DMA_ISSUE_UNROLL * TOP_K):
                a = g * (DMA_ISSUE_UNROLL * TOP_K) + u
                src = pl.multiple_of(dest_ref[base + a] * chunks, chunks)
                dst = pl.multiple_of(a * chunks, chunks)
                pltpu.make_async_copy(ys_hbm.at[pl.ds(src, chunks), :],
                                      gbuf.at[dst_slot, pl.ds(dst, chunks), :],
                                      sem.at[dst_slot]).start()
            return carry

        lax.fori_loop(0, tile // DMA_ISSUE_UNROLL, issue, 0)

    @pl.when(i == 0)
    def _():
        gather(0, 0)

    @pl.when(i + 1 < pl.num_programs(0))
    def _():
        gather(i + 1, 1 - slot)

    pltpu.make_async_copy(ys_hbm.at[pl.ds(0, per_tile * chunks), :], gbuf.at[slot],
                          sem.at[slot]).wait()

    gates = gate_ref[...]
    gk = [gates[:, k:k + 1] for k in range(TOP_K)]
    cols = []
    for c in range(chunks):
        acc = x1_ref[:, c * LANES:(c + 1) * LANES]
        for k in range(TOP_K):
            acc = acc + gk[k] * gbuf[slot, pl.ds(k * chunks + c, tile, stride=TOP_K * chunks), :]
        cols.append(acc)
    out = jnp.concatenate(cols, axis=-1)
    if final_norm:
        out = _rms(out, fg_ref[...])
    o_ref[...] = out


def _combine(dest, ys, x1, gates, final_g, final_norm, tile=128):
    n, d = x1.shape
    chunks = d // LANES
    steps = n // tile
    grid_spec = pltpu.PrefetchScalarGridSpec(
        num_scalar_prefetch=1,
        grid=(steps,),
        in_specs=[
            pl.BlockSpec(memory_space=pl.ANY),
            pl.BlockSpec((tile, d), lambda i, dest: (i, 0)),
            pl.BlockSpec((tile, LANES), lambda i, dest: (i, 0)),
            pl.BlockSpec((1, d), lambda i, dest: (0, 0)),
        ],
        out_specs=pl.BlockSpec((tile, d), lambda i, dest: (i, 0)),
        scratch_shapes=[pltpu.VMEM((2, tile * TOP_K * chunks, LANES), F32),
                        pltpu.SemaphoreType.DMA((2,))],
    )
    return pl.pallas_call(
        functools.partial(_combine_kernel, tile=tile, chunks=chunks, final_norm=final_norm),
        grid_spec=grid_spec,
        out_shape=jax.ShapeDtypeStruct((n, d), F32),
        compiler_params=pltpu.CompilerParams(dimension_semantics=("arbitrary",)),
        name="combine",
    )(dest, ys, x1, gates, final_g)


def _routing_tables(counts, idx, rank, n_rows_cap):
    n_experts = counts.shape[0]
    gran, item_rows = MOE_ROW_GRAN, MOE_ITEM_ROWS
    padded = ((counts + gran - 1) // gran) * gran
    ends = jnp.cumsum(padded)
    offs = ends - padded
    dest = offs[idx] + rank
    items_per = (padded + item_rows - 1) // item_rows
    item_ends = jnp.cumsum(items_per)
    item_starts = item_ends - items_per
    n_items = n_experts + n_rows_cap // item_rows
    slot = jnp.arange(n_items, dtype=jnp.int32)
    total = item_ends[-1]
    live = slot < total
    live_slot = jnp.minimum(slot, total - 1)
    e_of = jnp.minimum(jnp.sum((live_slot[:, None] >= item_ends[None, :]).astype(jnp.int32), axis=1),
                       n_experts - 1).astype(jnp.int32)
    local = slot - item_starts[e_of]
    row0 = offs[e_of] + local * item_rows
    rows = jnp.clip(padded[e_of] - local * item_rows, 0, item_rows)
    nblk = jnp.where(live, rows // gran, 0).astype(jnp.int32)
    row0 = jnp.where(live, row0, 0).astype(jnp.int32)
    used_blocks = (ends[-1:] // gran).astype(jnp.int32)
    return dest.astype(jnp.int32), e_of, row0, nblk, used_blocks


def _layer(x3, mix_g, w_in, conv_w, conv_g, attn_g, w_out, ffn_g, w_router, b_router,
           w_gate_up, b_gate_up, w_down, b_down, final_g, final_norm):
    b, seq, d = x3.shape
    n = b * seq
    cw_cols = conv_w.shape[1]
    aw_cols = attn_g.shape[0]
    n_experts = w_router.shape[1]
    chunks = d // LANES

    proj = _in_proj(x3.reshape(n, d), mix_g.reshape(1, d), w_in.astype(BF16))
    proj3 = proj.reshape(b, seq, -1)
    y_attn = _attention(proj3, cw_cols, aw_cols)
    x1, h2_tm, idx, gates, rank, counts = _out_proj(
        proj3, y_attn, x3, conv_w, conv_g.reshape(1, -1), attn_g.reshape(1, -1),
        w_out.astype(BF16), ffn_g.reshape(1, d), w_router, b_router.reshape(1, -1))

    n_rows = n * TOP_K + n_experts * MOE_ROW_GRAN
    dest, item_e, item_row0, item_nblk, used_blocks = _routing_tables(
        counts[0], idx[:, :TOP_K], rank[:, :TOP_K], n_rows)
    dest_flat = dest.reshape(n * TOP_K)
    xs = _dispatch(dest_flat, h2_tm, n_rows, chunks)
    ys = _moe(item_e, item_row0, item_nblk, used_blocks, xs, w_gate_up, b_gate_up, w_down,
              b_down, n_rows)
    out = _combine(dest_flat, ys, x1.reshape(n, d), gates, final_g.reshape(1, d), final_norm)
    return out.reshape(b, seq, d)


def kernel(x, mix_norm_g, w_in, conv_w, conv_norm_g, attn_norm_g, w_out, ffn_norm_g, w_router,
           b_router, w_gate_up, b_gate_up, w_down, b_down, final_norm_g):
    depth = w_in.shape[0]
    for layer in range(depth):
        x = _layer(x, mix_norm_g[layer], w_in[layer], conv_w[layer], conv_norm_g[layer],
                   attn_norm_g[layer], w_out[layer], ffn_norm_g[layer], w_router[layer],
                   b_router[layer], w_gate_up[layer], b_gate_up[layer], w_down[layer],
                   b_down[layer], final_norm_g, layer == depth - 1)
    return x
```

```python
import functools

import jax
import jax.numpy as jnp
from jax import lax
from jax.experimental import pallas as pl
from jax.experimental.pallas import tpu as pltpu

HEAD_DIM = 64
CONV_K = 3
ROT_DIM = HEAD_DIM // 4
ROPE_THETA = 500000.0
DILATIONS = (1, 4, 16)
ATTN_BLOCK = 128
TOP_K = 4
SWIGLU_LIMIT = 7.0
SWIGLU_ALPHA = 1.702
EPS = 1e-5

LANES = 128
SUBLANES = 8

MOE_ROW_GRAN = 256
MOE_ITEM_ROWS = 1536
MOE_COL_TILE = 256
BF16 = jnp.bfloat16
F32 = jnp.float32


def _rms(x, g):
    return x * lax.rsqrt(jnp.mean(x * x, axis=-1, keepdims=True) + EPS) * g


def _in_proj_kernel(x_ref, g_ref, w_ref, o_ref, xn_ref):
    @pl.when(pl.program_id(1) == 0)
    def _():
        xn_ref[...] = _rms(x_ref[...], g_ref[...]).astype(BF16)

    o_ref[...] = jnp.dot(xn_ref[...], w_ref[...], preferred_element_type=F32)


def _in_proj(x2, g, w_bf16, tm=1024, tn=512):
    n, d = x2.shape
    cols = w_bf16.shape[1]
    return pl.pallas_call(
        _in_proj_kernel,
        grid=(n // tm, cols // tn),
        in_specs=[
            pl.BlockSpec((tm, d), lambda i, j: (i, 0)),
            pl.BlockSpec((1, d), lambda i, j: (0, 0)),
            pl.BlockSpec((d, tn), lambda i, j: (0, j)),
        ],
        out_specs=pl.BlockSpec((tm, tn), lambda i, j: (i, j)),
        out_shape=jax.ShapeDtypeStruct((n, cols), F32),
        scratch_shapes=[pltpu.VMEM((tm, d), BF16)],
        compiler_params=pltpu.CompilerParams(
            dimension_semantics=("arbitrary", "arbitrary"),
            vmem_limit_bytes=40 * 1024 * 1024),
        name="in_proj",
    )(x2, g, w_bf16)


def _attn_kernel(q_ref, k_ref, v_ref, cos_ref, sa_ref, sb_ref, o_ref,
                 qn, kn, q4, k4, v4, q16, k16, v16,
                 a1, d1, m1, a4, d4, m4, a16, d16, m16, acm, dcm, mcm, bias_s, *, seq):
    nblk_total = seq // ATTN_BLOCK
    lane = lax.broadcasted_iota(jnp.int32, (1, LANES), 1)
    head0 = lane < HEAD_DIM

    qi = lax.broadcasted_iota(jnp.int32, (ATTN_BLOCK, 2 * ATTN_BLOCK), 0)
    kj = lax.broadcasted_iota(jnp.int32, (ATTN_BLOCK, 2 * ATTN_BLOCK), 1)
    band = (kj >= qi) & (kj <= qi + ATTN_BLOCK)
    neg = jnp.float32(-jnp.inf)
    bias_s[0] = jnp.where(band, 0.0, neg)
    bias_s[1] = jnp.where(band & (kj >= ATTN_BLOCK), 0.0, neg)

    def rot(x):
        return (x * cos_ref[...] + pltpu.roll(x, LANES - ROT_DIM // 2, 1) * sa_ref[...]
                + pltpu.roll(x, ROT_DIM // 2, 1) * sb_ref[...])

    qn[...] = rot(q_ref[...]) * (HEAD_DIM ** -0.5)
    kn[...] = rot(k_ref[...])

    for d, (qc, kc, vc) in ((4, (q4, k4, v4)), (16, (q16, k16, v16))):
        ln = seq // d
        for r in range(d):
            qc[r * ln:(r + 1) * ln, :] = qn[pl.ds(r, ln, stride=d), :]
            kc[r * ln:(r + 1) * ln, :] = kn[pl.ds(r, ln, stride=d), :]
            vc[r * ln:(r + 1) * ln, :] = v_ref[pl.ds(r, ln, stride=d), :]

    unroll = 16

    def run_branch(qsrc, ksrc, vsrc, dsts, blocks_per_class):
        a_dst, d_dst, m_dst = dsts

        def attend(b, first):
            cur = pl.ds(pl.multiple_of(b * ATTN_BLOCK, ATTN_BLOCK), ATTN_BLOCK)
            qb = qsrc[cur, :]
            prv = pl.ds(pl.multiple_of(jnp.maximum(b - 1, 0) * ATTN_BLOCK, ATTN_BLOCK),
                        ATTN_BLOCK)
            bias = bias_s[int(first)] if isinstance(first, bool) else bias_s[first]
            kk = jnp.concatenate([ksrc[prv, :], ksrc[cur, :]], axis=0).astype(BF16)
            vv = jnp.concatenate([vsrc[prv, :], vsrc[cur, :]], axis=0)
            res, mx = [], []
            for h in range(2):
                mine = head0 if h == 0 else jnp.logical_not(head0)
                qh = jnp.where(mine, qb, 0.0).astype(BF16)
                s = lax.dot_general(qh, kk, (((1,), (1,)), ((), ())), preferred_element_type=F32)
                s = s + bias
                m = jnp.max(s, axis=-1, keepdims=True)
                p = jnp.exp(s - m).astype(BF16)
                vh = jnp.where(mine, vv, 1.0).astype(BF16)
                res.append(jnp.dot(p, vh, preferred_element_type=F32))
                mx.append(m)
            a_dst[cur, :] = jnp.where(head0, res[0], res[1])
            d_dst[cur, :] = pltpu.roll(jnp.where(head0, res[1], res[0]), HEAD_DIM, 1)
            m_dst[cur, :] = jnp.where(head0, mx[0], mx[1])

        def body(i, carry):
            for u in range(unroll):
                b = i * unroll + u
                if unroll % blocks_per_class == 0:
                    first = u % blocks_per_class == 0
                elif blocks_per_class % unroll == 0 and u != 0:
                    first = False
                else:
                    first = jnp.where(jnp.asarray(b % blocks_per_class == 0), 1, 0)
                attend(b, first)
            return carry

        lax.fori_loop(0, nblk_total // unroll, body, 0)

    res1, res4, res16, res_cm = (a1, d1, m1), (a4, d4, m4), (a16, d16, m16), (acm, dcm, mcm)
    run_branch(qn, kn, v_ref, res1, nblk_total)
    for d, srcs, res in ((4, (q4, k4, v4), res4), (16, (q16, k16, v16), res16)):
        run_branch(*srcs, res_cm, nblk_total // d)
        ln = seq // d
        for cm, nat in zip(res_cm, res):
            for r in range(d):
                nat[pl.ds(r, ln, stride=d), :] = cm[r * ln:(r + 1) * ln, :]

    ma, mb, mc = m1[...], m4[...], m16[...]
    mx = jnp.maximum(jnp.maximum(ma, mb), mc)
    ea, eb, ec = jnp.exp(ma - mx), jnp.exp(mb - mx), jnp.exp(mc - mx)
    o_ref[...] = ((ea * a1[...] + eb * a4[...] + ec * a16[...])
                  / (ea * d1[...] + eb * d4[...] + ec * d16[...]))


def _rotary_tables(seq):
    half = ROT_DIM // 2
    inv_freq = ROPE_THETA ** (-jnp.arange(0, ROT_DIM, 2, dtype=F32) / ROT_DIM)
    ang = jnp.arange(seq, dtype=F32)[:, None] * inv_freq[None, :]
    cos, sin = jnp.cos(ang), jnp.sin(ang)
    pos = jnp.arange(LANES) % HEAD_DIM
    fidx = pos % half
    in_lo = pos < half
    in_hi = (pos >= half) & (pos < ROT_DIM)
    cos_t = jnp.where((in_lo | in_hi)[None, :], cos[:, fidx], 1.0)
    sa = jnp.where(in_lo[None, :], -sin[:, fidx], 0.0)
    sb = jnp.where(in_hi[None, :], sin[:, fidx], 0.0)
    return cos_t.astype(F32), sa.astype(F32), sb.astype(F32)


def _attention(proj3, conv_w_cols, attn_w_cols):
    b, seq, _ = proj3.shape
    assert seq % (ATTN_BLOCK * DILATIONS[-1]) == 0
    n_pairs = attn_w_cols // LANES
    qoff = 3 * conv_w_cols // LANES
    cos_t, sa, sb = _rotary_tables(seq)
    blk = lambda off: pl.BlockSpec((None, seq, LANES), lambda i, j: (i, 0, off + j))
    tab = pl.BlockSpec((seq, LANES), lambda i, j: (0, 0))
    big = pltpu.VMEM((seq, LANES), F32)
    return pl.pallas_call(
        functools.partial(_attn_kernel, seq=seq),
        grid=(b, n_pairs),
        in_specs=[blk(qoff), blk(qoff + n_pairs), blk(qoff + 2 * n_pairs), tab, tab, tab],
        out_specs=pl.BlockSpec((None, seq, LANES), lambda i, j: (i, 0, j)),
        out_shape=jax.ShapeDtypeStruct((b, seq, attn_w_cols), F32),
        scratch_shapes=[big] * 20 + [pltpu.VMEM((2, ATTN_BLOCK, 2 * ATTN_BLOCK), F32)],
        compiler_params=pltpu.CompilerParams(
            dimension_semantics=("arbitrary", "arbitrary"),
            vmem_limit_bytes=48 * 1024 * 1024),
        name="attention",
    )(proj3, proj3, proj3, cos_t, sa, sb)


def _out_proj_kernel(cx_ref, cb_ref, cc_ref, hx_ref, hc_ref, ya_ref, x_ref,
                     cw_ref, cg_ref, ag_ref, wo_ref, fg_ref, wrh_ref, wrl_ref, br_ref,
                     x1_ref, h2_ref, idx_ref, gate_ref, rank_ref, cnt_ref, carry,
                     *, tile, n_experts):
    first_tile_of_seq = pl.program_id(1) == 0
    first_step = (pl.program_id(0) == 0) & first_tile_of_seq

    @pl.when(first_step)
    def _():
        carry[...] = jnp.zeros_like(carry)

    u = cc_ref[...] * cx_ref[...]
    uh = jnp.where(first_tile_of_seq, 0.0, hc_ref[...] * hx_ref[...])
    row8 = lax.broadcasted_iota(jnp.int32, (SUBLANES, 1), 0)

    def shifted(k):
        r = pltpu.roll(u, k, 0)
        top = jnp.where(row8 < k, pltpu.roll(uh, k, 0), r[:SUBLANES])
        return jnp.concatenate([top, r[SUBLANES:]], axis=0)

    conv = cw_ref[2:3, :] * u + cw_ref[1:2, :] * shifted(1) + cw_ref[0:1, :] * shifted(2)
    y_conv = cb_ref[...] * conv

    mixed = jnp.concatenate(
        [_rms(y_conv, cg_ref[...]), _rms(ya_ref[...], ag_ref[...])], axis=-1).astype(BF16)
    x1 = x_ref[...] + jnp.dot(mixed, wo_ref[...], preferred_element_type=F32)
    x1_ref[...] = x1
    h2 = _rms(x1, fg_ref[...])
    d = h2.shape[-1]
    chunks = d // LANES
    for c in range(chunks):
        h2_ref[pl.ds(c, tile, stride=chunks), :] = h2[:, c * LANES:(c + 1) * LANES]

    h2_hi = h2.astype(BF16)
    h2_lo = (h2 - h2_hi.astype(F32)).astype(BF16)
    logits = (jnp.dot(h2_hi, wrh_ref[...], preferred_element_type=F32)
              + jnp.dot(h2_hi, wrl_ref[...], preferred_element_type=F32)
              + jnp.dot(h2_lo, wrh_ref[...], preferred_element_type=F32)) + br_ref[...]
    eio = lax.broadcasted_iota(jnp.int32, (tile, n_experts), 1).astype(F32)
    work = logits
    vals, idxs = [], []
    for _ in range(TOP_K):
        m = jnp.max(work, axis=1, keepdims=True)
        ik = jnp.min(jnp.where(work == m, eio, float(n_experts)), axis=1, keepdims=True)
        vals.append(m)
        idxs.append(ik)
        work = jnp.where(eio == ik, -jnp.inf, work)
    exps = [jnp.exp(v - vals[0]) for v in vals]
    tot = exps[0] + exps[1] + exps[2] + exps[3]

    onehot = jnp.zeros((tile, n_experts), F32)
    for ik in idxs:
        onehot = onehot + (eio == ik).astype(F32)
    ri = lax.broadcasted_iota(jnp.int32, (tile, tile), 0)
    ci = lax.broadcasted_iota(jnp.int32, (tile, tile), 1)
    tri = (ci < ri).astype(BF16)
    before = jnp.dot(tri, onehot.astype(BF16), preferred_element_type=F32) + carry[...]
    carry[...] = carry[...] + jnp.sum(onehot, axis=0, keepdims=True)
    cnt_ref[...] = carry[...].astype(jnp.int32)

    lio = lax.broadcasted_iota(jnp.int32, (tile, LANES), 1)
    idx_out = jnp.zeros((tile, LANES), jnp.int32)
    gate_out = jnp.zeros((tile, LANES), F32)
    rank_out = jnp.zeros((tile, LANES), jnp.int32)
    for k in range(TOP_K):
        rk = jnp.sum(jnp.where(eio == idxs[k], before, 0.0), axis=1, keepdims=True)
        idx_out = jnp.where(lio == k, idxs[k].astype(jnp.int32), idx_out)
        gate_out = jnp.where(lio == k, exps[k] / tot, gate_out)
        rank_out = jnp.where(lio == k, rk.astype(jnp.int32), rank_out)
    idx_ref[...] = idx_out
    gate_ref[...] = gate_out
    rank_ref[...] = rank_out


def _out_proj(proj3, y_attn, x3, conv_w, conv_g, attn_g, w_out_bf16, ffn_g, w_router, b_router,
              tile=256):
    b, seq, d = x3.shape
    cw_cols = conv_w.shape[1]
    aw_cols = y_attn.shape[2]
    n_experts = w_router.shape[1]
    w_router_hi = w_router.astype(BF16)
    w_router_lo = (w_router - w_router_hi.astype(F32)).astype(BF16)
    chunks = d // LANES
    tiles = seq // tile
    n = b * seq
    halo_blocks = tile // SUBLANES
    row_blk = lambda width, col: pl.BlockSpec((None, tile, width), lambda i, j: (i, j, col))
    halo = lambda col: pl.BlockSpec(
        (None, SUBLANES, cw_cols), lambda i, j: (i, jnp.maximum(j * halo_blocks - 1, 0), col))
    const = lambda shape: pl.BlockSpec(shape, lambda i, j: (0,) * len(shape))
    flat = lambda width: pl.BlockSpec((tile, width), lambda i, j: (i * tiles + j, 0))
    outs = pl.pallas_call(
        functools.partial(_out_proj_kernel, tile=tile, n_experts=n_experts),
        grid=(b, tiles),
        in_specs=[
            row_blk(cw_cols, 0), row_blk(cw_cols, 1), row_blk(cw_cols, 2), halo(0), halo(2),
            row_blk(aw_cols, 0), row_blk(d, 0),
            const((CONV_K, cw_cols)), const((1, cw_cols)), const((1, aw_cols)),
            const((d, d)), const((1, d)), const((d, n_experts)), const((d, n_experts)),
            const((1, n_experts)),
        ],
        out_specs=[
            row_blk(d, 0),
            pl.BlockSpec((tile * chunks, LANES), lambda i, j: (i * tiles + j, 0)),
            flat(LANES), flat(LANES), flat(LANES),
            const((1, n_experts)),
        ],
        out_shape=[
            jax.ShapeDtypeStruct((b, seq, d), F32),
            jax.ShapeDtypeStruct((n * chunks, LANES), F32),
            jax.ShapeDtypeStruct((n, LANES), jnp.int32),
            jax.ShapeDtypeStruct((n, LANES), F32),
            jax.ShapeDtypeStruct((n, LANES), jnp.int32),
            jax.ShapeDtypeStruct((1, n_experts), jnp.int32),
        ],
        scratch_shapes=[pltpu.VMEM((1, n_experts), F32)],
        compiler_params=pltpu.CompilerParams(
            dimension_semantics=("arbitrary", "arbitrary"),
            vmem_limit_bytes=48 * 1024 * 1024),
        name="out_proj",
    )(proj3, proj3, proj3, proj3, proj3, y_attn, x3, conv_w, conv_g, attn_g, w_out_bf16,
      ffn_g, w_router_hi, w_router_lo, b_router)
    return outs


def _row_copy(src_hbm, src_row, dst_hbm, dst_row, chunks, sem):
    return pltpu.make_async_copy(
        src_hbm.at[pl.ds(pl.multiple_of(src_row * chunks, chunks), chunks), :],
        dst_hbm.at[pl.ds(pl.multiple_of(dst_row * chunks, chunks), chunks), :],
        sem)


DMA_ISSUE_UNROLL = 2


def _dispatch_kernel(dest_ref, pad_start, pad_rows, used_blocks, h2_ref, xs_hbm, zbuf, sem, zsem,
                     *, tile, chunks, n_experts, n_row_blocks):
    gran = MOE_ROW_GRAN

    @pl.when(pl.program_id(0) == 0)
    def _():
        zbuf[...] = jnp.zeros_like(zbuf)

        def zero_rows(first_row, n_rows):
            cp = pltpu.make_async_copy(
                zbuf.at[pl.ds(0, n_rows * chunks), :],
                xs_hbm.at[pl.ds(pl.multiple_of(first_row * chunks, chunks), n_rows * chunks), :],
                zsem)
            cp.start()
            cp.wait()

        def per_expert(e, carry):
            row = pad_start[e]
            left = pad_rows[e]
            for bit in reversed(range(gran.bit_length() - 1)):
                take = (left >> bit) & 1

                @pl.when(take == 1)
                def _():
                    zero_rows(row, 1 << bit)

                row = row + (take << bit)
            return carry

        lax.fori_loop(0, n_experts, per_expert, 0)

        def tail(blk, carry):
            zero_rows(blk * gran, gran)
            return carry

        lax.fori_loop(used_blocks[0], n_row_blocks, tail, 0)

    def issue(g, carry):
        for u in range(DMA_ISSUE_UNROLL):
            t = g * DMA_ISSUE_UNROLL + u
            for k in range(TOP_K):
                _row_copy(h2_ref, t, xs_hbm, dest_ref[0, 0, t * TOP_K + k], chunks, sem).start()
        return carry

    lax.fori_loop(0, tile // DMA_ISSUE_UNROLL, issue, 0)
    rows = tile * TOP_K * chunks
    pltpu.make_async_copy(xs_hbm.at[pl.ds(0, rows), :], xs_hbm.at[pl.ds(0, rows), :], sem).wait()


def _dispatch(dest, pad_start, pad_rows, used_blocks, h2_tm, n_rows, chunks, tile=256):
    steps = dest.shape[0] // (tile * TOP_K)
    dest3 = dest.reshape(steps, 1, tile * TOP_K)
    gran = MOE_ROW_GRAN
    smem = pl.BlockSpec(memory_space=pltpu.SMEM)
    return pl.pallas_call(
        functools.partial(_dispatch_kernel, tile=tile, chunks=chunks,
                          n_experts=pad_start.shape[0], n_row_blocks=n_rows // gran),
        grid=(steps,),
        in_specs=[
            pl.BlockSpec((1, 1, tile * TOP_K), lambda i: (i, 0, 0), memory_space=pltpu.SMEM),
            smem, smem, smem,
            pl.BlockSpec((tile * chunks, LANES), lambda i: (i, 0)),
        ],
        out_specs=pl.BlockSpec(memory_space=pl.ANY),
        out_shape=jax.ShapeDtypeStruct((n_rows * chunks, LANES), F32),
        scratch_shapes=[pltpu.VMEM((gran * chunks, LANES), F32),
                        pltpu.SemaphoreType.DMA(()), pltpu.SemaphoreType.DMA(())],
        compiler_params=pltpu.CompilerParams(dimension_semantics=("arbitrary",)),
        name="dispatch",
    )(dest3, pad_start, pad_rows, used_blocks, h2_tm)


def _moe_kernel(item_e, item_row0, item_nblk, used_blocks, xs_hbm, wg_ref, wu_ref, bg_ref,
                bu_ref, wd_ref, bd_ref, ys_hbm, x_in, xb, yacc, wgb, wub, wdb, stage, sem_in,
                sem_out, *, chunks, n_col_steps, n_row_blocks):
    del item_e
    s = pl.program_id(0)
    j = pl.program_id(1)
    nb = item_nblk[s]
    row0 = item_row0[s]
    gran = MOE_ROW_GRAN

    def ys_block(blk):
        return ys_hbm.at[pl.ds(pl.multiple_of(blk * (gran * chunks), gran * chunks),
                               gran * chunks), :]

    @pl.when((s == pl.num_programs(0) - 1) & (j == n_col_steps - 1))
    def _():
        stage[0] = jnp.zeros(stage.shape[1:], stage.dtype)

        def fill(blk, carry):
            cp = pltpu.make_async_copy(stage.at[0], ys_block(blk), sem_out.at[0])
            cp.start()
            cp.wait()
            return carry

        lax.fori_loop(used_blocks[0], n_row_blocks, fill, 0)

    @pl.when(nb > 0)
    def _():
        @pl.when(j == 0)
        def _():
            def x_copy(sb, slot):
                return pltpu.make_async_copy(
                    xs_hbm.at[pl.ds(pl.multiple_of((row0 + sb * gran) * chunks, gran * chunks),
                                    gran * chunks), :], x_in.at[slot], sem_in.at[slot])

            x_copy(0, 0).start()

            def load(sb, carry):
                slot = sb % 2

                @pl.when(sb + 1 < nb)
                def _():
                    x_copy(sb + 1, 1 - slot).start()

                x_copy(sb, slot).wait()
                rows = pl.ds(pl.multiple_of(sb * gran, gran), gran)
                for c in range(chunks):
                    xb[rows, c * LANES:(c + 1) * LANES] = (
                        x_in[slot, pl.ds(c, gran, stride=chunks), :].astype(BF16))
                yacc[rows, :] = jnp.broadcast_to(bd_ref[...], (gran, yacc.shape[1]))
                return carry

            lax.fori_loop(0, nb, load, 0)

        wgb[...] = wg_ref[...].astype(BF16)
        wub[...] = wu_ref[...].astype(BF16)
        wdb[...] = wd_ref[...].astype(BF16)

        def block_rows(sb):
            return pl.ds(pl.multiple_of(sb * gran, gran), gran)

        def partial_out(sb):
            x = xb[block_rows(sb), :]
            gate = jnp.dot(x, wgb[...], preferred_element_type=F32) + bg_ref[...]
            up = jnp.dot(x, wub[...], preferred_element_type=F32) + bu_ref[...]
            gate = jnp.minimum(gate, SWIGLU_LIMIT)
            up = jnp.clip(up, -SWIGLU_LIMIT, SWIGLU_LIMIT)
            act = (up + 1.0) * (gate * jax.nn.sigmoid(SWIGLU_ALPHA * gate))
            return jnp.dot(act.astype(BF16), wdb[...], preferred_element_type=F32)

        def pair(i, carry):
            pa = partial_out(2 * i)
            pb = partial_out(2 * i + 1)
            yacc[block_rows(2 * i), :] += pa
            yacc[block_rows(2 * i + 1), :] += pb
            return carry

        lax.fori_loop(0, nb // 2, pair, 0)

        @pl.when(nb % 2 == 1)
        def _():
            yacc[block_rows(nb - 1), :] += partial_out(nb - 1)

        @pl.when(j == n_col_steps - 1)
        def _():
            def y_copy(sb, slot):
                return pltpu.make_async_copy(stage.at[slot], ys_block(row0 // gran + sb),
                                             sem_out.at[slot])

            def store(sb, carry):
                slot = sb % 2

                @pl.when(sb >= 2)
                def _():
                    y_copy(sb - 2, slot).wait()

                for c in range(chunks):
                    stage[slot, pl.ds(c, gran, stride=chunks), :] = (
                        yacc[block_rows(sb), c * LANES:(c + 1) * LANES])
                y_copy(sb, slot).start()
                return carry

            lax.fori_loop(0, nb, store, 0)

            @pl.when(nb >= 2)
            def _():
                y_copy(nb - 2, nb % 2).wait()

            y_copy(nb - 1, (nb - 1) % 2).wait()


def _moe(item_e, item_row0, item_nblk, used_blocks, live_items, xs, w_gate_up, b_gate_up, w_down,
         b_down, n_rows):
    n_experts, d, two_de = w_gate_up.shape
    de = two_de // 2
    chunks = d // LANES
    tn = MOE_COL_TILE
    n_col_steps = de // tn
    n_items = item_e.shape[0]
    gran = MOE_ROW_GRAN

    def col(j, nblk):
        return jnp.where(nblk > 0, j, n_col_steps - 1)

    bgu3 = b_gate_up.reshape(n_experts, 1, two_de)
    bd3 = b_down.reshape(n_experts, 1, d)
    grid_spec = pltpu.PrefetchScalarGridSpec(
        num_scalar_prefetch=4,
        grid=(live_items, n_col_steps),
        in_specs=[
            pl.BlockSpec(memory_space=pl.ANY),
            pl.BlockSpec((None, d, tn), lambda s, j, e, r, nb, ub: (e[s], 0, col(j, nb[s]))),
            pl.BlockSpec((None, d, tn),
                         lambda s, j, e, r, nb, ub: (e[s], 0, n_col_steps + col(j, nb[s]))),
            pl.BlockSpec((None, 1, tn), lambda s, j, e, r, nb, ub: (e[s], 0, col(j, nb[s]))),
            pl.BlockSpec((None, 1, tn),
                         lambda s, j, e, r, nb, ub: (e[s], 0, n_col_steps + col(j, nb[s]))),
            pl.BlockSpec((None, tn, d), lambda s, j, e, r, nb, ub: (e[s], col(j, nb[s]), 0)),
            pl.BlockSpec((None, 1, d), lambda s, j, e, r, nb, ub: (e[s], 0, 0)),
        ],
        out_specs=pl.BlockSpec(memory_space=pl.ANY),
        scratch_shapes=[
            pltpu.VMEM((2, gran * chunks, LANES), F32),
            pltpu.VMEM((MOE_ITEM_ROWS, d), BF16),
            pltpu.VMEM((MOE_ITEM_ROWS, d), F32),
            pltpu.VMEM((d, tn), BF16),
            pltpu.VMEM((d, tn), BF16),
            pltpu.VMEM((tn, d), BF16),
            pltpu.VMEM((2, gran * chunks, LANES), F32),
            pltpu.SemaphoreType.DMA((2,)),
            pltpu.SemaphoreType.DMA((2,)),
        ],
    )
    return pl.pallas_call(
        functools.partial(_moe_kernel, chunks=chunks, n_col_steps=n_col_steps,
                          n_row_blocks=n_rows // gran),
        grid_spec=grid_spec,
        out_shape=jax.ShapeDtypeStruct((n_rows * chunks, LANES), F32),
        compiler_params=pltpu.CompilerParams(
            dimension_semantics=("arbitrary", "arbitrary"),
            vmem_limit_bytes=56 * 1024 * 1024),
        name="moe",
    )(item_e, item_row0, item_nblk, used_blocks, xs, w_gate_up, w_gate_up, bgu3, bgu3, w_down,
      bd3)


def _combine_kernel(dest_ref, ys_hbm, x1_ref, gate_ref, fg_ref, o_ref, gbuf, sem,
                    *, tile, chunks, final_norm):
    i = pl.program_id(0)
    slot = i % 2
    per_tile = tile * TOP_K

    def gather(step, dst_slot):
        base = step * per_tile

        def issue(g, carry):
            for u in range(DMA_ISSUE_UNROLL * TOP_K):
                a = g * (DMA_ISSUE_UNROLL * TOP_K) + u
                src = pl.multiple_of(dest_ref[base + a] * chunks, chunks)
                dst = pl.multiple_of(a * chunks, chunks)
                pltpu.make_async_copy(ys_hbm.at[pl.ds(src, chunks), :],
                                      gbuf.at[dst_slot, pl.ds(dst, chunks), :],
                                      sem.at[dst_slot]).start()
            return carry

        lax.fori_loop(0, tile // DMA_ISSUE_UNROLL, issue, 0)

    @pl.when(i == 0)
    def _():
        gather(0, 0)

    @pl.when(i + 1 < pl.num_programs(0))
    def _():
        gather(i + 1, 1 - slot)

    pltpu.make_async_copy(ys_hbm.at[pl.ds(0, per_tile * chunks), :], gbuf.at[slot],
                          sem.at[slot]).wait()

    gates = gate_ref[...]
    gk = [gates[:, k:k + 1] for k in range(TOP_K)]
    cols = []
    for c in range(chunks):
        acc = x1_ref[:, c * LANES:(c + 1) * LANES]
        for k in range(TOP_K):
            acc = acc + gk[k] * gbuf[slot, pl.ds(k * chunks + c, tile, stride=TOP_K * chunks), :]
        cols.append(acc)
    out = jnp.concatenate(cols, axis=-1)
    if final_norm:
        out = _rms(out, fg_ref[...])
    o_ref[...] = out


def _combine(dest, ys, x1, gates, final_g, final_norm, tile=128):
    n, d = x1.shape
    chunks = d // LANES
    steps = n // tile
    grid_spec = pltpu.PrefetchScalarGridSpec(
        num_scalar_prefetch=1,
        grid=(steps,),
        in_specs=[
            pl.BlockSpec(memory_space=pl.ANY),
            pl.BlockSpec((tile, d), lambda i, dest: (i, 0)),
            pl.BlockSpec((tile, LANES), lambda i, dest: (i, 0)),
            pl.BlockSpec((1, d), lambda i, dest: (0, 0)),
        ],
        out_specs=pl.BlockSpec((tile, d), lambda i, dest: (i, 0)),
        scratch_shapes=[pltpu.VMEM((2, tile * TOP_K * chunks, LANES), F32),
                        pltpu.SemaphoreType.DMA((2,))],
    )
    return pl.pallas_call(
        functools.partial(_combine_kernel, tile=tile, chunks=chunks, final_norm=final_norm),
        grid_spec=grid_spec,
        out_shape=jax.ShapeDtypeStruct((n, d), F32),
        compiler_params=pltpu.CompilerParams(dimension_semantics=("arbitrary",)),
        name="combine",
    )(dest, ys, x1, gates, final_g)


def _routing_tables(counts, idx, rank, n_rows_cap):
    n_experts = counts.shape[0]
    gran, item_rows = MOE_ROW_GRAN, MOE_ITEM_ROWS
    padded = ((counts + gran - 1) // gran) * gran
    ends = jnp.cumsum(padded)
    offs = ends - padded
    dest = offs[idx] + rank
    items_per = (padded + item_rows - 1) // item_rows
    item_ends = jnp.cumsum(items_per)
    item_starts = item_ends - items_per
    n_items = n_experts + n_rows_cap // item_rows
    slot = jnp.arange(n_items, dtype=jnp.int32)
    total = item_ends[-1]
    live = slot < total
    live_slot = jnp.minimum(slot, total - 1)
    e_of = jnp.minimum(jnp.sum((live_slot[:, None] >= item_ends[None, :]).astype(jnp.int32), axis=1),
                       n_experts - 1).astype(jnp.int32)
    local = slot - item_starts[e_of]
    row0 = offs[e_of] + local * item_rows
    rows = jnp.clip(padded[e_of] - local * item_rows, 0, item_rows)
    nblk = jnp.where(live, rows // gran, 0).astype(jnp.int32)
    row0 = jnp.where(live, row0, 0).astype(jnp.int32)
    used_blocks = (ends[-1:] // gran).astype(jnp.int32)
    pad_start = (offs + counts).astype(jnp.int32)
    pad_rows = (padded - counts).astype(jnp.int32)
    return (dest.astype(jnp.int32), e_of, row0, nblk, used_blocks, total.astype(jnp.int32),
            pad_start, pad_rows)


def _layer(x3, mix_g, w_in, conv_w, conv_g, attn_g, w_out, ffn_g, w_router, b_router,
           w_gate_up, b_gate_up, w_down, b_down, final_g, final_norm):
    b, seq, d = x3.shape
    n = b * seq
    cw_cols = conv_w.shape[1]
    aw_cols = attn_g.shape[0]
    n_experts = w_router.shape[1]
    chunks = d // LANES

    proj = _in_proj(x3.reshape(n, d), mix_g.reshape(1, d), w_in.astype(BF16))
    proj3 = proj.reshape(b, seq, -1)
    y_attn = _attention(proj3, cw_cols, aw_cols)
    x1, h2_tm, idx, gates, rank, counts = _out_proj(
        proj3, y_attn, x3, conv_w, conv_g.reshape(1, -1), attn_g.reshape(1, -1),
        w_out.astype(BF16), ffn_g.reshape(1, d), w_router, b_router.reshape(1, -1))

    n_rows = n * TOP_K + n_experts * MOE_ROW_GRAN
    (dest, item_e, item_row0, item_nblk, used_blocks, live_items, pad_start,
     pad_rows) = _routing_tables(counts[0], idx[:, :TOP_K], rank[:, :TOP_K], n_rows)
    dest_flat = dest.reshape(n * TOP_K)
    xs = _dispatch(dest_flat, pad_start, pad_rows, used_blocks, h2_tm, n_rows, chunks)
    ys = _moe(item_e, item_row0, item_nblk, used_blocks, live_items, xs, w_gate_up, b_gate_up,
              w_down, b_down, n_rows)
    out = _combine(dest_flat, ys, x1.reshape(n, d), gates, final_g.reshape(1, d), final_norm)
    return out.reshape(b, seq, d)


def kernel(x, mix_norm_g, w_in, conv_w, conv_norm_g, attn_norm_g, w_out, ffn_norm_g, w_router,
           b_router, w_gate_up, b_gate_up, w_down, b_down, final_norm_g):
    depth = w_in.shape[0]
    for layer in range(depth):
        x = _layer(x, mix_norm_g[layer], w_in[layer], conv_w[layer], conv_norm_g[layer],
                   attn_norm_g[layer], w_out[layer], ffn_norm_g[layer], w_router[layer],
                   b_router[layer], w_gate_up[layer], b_gate_up[layer], w_down[layer],
                   b_down[layer], final_norm_g, layer == depth - 1)
    return x
```

```python
import functools

import jax
import jax.numpy as jnp
from jax import lax
from jax.experimental import pallas as pl
from jax.experimental.pallas import tpu as pltpu

HEAD_DIM = 64
CONV_K = 3
ROT_DIM = HEAD_DIM // 4
ROPE_THETA = 500000.0
DILATIONS = (1, 4, 16)
ATTN_BLOCK = 128
TOP_K = 4
SWIGLU_LIMIT = 7.0
SWIGLU_ALPHA = 1.702
EPS = 1e-5

LANES = 128
SUBLANES = 8

MOE_ROW_GRAN = 256
MOE_ITEM_ROWS = 1536
MOE_COL_TILE = 256
BF16 = jnp.bfloat16
F32 = jnp.float32


def _token_row_stride(chunks):
    return chunks + SUBLANES


def _token_rows(ref, first_token, n_tokens, chunks):
    return ref.at[pl.ds(pl.multiple_of(first_token * chunks, chunks), n_tokens * chunks), :]


def _rms(x, g):
    return x * lax.rsqrt(jnp.mean(x * x, axis=-1, keepdims=True) + EPS) * g


def _in_proj_kernel(x_ref, g_ref, w_ref, o_ref, xn_ref):
    @pl.when(pl.program_id(1) == 0)
    def _():
        xn_ref[...] = _rms(x_ref[...], g_ref[...]).astype(BF16)

    o_ref[...] = jnp.dot(xn_ref[...], w_ref[...], preferred_element_type=F32)


def _in_proj(x2, g, w_bf16, tm=1024, tn=512):
    n, d = x2.shape
    cols = w_bf16.shape[1]
    return pl.pallas_call(
        _in_proj_kernel,
        grid=(n // tm, cols // tn),
        in_specs=[
            pl.BlockSpec((tm, d), lambda i, j: (i, 0)),
            pl.BlockSpec((1, d), lambda i, j: (0, 0)),
            pl.BlockSpec((d, tn), lambda i, j: (0, j)),
        ],
        out_specs=pl.BlockSpec((tm, tn), lambda i, j: (i, j)),
        out_shape=jax.ShapeDtypeStruct((n, cols), F32),
        scratch_shapes=[pltpu.VMEM((tm, d), BF16)],
        compiler_params=pltpu.CompilerParams(
            dimension_semantics=("arbitrary", "arbitrary"),
            vmem_limit_bytes=40 * 1024 * 1024),
        name="in_proj",
    )(x2, g, w_bf16)


def _attn_kernel(q_ref, k_ref, v_ref, cos_ref, sa_ref, sb_ref, o_ref,
                 qn, kn, q4, k4, v4, q16, k16, v16,
                 a1, d1, m1, a4, d4, m4, a16, d16, m16, acm, dcm, mcm, bias_s, *, seq):
    nblk_total = seq // ATTN_BLOCK
    lane = lax.broadcasted_iota(jnp.int32, (1, LANES), 1)
    head0 = lane < HEAD_DIM

    qi = lax.broadcasted_iota(jnp.int32, (ATTN_BLOCK, 2 * ATTN_BLOCK), 0)
    kj = lax.broadcasted_iota(jnp.int32, (ATTN_BLOCK, 2 * ATTN_BLOCK), 1)
    band = (kj >= qi) & (kj <= qi + ATTN_BLOCK)
    neg = jnp.float32(-jnp.inf)
    bias_s[0] = jnp.where(band, 0.0, neg)
    bias_s[1] = jnp.where(band & (kj >= ATTN_BLOCK), 0.0, neg)

    def rot(x):
        return (x * cos_ref[...] + pltpu.roll(x, LANES - ROT_DIM // 2, 1) * sa_ref[...]
                + pltpu.roll(x, ROT_DIM // 2, 1) * sb_ref[...])

    qn[...] = rot(q_ref[...]) * (HEAD_DIM ** -0.5)
    kn[...] = rot(k_ref[...])

    for d, (qc, kc, vc) in ((4, (q4, k4, v4)), (16, (q16, k16, v16))):
        ln = seq // d
        for r in range(d):
            qc[r * ln:(r + 1) * ln, :] = qn[pl.ds(r, ln, stride=d), :]
            kc[r * ln:(r + 1) * ln, :] = kn[pl.ds(r, ln, stride=d), :]
            vc[r * ln:(r + 1) * ln, :] = v_ref[pl.ds(r, ln, stride=d), :]

    unroll = 16

    def run_branch(qsrc, ksrc, vsrc, dsts, blocks_per_class):
        a_dst, d_dst, m_dst = dsts

        def attend(b, first):
            cur = pl.ds(pl.multiple_of(b * ATTN_BLOCK, ATTN_BLOCK), ATTN_BLOCK)
            qb = qsrc[cur, :]
            prv = pl.ds(pl.multiple_of(jnp.maximum(b - 1, 0) * ATTN_BLOCK, ATTN_BLOCK),
                        ATTN_BLOCK)
            bias = bias_s[int(first)] if isinstance(first, bool) else bias_s[first]
            kk = jnp.concatenate([ksrc[prv, :], ksrc[cur, :]], axis=0).astype(BF16)
            vv = jnp.concatenate([vsrc[prv, :], vsrc[cur, :]], axis=0)
            res, mx = [], []
            for h in range(2):
                mine = head0 if h == 0 else jnp.logical_not(head0)
                qh = jnp.where(mine, qb, 0.0).astype(BF16)
                s = lax.dot_general(qh, kk, (((1,), (1,)), ((), ())), preferred_element_type=F32)
                s = s + bias
                m = jnp.max(s, axis=-1, keepdims=True)
                p = jnp.exp(s - m).astype(BF16)
                vh = jnp.where(mine, vv, 1.0).astype(BF16)
                res.append(jnp.dot(p, vh, preferred_element_type=F32))
                mx.append(m)
            a_dst[cur, :] = jnp.where(head0, res[0], res[1])
            d_dst[cur, :] = pltpu.roll(jnp.where(head0, res[1], res[0]), HEAD_DIM, 1)
            m_dst[cur, :] = jnp.where(head0, mx[0], mx[1])

        def body(i, carry):
            for u in range(unroll):
                b = i * unroll + u
                if unroll % blocks_per_class == 0:
                    first = u % blocks_per_class == 0
                elif blocks_per_class % unroll == 0 and u != 0:
                    first = False
                else:
                    first = jnp.where(jnp.asarray(b % blocks_per_class == 0), 1, 0)
                attend(b, first)
            return carry

        lax.fori_loop(0, nblk_total // unroll, body, 0)

    res1, res4, res16, res_cm = (a1, d1, m1), (a4, d4, m4), (a16, d16, m16), (acm, dcm, mcm)
    run_branch(qn, kn, v_ref, res1, nblk_total)
    for d, srcs, res in ((4, (q4, k4, v4), res4), (16, (q16, k16, v16), res16)):
        run_branch(*srcs, res_cm, nblk_total // d)
        ln = seq // d
        for cm, nat in zip(res_cm, res):
            for r in range(d):
                nat[pl.ds(r, ln, stride=d), :] = cm[r * ln:(r + 1) * ln, :]

    ma, mb, mc = m1[...], m4[...], m16[...]
    mx = jnp.maximum(jnp.maximum(ma, mb), mc)
    ea, eb, ec = jnp.exp(ma - mx), jnp.exp(mb - mx), jnp.exp(mc - mx)
    o_ref[...] = ((ea * a1[...] + eb * a4[...] + ec * a16[...])
                  / (ea * d1[...] + eb * d4[...] + ec * d16[...]))


def _rotary_tables(seq):
    half = ROT_DIM // 2
    inv_freq = ROPE_THETA ** (-jnp.arange(0, ROT_DIM, 2, dtype=F32) / ROT_DIM)
    ang = jnp.arange(seq, dtype=F32)[:, None] * inv_freq[None, :]
    cos, sin = jnp.cos(ang), jnp.sin(ang)
    pos = jnp.arange(LANES) % HEAD_DIM
    fidx = pos % half
    in_lo = pos < half
    in_hi = (pos >= half) & (pos < ROT_DIM)
    cos_t = jnp.where((in_lo | in_hi)[None, :], cos[:, fidx], 1.0)
    sa = jnp.where(in_lo[None, :], -sin[:, fidx], 0.0)
    sb = jnp.where(in_hi[None, :], sin[:, fidx], 0.0)
    return cos_t.astype(F32), sa.astype(F32), sb.astype(F32)


def _attention(proj3, conv_w_cols, attn_w_cols):
    b, seq, _ = proj3.shape
    assert seq % (ATTN_BLOCK * DILATIONS[-1]) == 0
    n_pairs = attn_w_cols // LANES
    qoff = 3 * conv_w_cols // LANES
    cos_t, sa, sb = _rotary_tables(seq)
    blk = lambda off: pl.BlockSpec((None, seq, LANES), lambda i, j: (i, 0, off + j))
    tab = pl.BlockSpec((seq, LANES), lambda i, j: (0, 0))
    big = pltpu.VMEM((seq, LANES), F32)
    return pl.pallas_call(
        functools.partial(_attn_kernel, seq=seq),
        grid=(b, n_pairs),
        in_specs=[blk(qoff), blk(qoff + n_pairs), blk(qoff + 2 * n_pairs), tab, tab, tab],
        out_specs=pl.BlockSpec((None, seq, LANES), lambda i, j: (i, 0, j)),
        out_shape=jax.ShapeDtypeStruct((b, seq, attn_w_cols), F32),
        scratch_shapes=[big] * 20 + [pltpu.VMEM((2, ATTN_BLOCK, 2 * ATTN_BLOCK), F32)],
        compiler_params=pltpu.CompilerParams(
            dimension_semantics=("arbitrary", "arbitrary"),
            vmem_limit_bytes=48 * 1024 * 1024),
        name="attention",
    )(proj3, proj3, proj3, cos_t, sa, sb)


def _out_proj_kernel(cx_ref, cb_ref, cc_ref, hx_ref, hc_ref, ya_ref, x_ref,
                     cw_ref, cg_ref, ag_ref, wo_ref, fg_ref, wrh_ref, wrl_ref, br_ref,
                     x1_ref, h2_ref, idx_ref, gate_ref, rank_ref, cnt_ref, carry,
                     *, tile, n_experts):
    first_tile_of_seq = pl.program_id(1) == 0
    first_step = (pl.program_id(0) == 0) & first_tile_of_seq

    @pl.when(first_step)
    def _():
        carry[...] = jnp.zeros_like(carry)

    u = cc_ref[...] * cx_ref[...]
    uh = jnp.where(first_tile_of_seq, 0.0, hc_ref[...] * hx_ref[...])
    row8 = lax.broadcasted_iota(jnp.int32, (SUBLANES, 1), 0)

    def shifted(k):
        r = pltpu.roll(u, k, 0)
        top = jnp.where(row8 < k, pltpu.roll(uh, k, 0), r[:SUBLANES])
        return jnp.concatenate([top, r[SUBLANES:]], axis=0)

    conv = cw_ref[2:3, :] * u + cw_ref[1:2, :] * shifted(1) + cw_ref[0:1, :] * shifted(2)
    y_conv = cb_ref[...] * conv

    mixed = jnp.concatenate(
        [_rms(y_conv, cg_ref[...]), _rms(ya_ref[...], ag_ref[...])], axis=-1).astype(BF16)
    x1 = x_ref[...] + jnp.dot(mixed, wo_ref[...], preferred_element_type=F32)
    x1_ref[...] = x1
    h2 = _rms(x1, fg_ref[...])
    d = h2.shape[-1]
    chunks = d // LANES
    for c in range(chunks):
        h2_ref[pl.ds(c, tile, stride=chunks), :] = h2[:, c * LANES:(c + 1) * LANES]

    h2_hi = h2.astype(BF16)
    h2_lo = (h2 - h2_hi.astype(F32)).astype(BF16)
    logits = (jnp.dot(h2_hi, wrh_ref[...], preferred_element_type=F32)
              + jnp.dot(h2_hi, wrl_ref[...], preferred_element_type=F32)
              + jnp.dot(h2_lo, wrh_ref[...], preferred_element_type=F32)) + br_ref[...]
    eio = lax.broadcasted_iota(jnp.int32, (tile, n_experts), 1).astype(F32)
    work = logits
    vals, idxs = [], []
    for _ in range(TOP_K):
        m = jnp.max(work, axis=1, keepdims=True)
        ik = jnp.min(jnp.where(work == m, eio, float(n_experts)), axis=1, keepdims=True)
        vals.append(m)
        idxs.append(ik)
        work = jnp.where(eio == ik, -jnp.inf, work)
    exps = [jnp.exp(v - vals[0]) for v in vals]
    tot = exps[0] + exps[1] + exps[2] + exps[3]

    onehot = jnp.zeros((tile, n_experts), F32)
    for ik in idxs:
        onehot = onehot + (eio == ik).astype(F32)
    ri = lax.broadcasted_iota(jnp.int32, (tile, tile), 0)
    ci = lax.broadcasted_iota(jnp.int32, (tile, tile), 1)
    tri = (ci < ri).astype(BF16)
    before = jnp.dot(tri, onehot.astype(BF16), preferred_element_type=F32) + carry[...]
    carry[...] = carry[...] + jnp.sum(onehot, axis=0, keepdims=True)
    cnt_ref[...] = carry[...].astype(jnp.int32)

    lio = lax.broadcasted_iota(jnp.int32, (tile, LANES), 1)
    idx_out = jnp.zeros((tile, LANES), jnp.int32)
    gate_out = jnp.zeros((tile, LANES), F32)
    rank_out = jnp.zeros((tile, LANES), jnp.int32)
    for k in range(TOP_K):
        rk = jnp.sum(jnp.where(eio == idxs[k], before, 0.0), axis=1, keepdims=True)
        idx_out = jnp.where(lio == k, idxs[k].astype(jnp.int32), idx_out)
        gate_out = jnp.where(lio == k, exps[k] / tot, gate_out)
        rank_out = jnp.where(lio == k, rk.astype(jnp.int32), rank_out)
    idx_ref[...] = idx_out
    gate_ref[...] = gate_out
    rank_ref[...] = rank_out


def _out_proj(proj3, y_attn, x3, conv_w, conv_g, attn_g, w_out_bf16, ffn_g, w_router, b_router,
              tile=256):
    b, seq, d = x3.shape
    cw_cols = conv_w.shape[1]
    aw_cols = y_attn.shape[2]
    n_experts = w_router.shape[1]
    w_router_hi = w_router.astype(BF16)
    w_router_lo = (w_router - w_router_hi.astype(F32)).astype(BF16)
    chunks = d // LANES
    tiles = seq // tile
    n = b * seq
    halo_blocks = tile // SUBLANES
    row_blk = lambda width, col: pl.BlockSpec((None, tile, width), lambda i, j: (i, j, col))
    halo = lambda col: pl.BlockSpec(
        (None, SUBLANES, cw_cols), lambda i, j: (i, jnp.maximum(j * halo_blocks - 1, 0), col))
    const = lambda shape: pl.BlockSpec(shape, lambda i, j: (0,) * len(shape))
    flat = lambda width: pl.BlockSpec((tile, width), lambda i, j: (i * tiles + j, 0))
    outs = pl.pallas_call(
        functools.partial(_out_proj_kernel, tile=tile, n_experts=n_experts),
        grid=(b, tiles),
        in_specs=[
            row_blk(cw_cols, 0), row_blk(cw_cols, 1), row_blk(cw_cols, 2), halo(0), halo(2),
            row_blk(aw_cols, 0), row_blk(d, 0),
            const((CONV_K, cw_cols)), const((1, cw_cols)), const((1, aw_cols)),
            const((d, d)), const((1, d)), const((d, n_experts)), const((d, n_experts)),
            const((1, n_experts)),
        ],
        out_specs=[
            row_blk(d, 0),
            pl.BlockSpec((tile * chunks, LANES), lambda i, j: (i * tiles + j, 0)),
            flat(LANES), flat(LANES), flat(LANES),
            const((1, n_experts)),
        ],
        out_shape=[
            jax.ShapeDtypeStruct((b, seq, d), F32),
            jax.ShapeDtypeStruct((n * chunks, LANES), F32),
            jax.ShapeDtypeStruct((n, LANES), jnp.int32),
            jax.ShapeDtypeStruct((n, LANES), F32),
            jax.ShapeDtypeStruct((n, LANES), jnp.int32),
            jax.ShapeDtypeStruct((1, n_experts), jnp.int32),
        ],
        scratch_shapes=[pltpu.VMEM((1, n_experts), F32)],
        compiler_params=pltpu.CompilerParams(
            dimension_semantics=("arbitrary", "arbitrary"),
            vmem_limit_bytes=48 * 1024 * 1024),
        name="out_proj",
    )(proj3, proj3, proj3, proj3, proj3, y_attn, x3, conv_w, conv_g, attn_g, w_out_bf16,
      ffn_g, w_router_hi, w_router_lo, b_router)
    return outs


DMA_ISSUE_UNROLL = 4


def _dispatch_kernel(dest_ref, pad_start, pad_rows, used_blocks, h2_ref, xs_hbm, zbuf, sem, zsem,
                     *, tile, chunks, n_experts, n_row_blocks):
    gran = MOE_ROW_GRAN

    @pl.when(pl.program_id(0) == 0)
    def _():
        zbuf[...] = jnp.zeros_like(zbuf)

        def zero_copy(first_row, n_rows):
            return pltpu.make_async_copy(_token_rows(zbuf, 0, n_rows, chunks),
                                         _token_rows(xs_hbm, first_row, n_rows, chunks), zsem)

        def for_each_zero_run(act):
            def per_expert(e, carry):
                row = pad_start[e]
                left = pad_rows[e]
                for bit in reversed(range(gran.bit_length() - 1)):
                    take = (left >> bit) & 1

                    @pl.when(take == 1)
                    def _():
                        act(zero_copy(row, 1 << bit))

                    row = row + (take << bit)
                return carry

            lax.fori_loop(0, n_experts, per_expert, 0)

            def tail(blk, carry):
                act(zero_copy(blk * gran, gran))
                return carry

            lax.fori_loop(used_blocks[0], n_row_blocks, tail, 0)

        for_each_zero_run(lambda cp: cp.start())
        for_each_zero_run(lambda cp: cp.wait())

    def issue(g, carry):
        for u in range(DMA_ISSUE_UNROLL):
            t = g * DMA_ISSUE_UNROLL + u
            for k in range(TOP_K):
                pltpu.make_async_copy(
                    _token_rows(h2_ref, t, 1, chunks),
                    _token_rows(xs_hbm, dest_ref[0, 0, t * TOP_K + k], 1, chunks), sem).start()
        return carry

    lax.fori_loop(0, tile // DMA_ISSUE_UNROLL, issue, 0)
    counted = _token_rows(xs_hbm, 0, tile * TOP_K, chunks)
    pltpu.make_async_copy(counted, counted, sem).wait()


def _dispatch(dest, pad_start, pad_rows, used_blocks, h2_tm, n_rows, tile=256):
    steps = dest.shape[0] // (tile * TOP_K)
    dest3 = dest.reshape(steps, 1, tile * TOP_K)
    chunks = h2_tm.shape[0] // (dest.shape[0] // TOP_K)
    gran = MOE_ROW_GRAN
    smem = pl.BlockSpec(memory_space=pltpu.SMEM)
    return pl.pallas_call(
        functools.partial(_dispatch_kernel, tile=tile, chunks=chunks,
                          n_experts=pad_start.shape[0], n_row_blocks=n_rows // gran),
        grid=(steps,),
        in_specs=[
            pl.BlockSpec((1, 1, tile * TOP_K), lambda i: (i, 0, 0), memory_space=pltpu.SMEM),
            smem, smem, smem,
            pl.BlockSpec((tile * chunks, LANES), lambda i: (i, 0)),
        ],
        out_specs=pl.BlockSpec(memory_space=pl.ANY),
        out_shape=jax.ShapeDtypeStruct((n_rows * chunks, LANES), F32),
        scratch_shapes=[pltpu.VMEM((gran * chunks, LANES), F32),
                        pltpu.SemaphoreType.DMA(()), pltpu.SemaphoreType.DMA(())],
        compiler_params=pltpu.CompilerParams(dimension_semantics=("arbitrary",)),
        name="dispatch",
    )(dest3, pad_start, pad_rows, used_blocks, h2_tm)


def _moe_kernel(item_e, item_row0, item_nblk, used_blocks, xs_hbm, wg_ref, wu_ref, bg_ref,
                bu_ref, wd_ref, bd_ref, ys_hbm, iobuf, xb, yacc, wgb, wub, wdb, sem_in, sem_out,
                *, chunks, n_col_steps, n_row_blocks):
    del item_e
    s = pl.program_id(0)
    j = pl.program_id(1)
    nb = item_nblk[s]
    row0 = item_row0[s]
    gran = MOE_ROW_GRAN

    def staged(slot):
        return iobuf.at[slot]

    def ys_block(blk):
        return _token_rows(ys_hbm, blk * gran, gran, chunks)

    @pl.when((s == pl.num_programs(0) - 1) & (j == n_col_steps - 1))
    def _():
        iobuf[0] = jnp.zeros(iobuf.shape[1:], iobuf.dtype)

        def fill(blk, carry):
            cp = pltpu.make_async_copy(staged(0), ys_block(blk), sem_out.at[0])
            cp.start()
            cp.wait()
            return carry

        lax.fori_loop(used_blocks[0], n_row_blocks, fill, 0)

    @pl.when(nb > 0)
    def _():
        @pl.when(j == 0)
        def _():
            def x_copy(sb, slot):
                return pltpu.make_async_copy(
                    _token_rows(xs_hbm, row0 + sb * gran, gran, chunks), staged(slot),
                    sem_in.at[slot])

            x_copy(0, 0).start()

            def load(sb, carry):
                slot = sb % 2

                @pl.when(sb + 1 < nb)
                def _():
                    x_copy(sb + 1, 1 - slot).start()

                x_copy(sb, slot).wait()
                rows = pl.ds(pl.multiple_of(sb * gran, gran), gran)
                for c in range(chunks):
                    xb[rows, c * LANES:(c + 1) * LANES] = (
                        iobuf[slot, pl.ds(c, gran, stride=chunks), :].astype(BF16))
                yacc[rows, :] = jnp.broadcast_to(bd_ref[...], (gran, yacc.shape[1]))
                return carry

            lax.fori_loop(0, nb, load, 0)

        def cast_weights():
            wgb[...] = wg_ref[...].astype(BF16)
            wub[...] = wu_ref[...].astype(BF16)
            wdb[...] = wd_ref[...].astype(BF16)

        def block_rows(sb):
            return pl.ds(pl.multiple_of(sb * gran, gran), gran)

        def partial_out(sb):
            x = xb[block_rows(sb), :]
            gate = jnp.dot(x, wgb[...], preferred_element_type=F32) + bg_ref[...]
            up = jnp.dot(x, wub[...], preferred_element_type=F32) + bu_ref[...]
            gate = jnp.minimum(gate, SWIGLU_LIMIT)
            up = jnp.clip(up, -SWIGLU_LIMIT, SWIGLU_LIMIT)
            act = (up + 1.0) * (gate * jax.nn.sigmoid(SWIGLU_ALPHA * gate))
            return jnp.dot(act.astype(BF16), wdb[...], preferred_element_type=F32)

        def pair(i, carry):
            pa = partial_out(2 * i)
            pb = partial_out(2 * i + 1)
            yacc[block_rows(2 * i), :] += pa
            yacc[block_rows(2 * i + 1), :] += pb
            return carry

        def single(sb):
            yacc[block_rows(sb), :] += partial_out(sb)

        @pl.when(nb >= 2)
        def _():
            cast_weights()
            pair(0, 0)
            lax.fori_loop(1, nb // 2, pair, 0)

            @pl.when(nb % 2 == 1)
            def _():
                single(nb - 1)

        @pl.when(nb == 1)
        def _():
            cast_weights()
            single(0)

        @pl.when(j == n_col_steps - 1)
        def _():
            def y_copy(sb, slot):
                return pltpu.make_async_copy(staged(slot), ys_block(row0 // gran + sb),
                                             sem_out.at[slot])

            def store(sb, carry):
                slot = sb % 2

                @pl.when(sb >= 2)
                def _():
                    y_copy(sb - 2, slot).wait()

                for c in range(chunks):
                    iobuf[slot, pl.ds(c, gran, stride=chunks), :] = (
                        yacc[block_rows(sb), c * LANES:(c + 1) * LANES])
                y_copy(sb, slot).start()
                return carry

            lax.fori_loop(0, nb, store, 0)

            @pl.when(nb >= 2)
            def _():
                y_copy(nb - 2, nb % 2).wait()

            y_copy(nb - 1, (nb - 1) % 2).wait()


def _moe(item_e, item_row0, item_nblk, used_blocks, xs, w_gate_up, b_gate_up, w_down, b_down,
         n_rows):
    n_experts, d, two_de = w_gate_up.shape
    de = two_de // 2
    chunks = d // LANES
    tn = MOE_COL_TILE
    n_col_steps = de // tn
    n_items = item_e.shape[0]
    gran = MOE_ROW_GRAN

    def col(j, nblk):
        return jnp.where(nblk > 0, j, n_col_steps - 1)

    bgu3 = b_gate_up.reshape(n_experts, 1, two_de)
    bd3 = b_down.reshape(n_experts, 1, d)
    grid_spec = pltpu.PrefetchScalarGridSpec(
        num_scalar_prefetch=4,
        grid=(n_items, n_col_steps),
        in_specs=[
            pl.BlockSpec(memory_space=pl.ANY),
            pl.BlockSpec((None, d, tn), lambda s, j, e, r, nb, ub: (e[s], 0, col(j, nb[s]))),
            pl.BlockSpec((None, d, tn),
                         lambda s, j, e, r, nb, ub: (e[s], 0, n_col_steps + col(j, nb[s]))),
            pl.BlockSpec((None, 1, tn), lambda s, j, e, r, nb, ub: (e[s], 0, col(j, nb[s]))),
            pl.BlockSpec((None, 1, tn),
                         lambda s, j, e, r, nb, ub: (e[s], 0, n_col_steps + col(j, nb[s]))),
            pl.BlockSpec((None, tn, d), lambda s, j, e, r, nb, ub: (e[s], col(j, nb[s]), 0)),
            pl.BlockSpec((None, 1, d), lambda s, j, e, r, nb, ub: (e[s], 0, 0)),
        ],
        out_specs=pl.BlockSpec(memory_space=pl.ANY),
        scratch_shapes=[
            pltpu.VMEM((2, gran * chunks, LANES), F32),
            pltpu.VMEM((MOE_ITEM_ROWS, d), BF16),
            pltpu.VMEM((MOE_ITEM_ROWS, d), F32),
            pltpu.VMEM((d, tn), BF16),
            pltpu.VMEM((d, tn), BF16),
            pltpu.VMEM((tn, d), BF16),
            pltpu.SemaphoreType.DMA((2,)),
            pltpu.SemaphoreType.DMA((2,)),
        ],
    )
    return pl.pallas_call(
        functools.partial(_moe_kernel, chunks=chunks, n_col_steps=n_col_steps,
                          n_row_blocks=n_rows // gran),
        grid_spec=grid_spec,
        out_shape=jax.ShapeDtypeStruct((n_rows * chunks, LANES), F32),
        compiler_params=pltpu.CompilerParams(
            dimension_semantics=("arbitrary", "arbitrary"),
            vmem_limit_bytes=56 * 1024 * 1024),
        name="moe",
    )(item_e, item_row0, item_nblk, used_blocks, xs, w_gate_up, w_gate_up, bgu3, bgu3, w_down,
      bd3)


def _combine_kernel(dest_ref, ys_hbm, x1_ref, gate_ref, fg_ref, o_ref, gbuf, sem,
                    *, tile, chunks, final_norm):
    i = pl.program_id(0)
    slot = i % 2
    per_tile = tile * TOP_K
    stride = _token_row_stride(chunks)

    def gather(step, dst_slot):
        base = step * per_tile

        def issue(g, carry):
            for u in range(DMA_ISSUE_UNROLL):
                t = g * DMA_ISSUE_UNROLL + u
                for k in range(TOP_K):
                    pltpu.make_async_copy(
                        _token_rows(ys_hbm, dest_ref[base + t * TOP_K + k], 1, chunks),
                        gbuf.at[dst_slot, pl.ds(pl.multiple_of((k * tile + t) * stride, SUBLANES),
                                                chunks), :],
                        sem.at[dst_slot]).start()
            return carry

        lax.fori_loop(0, tile // DMA_ISSUE_UNROLL, issue, 0)

    @pl.when(i == 0)
    def _():
        gather(0, 0)

    @pl.when(i + 1 < pl.num_programs(0))
    def _():
        gather(i + 1, 1 - slot)

    counted = gbuf.at[slot, pl.ds(0, per_tile * chunks), :]
    pltpu.make_async_copy(counted, counted, sem.at[slot]).wait()

    gates = gate_ref[...]
    gk = [gates[:, k:k + 1] for k in range(TOP_K)]
    cols = []
    for c in range(chunks):
        acc = x1_ref[:, c * LANES:(c + 1) * LANES]
        for k in range(TOP_K):
            acc = acc + gk[k] * gbuf[slot, pl.ds(k * tile * stride + c, tile, stride=stride), :]
        cols.append(acc)
    out = jnp.concatenate(cols, axis=-1)
    if final_norm:
        out = _rms(out, fg_ref[...])
    o_ref[...] = out


def _combine(dest, ys, x1, gates, final_g, final_norm, tile=128):
    n, d = x1.shape
    chunks = d // LANES
    steps = n // tile
    grid_spec = pltpu.PrefetchScalarGridSpec(
        num_scalar_prefetch=1,
        grid=(steps,),
        in_specs=[
            pl.BlockSpec(memory_space=pl.ANY),
            pl.BlockSpec((tile, d), lambda i, dest: (i, 0)),
            pl.BlockSpec((tile, LANES), lambda i, dest: (i, 0)),
            pl.BlockSpec((1, d), lambda i, dest: (0, 0)),
        ],
        out_specs=pl.BlockSpec((tile, d), lambda i, dest: (i, 0)),
        scratch_shapes=[pltpu.VMEM((2, TOP_K * tile * _token_row_stride(chunks), LANES), F32),
                        pltpu.SemaphoreType.DMA((2,))],
    )
    return pl.pallas_call(
        functools.partial(_combine_kernel, tile=tile, chunks=chunks, final_norm=final_norm),
        grid_spec=grid_spec,
        out_shape=jax.ShapeDtypeStruct((n, d), F32),
        compiler_params=pltpu.CompilerParams(dimension_semantics=("arbitrary",),
                                             vmem_limit_bytes=40 * 1024 * 1024),
        name="combine",
    )(dest, ys, x1, gates, final_g)


def _routing_tables(counts, idx, rank, n_rows_cap):
    n_experts = counts.shape[0]
    gran, item_rows = MOE_ROW_GRAN, MOE_ITEM_ROWS
    padded = ((counts + gran - 1) // gran) * gran
    ends = jnp.cumsum(padded)
    offs = ends - padded
    dest = offs[idx] + rank
    items_per = (padded + item_rows - 1) // item_rows
    item_ends = jnp.cumsum(items_per)
    item_starts = item_ends - items_per
    n_items = n_experts + n_rows_cap // item_rows
    slot = jnp.arange(n_items, dtype=jnp.int32)
    total = item_ends[-1]
    live = slot < total
    live_slot = jnp.minimum(slot, total - 1)
    e_of = jnp.minimum(jnp.sum((live_slot[:, None] >= item_ends[None, :]).astype(jnp.int32), axis=1),
                       n_experts - 1).astype(jnp.int32)
    local = slot - item_starts[e_of]
    row0 = offs[e_of] + local * item_rows
    rows = jnp.clip(padded[e_of] - local * item_rows, 0, item_rows)
    nblk = jnp.where(live, rows // gran, 0).astype(jnp.int32)
    row0 = jnp.where(live, row0, 0).astype(jnp.int32)
    used_blocks = (ends[-1:] // gran).astype(jnp.int32)
    pad_start = (offs + counts).astype(jnp.int32)
    pad_rows = (padded - counts).astype(jnp.int32)
    return dest.astype(jnp.int32), e_of, row0, nblk, used_blocks, pad_start, pad_rows


def _layer(x3, mix_g, w_in, conv_w, conv_g, attn_g, w_out, ffn_g, w_router, b_router,
           w_gate_up, b_gate_up, w_down, b_down, final_g, final_norm):
    b, seq, d = x3.shape
    n = b * seq
    cw_cols = conv_w.shape[1]
    aw_cols = attn_g.shape[0]
    n_experts = w_router.shape[1]
    chunks = d // LANES

    proj = _in_proj(x3.reshape(n, d), mix_g.reshape(1, d), w_in.astype(BF16))
    proj3 = proj.reshape(b, seq, -1)
    y_attn = _attention(proj3, cw_cols, aw_cols)
    x1, h2_tm, idx, gates, rank, counts = _out_proj(
        proj3, y_attn, x3, conv_w, conv_g.reshape(1, -1), attn_g.reshape(1, -1),
        w_out.astype(BF16), ffn_g.reshape(1, d), w_router, b_router.reshape(1, -1))

    n_rows = n * TOP_K + n_experts * MOE_ROW_GRAN
    dest, item_e, item_row0, item_nblk, used_blocks, pad_start, pad_rows = _routing_tables(
        counts[0], idx[:, :TOP_K], rank[:, :TOP_K], n_rows)
    dest_flat = dest.reshape(n * TOP_K)
    xs = _dispatch(dest_flat, pad_start, pad_rows, used_blocks, h2_tm, n_rows)
    ys = _moe(item_e, item_row0, item_nblk, used_blocks, xs, w_gate_up, b_gate_up, w_down,
              b_down, n_rows)
    out = _combine(dest_flat, ys, x1.reshape(n, d), gates, final_g.reshape(1, d), final_norm)
    return out.reshape(b, seq, d)


def kernel(x, mix_norm_g, w_in, conv_w, conv_norm_g, attn_norm_g, w_out, ffn_norm_g, w_router,
           b_router, w_gate_up, b_gate_up, w_down, b_down, final_norm_g):
    depth = w_in.shape[0]
    for layer in range(depth):
        x = _layer(x, mix_norm_g[layer], w_in[layer], conv_w[layer], conv_norm_g[layer],
                   attn_norm_g[layer], w_out[layer], ffn_norm_g[layer], w_router[layer],
                   b_router[layer], w_gate_up[layer], b_gate_up[layer], w_down[layer],
                   b_down[layer], final_norm_g, layer == depth - 1)
    return x
```

```python
import functools

import jax
import jax.numpy as jnp
from jax import lax
from jax.experimental import pallas as pl
from jax.experimental.pallas import tpu as pltpu

HEAD_DIM = 64
CONV_K = 3
ROT_DIM = HEAD_DIM // 4
ROPE_THETA = 500000.0
DILATIONS = (1, 4, 16)
ATTN_BLOCK = 128
TOP_K = 4
SWIGLU_LIMIT = 7.0
SWIGLU_ALPHA = 1.702
EPS = 1e-5

LANES = 128
SUBLANES = 8

MOE_ROW_GRAN = 256
MOE_ITEM_ROWS = 1536
MOE_COL_TILE = 256
BF16 = jnp.bfloat16
F32 = jnp.float32


def _token_row_stride(chunks):
    return chunks + SUBLANES


def _token_rows(ref, first_token, n_tokens, chunks):
    return ref.at[pl.ds(pl.multiple_of(first_token * chunks, chunks), n_tokens * chunks), :]


def _rms(x, g):
    return x * lax.rsqrt(jnp.mean(x * x, axis=-1, keepdims=True) + EPS) * g


def _in_proj_kernel(x_ref, g_ref, w_ref, o_ref, xn_ref):
    @pl.when(pl.program_id(1) == 0)
    def _():
        xn_ref[...] = _rms(x_ref[...], g_ref[...]).astype(BF16)

    o_ref[...] = jnp.dot(xn_ref[...], w_ref[...], preferred_element_type=F32)


def _in_proj(x2, g, w_bf16, tm=1024, tn=512):
    n, d = x2.shape
    cols = w_bf16.shape[1]
    return pl.pallas_call(
        _in_proj_kernel,
        grid=(n // tm, cols // tn),
        in_specs=[
            pl.BlockSpec((tm, d), lambda i, j: (i, 0)),
            pl.BlockSpec((1, d), lambda i, j: (0, 0)),
            pl.BlockSpec((d, tn), lambda i, j: (0, j)),
        ],
        out_specs=pl.BlockSpec((tm, tn), lambda i, j: (i, j)),
        out_shape=jax.ShapeDtypeStruct((n, cols), F32),
        scratch_shapes=[pltpu.VMEM((tm, d), BF16)],
        compiler_params=pltpu.CompilerParams(
            dimension_semantics=("arbitrary", "arbitrary"),
            vmem_limit_bytes=40 * 1024 * 1024),
        name="in_proj",
    )(x2, g, w_bf16)


def _attn_kernel(q_ref, k_ref, v_ref, cos_ref, sa_ref, sb_ref, o_ref,
                 qn, kn, q4, k4, v4, q16, k16, v16,
                 a1, d1, m1, a4, d4, m4, a16, d16, m16, acm, dcm, mcm, bias_s, *, seq):
    nblk_total = seq // ATTN_BLOCK
    lane = lax.broadcasted_iota(jnp.int32, (1, LANES), 1)
    head0 = lane < HEAD_DIM

    qi = lax.broadcasted_iota(jnp.int32, (ATTN_BLOCK, 2 * ATTN_BLOCK), 0)
    kj = lax.broadcasted_iota(jnp.int32, (ATTN_BLOCK, 2 * ATTN_BLOCK), 1)
    band = (kj >= qi) & (kj <= qi + ATTN_BLOCK)
    neg = jnp.float32(-jnp.inf)
    bias_s[0] = jnp.where(band, 0.0, neg)
    bias_s[1] = jnp.where(band & (kj >= ATTN_BLOCK), 0.0, neg)

    def rot(x):
        return (x * cos_ref[...] + pltpu.roll(x, LANES - ROT_DIM // 2, 1) * sa_ref[...]
                + pltpu.roll(x, ROT_DIM // 2, 1) * sb_ref[...])

    qn[...] = rot(q_ref[...]) * (HEAD_DIM ** -0.5)
    kn[...] = rot(k_ref[...])

    for d, (qc, kc, vc) in ((4, (q4, k4, v4)), (16, (q16, k16, v16))):
        ln = seq // d
        for r in range(d):
            qc[r * ln:(r + 1) * ln, :] = qn[pl.ds(r, ln, stride=d), :]
            kc[r * ln:(r + 1) * ln, :] = kn[pl.ds(r, ln, stride=d), :]
            vc[r * ln:(r + 1) * ln, :] = v_ref[pl.ds(r, ln, stride=d), :]

    unroll = 16

    def run_branch(qsrc, ksrc, vsrc, dsts, blocks_per_class):
        a_dst, d_dst, m_dst = dsts

        def attend(b, first):
            cur = pl.ds(pl.multiple_of(b * ATTN_BLOCK, ATTN_BLOCK), ATTN_BLOCK)
            qb = qsrc[cur, :]
            prv = pl.ds(pl.multiple_of(jnp.maximum(b - 1, 0) * ATTN_BLOCK, ATTN_BLOCK),
                        ATTN_BLOCK)
            bias = bias_s[int(first)] if isinstance(first, bool) else bias_s[first]
            kk = jnp.concatenate([ksrc[prv, :], ksrc[cur, :]], axis=0).astype(BF16)
            vv = jnp.concatenate([vsrc[prv, :], vsrc[cur, :]], axis=0)
            res, mx = [], []
            for h in range(2):
                mine = head0 if h == 0 else jnp.logical_not(head0)
                qh = jnp.where(mine, qb, 0.0).astype(BF16)
                s = lax.dot_general(qh, kk, (((1,), (1,)), ((), ())), preferred_element_type=F32)
                s = s + bias
                m = jnp.max(s, axis=-1, keepdims=True)
                p = jnp.exp(s - m).astype(BF16)
                vh = jnp.where(mine, vv, 1.0).astype(BF16)
                res.append(jnp.dot(p, vh, preferred_element_type=F32))
                mx.append(m)
            a_dst[cur, :] = jnp.where(head0, res[0], res[1])
            d_dst[cur, :] = pltpu.roll(jnp.where(head0, res[1], res[0]), HEAD_DIM, 1)
            m_dst[cur, :] = jnp.where(head0, mx[0], mx[1])

        def body(i, carry):
            for u in range(unroll):
                b = i * unroll + u
                if unroll % blocks_per_class == 0:
                    first = u % blocks_per_class == 0
                elif blocks_per_class % unroll == 0 and u != 0:
                    first = False
                else:
                    first = jnp.where(jnp.asarray(b % blocks_per_class == 0), 1, 0)
                attend(b, first)
            return carry

        lax.fori_loop(0, nblk_total // unroll, body, 0)

    res1, res4, res16, res_cm = (a1, d1, m1), (a4, d4, m4), (a16, d16, m16), (acm, dcm, mcm)
    run_branch(qn, kn, v_ref, res1, nblk_total)
    for d, srcs, res in ((4, (q4, k4, v4), res4), (16, (q16, k16, v16), res16)):
        run_branch(*srcs, res_cm, nblk_total // d)
        ln = seq // d
        for cm, nat in zip(res_cm, res):
            for r in range(d):
                nat[pl.ds(r, ln, stride=d), :] = cm[r * ln:(r + 1) * ln, :]

    ma, mb, mc = m1[...], m4[...], m16[...]
    mx = jnp.maximum(jnp.maximum(ma, mb), mc)
    ea, eb, ec = jnp.exp(ma - mx), jnp.exp(mb - mx), jnp.exp(mc - mx)
    o_ref[...] = ((ea * a1[...] + eb * a4[...] + ec * a16[...])
                  / (ea * d1[...] + eb * d4[...] + ec * d16[...]))


def _rotary_tables(seq):
    half = ROT_DIM // 2
    inv_freq = ROPE_THETA ** (-jnp.arange(0, ROT_DIM, 2, dtype=F32) / ROT_DIM)
    ang = jnp.arange(seq, dtype=F32)[:, None] * inv_freq[None, :]
    cos, sin = jnp.cos(ang), jnp.sin(ang)
    pos = jnp.arange(LANES) % HEAD_DIM
    fidx = pos % half
    in_lo = pos < half
    in_hi = (pos >= half) & (pos < ROT_DIM)
    cos_t = jnp.where((in_lo | in_hi)[None, :], cos[:, fidx], 1.0)
    sa = jnp.where(in_lo[None, :], -sin[:, fidx], 0.0)
    sb = jnp.where(in_hi[None, :], sin[:, fidx], 0.0)
    return cos_t.astype(F32), sa.astype(F32), sb.astype(F32)


def _attention(proj3, conv_w_cols, attn_w_cols):
    b, seq, _ = proj3.shape
    assert seq % (ATTN_BLOCK * DILATIONS[-1]) == 0
    n_pairs = attn_w_cols // LANES
    qoff = 3 * conv_w_cols // LANES
    cos_t, sa, sb = _rotary_tables(seq)
    blk = lambda off: pl.BlockSpec((None, seq, LANES), lambda i, j: (i, 0, off + j))
    tab = pl.BlockSpec((seq, LANES), lambda i, j: (0, 0))
    big = pltpu.VMEM((seq, LANES), F32)
    return pl.pallas_call(
        functools.partial(_attn_kernel, seq=seq),
        grid=(b, n_pairs),
        in_specs=[blk(qoff), blk(qoff + n_pairs), blk(qoff + 2 * n_pairs), tab, tab, tab],
        out_specs=pl.BlockSpec((None, seq, LANES), lambda i, j: (i, 0, j)),
        out_shape=jax.ShapeDtypeStruct((b, seq, attn_w_cols), F32),
        scratch_shapes=[big] * 20 + [pltpu.VMEM((2, ATTN_BLOCK, 2 * ATTN_BLOCK), F32)],
        compiler_params=pltpu.CompilerParams(
            dimension_semantics=("arbitrary", "arbitrary"),
            vmem_limit_bytes=48 * 1024 * 1024),
        name="attention",
    )(proj3, proj3, proj3, cos_t, sa, sb)


def _out_proj_kernel(cx_ref, cb_ref, cc_ref, hx_ref, hc_ref, ya_ref, x_ref,
                     cw_ref, cg_ref, ag_ref, wo_ref, fg_ref, wrh_ref, wrl_ref, br_ref,
                     x1_ref, h2_ref, idx_ref, gate_ref, rank_ref, cnt_ref, carry,
                     *, tile, n_experts):
    first_tile_of_seq = pl.program_id(1) == 0
    first_step = (pl.program_id(0) == 0) & first_tile_of_seq

    @pl.when(first_step)
    def _():
        carry[...] = jnp.zeros_like(carry)

    u = cc_ref[...] * cx_ref[...]
    uh = jnp.where(first_tile_of_seq, 0.0, hc_ref[...] * hx_ref[...])
    row8 = lax.broadcasted_iota(jnp.int32, (SUBLANES, 1), 0)

    def shifted(k):
        r = pltpu.roll(u, k, 0)
        top = jnp.where(row8 < k, pltpu.roll(uh, k, 0), r[:SUBLANES])
        return jnp.concatenate([top, r[SUBLANES:]], axis=0)

    conv = cw_ref[2:3, :] * u + cw_ref[1:2, :] * shifted(1) + cw_ref[0:1, :] * shifted(2)
    y_conv = cb_ref[...] * conv

    mixed = jnp.concatenate(
        [_rms(y_conv, cg_ref[...]), _rms(ya_ref[...], ag_ref[...])], axis=-1).astype(BF16)
    x1 = x_ref[...] + jnp.dot(mixed, wo_ref[...], preferred_element_type=F32)
    x1_ref[...] = x1
    h2 = _rms(x1, fg_ref[...])
    d = h2.shape[-1]
    chunks = d // LANES
    for c in range(chunks):
        h2_ref[pl.ds(c, tile, stride=chunks), :] = h2[:, c * LANES:(c + 1) * LANES]

    h2_hi = h2.astype(BF16)
    h2_lo = (h2 - h2_hi.astype(F32)).astype(BF16)
    logits = (jnp.dot(h2_hi, wrh_ref[...], preferred_element_type=F32)
              + jnp.dot(h2_hi, wrl_ref[...], preferred_element_type=F32)
              + jnp.dot(h2_lo, wrh_ref[...], preferred_element_type=F32)) + br_ref[...]
    eio = lax.broadcasted_iota(jnp.int32, (tile, n_experts), 1).astype(F32)
    work = logits
    vals, idxs = [], []
    for _ in range(TOP_K):
        m = jnp.max(work, axis=1, keepdims=True)
        ik = jnp.min(jnp.where(work == m, eio, float(n_experts)), axis=1, keepdims=True)
        vals.append(m)
        idxs.append(ik)
        work = jnp.where(eio == ik, -jnp.inf, work)
    exps = [jnp.exp(v - vals[0]) for v in vals]
    tot = exps[0] + exps[1] + exps[2] + exps[3]

    onehot = jnp.zeros((tile, n_experts), F32)
    for ik in idxs:
        onehot = onehot + (eio == ik).astype(F32)
    ri = lax.broadcasted_iota(jnp.int32, (tile, tile), 0)
    ci = lax.broadcasted_iota(jnp.int32, (tile, tile), 1)
    tri = (ci < ri).astype(BF16)
    before = jnp.dot(tri, onehot.astype(BF16), preferred_element_type=F32) + carry[...]
    carry[...] = carry[...] + jnp.sum(onehot, axis=0, keepdims=True)
    cnt_ref[...] = carry[...].astype(jnp.int32)

    lio = lax.broadcasted_iota(jnp.int32, (tile, LANES), 1)
    idx_out = jnp.zeros((tile, LANES), jnp.int32)
    gate_out = jnp.zeros((tile, LANES), F32)
    rank_out = jnp.zeros((tile, LANES), jnp.int32)
    for k in range(TOP_K):
        rk = jnp.sum(jnp.where(eio == idxs[k], before, 0.0), axis=1, keepdims=True)
        idx_out = jnp.where(lio == k, idxs[k].astype(jnp.int32), idx_out)
        gate_out = jnp.where(lio == k, exps[k] / tot, gate_out)
        rank_out = jnp.where(lio == k, rk.astype(jnp.int32), rank_out)
    idx_ref[...] = idx_out
    gate_ref[...] = gate_out
    rank_ref[...] = rank_out


def _out_proj(proj3, y_attn, x3, conv_w, conv_g, attn_g, w_out_bf16, ffn_g, w_router, b_router,
              tile=256):
    b, seq, d = x3.shape
    cw_cols = conv_w.shape[1]
    aw_cols = y_attn.shape[2]
    n_experts = w_router.shape[1]
    w_router_hi = w_router.astype(BF16)
    w_router_lo = (w_router - w_router_hi.astype(F32)).astype(BF16)
    chunks = d // LANES
    tiles = seq // tile
    n = b * seq
    halo_blocks = tile // SUBLANES
    row_blk = lambda width, col: pl.BlockSpec((None, tile, width), lambda i, j: (i, j, col))
    halo = lambda col: pl.BlockSpec(
        (None, SUBLANES, cw_cols), lambda i, j: (i, jnp.maximum(j * halo_blocks - 1, 0), col))
    const = lambda shape: pl.BlockSpec(shape, lambda i, j: (0,) * len(shape))
    flat = lambda width: pl.BlockSpec((tile, width), lambda i, j: (i * tiles + j, 0))
    outs = pl.pallas_call(
        functools.partial(_out_proj_kernel, tile=tile, n_experts=n_experts),
        grid=(b, tiles),
        in_specs=[
            row_blk(cw_cols, 0), row_blk(cw_cols, 1), row_blk(cw_cols, 2), halo(0), halo(2),
            row_blk(aw_cols, 0), row_blk(d, 0),
            const((CONV_K, cw_cols)), const((1, cw_cols)), const((1, aw_cols)),
            const((d, d)), const((1, d)), const((d, n_experts)), const((d, n_experts)),
            const((1, n_experts)),
        ],
        out_specs=[
            row_blk(d, 0),
            pl.BlockSpec((tile * chunks, LANES), lambda i, j: (i * tiles + j, 0)),
            flat(LANES), flat(LANES), flat(LANES),
            const((1, n_experts)),
        ],
        out_shape=[
            jax.ShapeDtypeStruct((b, seq, d), F32),
            jax.ShapeDtypeStruct((n * chunks, LANES), F32),
            jax.ShapeDtypeStruct((n, LANES), jnp.int32),
            jax.ShapeDtypeStruct((n, LANES), F32),
            jax.ShapeDtypeStruct((n, LANES), jnp.int32),
            jax.ShapeDtypeStruct((1, n_experts), jnp.int32),
        ],
        scratch_shapes=[pltpu.VMEM((1, n_experts), F32)],
        compiler_params=pltpu.CompilerParams(
            dimension_semantics=("arbitrary", "arbitrary"),
            vmem_limit_bytes=48 * 1024 * 1024),
        name="out_proj",
    )(proj3, proj3, proj3, proj3, proj3, y_attn, x3, conv_w, conv_g, attn_g, w_out_bf16,
      ffn_g, w_router_hi, w_router_lo, b_router)
    return outs


DMA_ISSUE_UNROLL = 4


def _dispatch_kernel(dest_ref, pad_start, pad_rows, used_blocks, h2_ref, xs_hbm, zbuf, sem, zsem,
                     *, tile, chunks, n_experts, n_row_blocks):
    gran = MOE_ROW_GRAN

    @pl.when(pl.program_id(0) == 0)
    def _():
        zbuf[...] = jnp.zeros_like(zbuf)

        def zero_copy(first_row, n_rows):
            return pltpu.make_async_copy(_token_rows(zbuf, 0, n_rows, chunks),
                                         _token_rows(xs_hbm, first_row, n_rows, chunks), zsem)

        def for_each_zero_run(act):
            def per_expert(e, carry):
                row = pad_start[e]
                left = pad_rows[e]
                for bit in reversed(range(gran.bit_length() - 1)):
                    take = (left >> bit) & 1

                    @pl.when(take == 1)
                    def _():
                        act(zero_copy(row, 1 << bit))

                    row = row + (take << bit)
                return carry

            lax.fori_loop(0, n_experts, per_expert, 0)

            def tail(blk, carry):
                act(zero_copy(blk * gran, gran))
                return carry

            lax.fori_loop(used_blocks[0], n_row_blocks, tail, 0)

        for_each_zero_run(lambda cp: cp.start())
        for_each_zero_run(lambda cp: cp.wait())

    def issue(g, carry):
        for u in range(DMA_ISSUE_UNROLL):
            t = g * DMA_ISSUE_UNROLL + u
            for k in range(TOP_K):
                pltpu.make_async_copy(
                    _token_rows(h2_ref, t, 1, chunks),
                    _token_rows(xs_hbm, dest_ref[0, 0, t * TOP_K + k], 1, chunks), sem).start()
        return carry

    lax.fori_loop(0, tile // DMA_ISSUE_UNROLL, issue, 0)
    counted = _token_rows(xs_hbm, 0, tile * TOP_K, chunks)
    pltpu.make_async_copy(counted, counted, sem).wait()


def _dispatch(dest, pad_start, pad_rows, used_blocks, h2_tm, n_rows, tile=256):
    steps = dest.shape[0] // (tile * TOP_K)
    dest3 = dest.reshape(steps, 1, tile * TOP_K)
    chunks = h2_tm.shape[0] // (dest.shape[0] // TOP_K)
    gran = MOE_ROW_GRAN
    smem = pl.BlockSpec(memory_space=pltpu.SMEM)
    return pl.pallas_call(
        functools.partial(_dispatch_kernel, tile=tile, chunks=chunks,
                          n_experts=pad_start.shape[0], n_row_blocks=n_rows // gran),
        grid=(steps,),
        in_specs=[
            pl.BlockSpec((1, 1, tile * TOP_K), lambda i: (i, 0, 0), memory_space=pltpu.SMEM),
            smem, smem, smem,
            pl.BlockSpec((tile * chunks, LANES), lambda i: (i, 0)),
        ],
        out_specs=pl.BlockSpec(memory_space=pl.ANY),
        out_shape=jax.ShapeDtypeStruct((n_rows * chunks, LANES), F32),
        scratch_shapes=[pltpu.VMEM((gran * chunks, LANES), F32),
                        pltpu.SemaphoreType.DMA(()), pltpu.SemaphoreType.DMA(())],
        compiler_params=pltpu.CompilerParams(dimension_semantics=("arbitrary",)),
        name="dispatch",
    )(dest3, pad_start, pad_rows, used_blocks, h2_tm)


MOE_WEIGHT_SLOTS = 3


def _moe_kernel(item_e, item_row0, item_nblk, used_blocks, live_items, xs_hbm, wgu_hbm, wdn_hbm,
                bgu_ref, bd_ref, ys_hbm, iobuf, xb, yacc, wgf, wuf, wdf, wgb, wub, wdb,
                sem_in, sem_out, sem_w, *, chunks, n_col_steps, n_row_blocks):
    s = pl.program_id(0)
    nb = item_nblk[s]
    row0 = item_row0[s]
    gran = MOE_ROW_GRAN
    tn = MOE_COL_TILE
    de = n_col_steps * tn
    total_chunks = live_items[0] * n_col_steps

    def staged(slot):
        return iobuf.at[slot]

    def ys_block(blk):
        return _token_rows(ys_hbm, blk * gran, gran, chunks)

    def weight_copies(g):
        e = item_e[g // n_col_steps]
        col = pl.multiple_of((g % n_col_steps) * tn, tn)
        slot = g % MOE_WEIGHT_SLOTS
        return (
            pltpu.make_async_copy(wgu_hbm.at[e, :, pl.ds(col, tn)], wgf.at[slot], sem_w.at[slot]),
            pltpu.make_async_copy(wgu_hbm.at[e, :, pl.ds(pl.multiple_of(de + col, tn), tn)],
                                  wuf.at[slot], sem_w.at[slot]),
            pltpu.make_async_copy(wdn_hbm.at[e, pl.ds(col, tn), :], wdf.at[slot], sem_w.at[slot]),
        )

    def fetch(g):
        @pl.when(g < total_chunks)
        def _():
            for cp in weight_copies(g):
                cp.start()

    @pl.when(s == 0)
    def _():
        for g in range(MOE_WEIGHT_SLOTS - 1):
            fetch(g)

    @pl.when(nb > 0)
    def _():
        def x_copy(sb, slot):
            return pltpu.make_async_copy(
                _token_rows(xs_hbm, row0 + sb * gran, gran, chunks), staged(slot),
                sem_in.at[slot])

        x_copy(0, 0).start()

        def load(sb, carry):
            slot = sb % 2

            @pl.when(sb + 1 < nb)
            def _():
                x_copy(sb + 1, 1 - slot).start()

            x_copy(sb, slot).wait()
            rows = pl.ds(pl.multiple_of(sb * gran, gran), gran)
            for c in range(chunks):
                xb[rows, c * LANES:(c + 1) * LANES] = (
                    iobuf[slot, pl.ds(c, gran, stride=chunks), :].astype(BF16))
            yacc[rows, :] = jnp.broadcast_to(bd_ref[...], (gran, yacc.shape[1]))
            return carry

        lax.fori_loop(0, nb, load, 0)

        def block_rows(sb):
            return pl.ds(pl.multiple_of(sb * gran, gran), gran)

        def column_step(j, carry):
            g = s * n_col_steps + j
            slot = g % MOE_WEIGHT_SLOTS
            for cp in weight_copies(g):
                cp.wait()
            fetch(g + MOE_WEIGHT_SLOTS - 1)
            col = pl.multiple_of(j * tn, tn)
            bg = bgu_ref[:, pl.ds(col, tn)]
            bu = bgu_ref[:, pl.ds(pl.multiple_of(de + col, tn), tn)]

            def cast_weights():
                wgb[...] = wgf[slot].astype(BF16)
                wub[...] = wuf[slot].astype(BF16)
                wdb[...] = wdf[slot].astype(BF16)

            def partial_out(sb):
                x = xb[block_rows(sb), :]
                gate = jnp.dot(x, wgb[...], preferred_element_type=F32) + bg
                up = jnp.dot(x, wub[...], preferred_element_type=F32) + bu
                gate = jnp.minimum(gate, SWIGLU_LIMIT)
                up = jnp.clip(up, -SWIGLU_LIMIT, SWIGLU_LIMIT)
                act = (up + 1.0) * (gate * jax.nn.sigmoid(SWIGLU_ALPHA * gate))
                return jnp.dot(act.astype(BF16), wdb[...], preferred_element_type=F32)

            def pair(i, c):
                pa = partial_out(2 * i)
                pb = partial_out(2 * i + 1)
                yacc[block_rows(2 * i), :] += pa
                yacc[block_rows(2 * i + 1), :] += pb
                return c

            def single(sb):
                yacc[block_rows(sb), :] += partial_out(sb)

            @pl.when(nb >= 2)
            def _():
                cast_weights()
                pair(0, 0)
                lax.fori_loop(1, nb // 2, pair, 0)

                @pl.when(nb % 2 == 1)
                def _():
                    single(nb - 1)

            @pl.when(nb == 1)
            def _():
                cast_weights()
                single(0)

            return carry

        lax.fori_loop(0, n_col_steps, column_step, 0)

        def y_copy(sb, slot):
            return pltpu.make_async_copy(staged(slot), ys_block(row0 // gran + sb),
                                         sem_out.at[slot])

        def store(sb, carry):
            slot = sb % 2

            @pl.when(sb >= 2)
            def _():
                y_copy(sb - 2, slot).wait()

            for c in range(chunks):
                iobuf[slot, pl.ds(c, gran, stride=chunks), :] = (
                    yacc[block_rows(sb), c * LANES:(c + 1) * LANES])
            y_copy(sb, slot).start()
            return carry

        lax.fori_loop(0, nb, store, 0)

        @pl.when(nb >= 2)
        def _():
            y_copy(nb - 2, nb % 2).wait()

        y_copy(nb - 1, (nb - 1) % 2).wait()

    @pl.when(s == pl.num_programs(0) - 1)
    def _():
        iobuf[0] = jnp.zeros(iobuf.shape[1:], iobuf.dtype)

        def fill(blk, carry):
            cp = pltpu.make_async_copy(staged(0), ys_block(blk), sem_out.at[0])
            cp.start()
            cp.wait()
            return carry

        lax.fori_loop(used_blocks[0], n_row_blocks, fill, 0)


def _moe(item_e, item_row0, item_nblk, used_blocks, live_items, xs, w_gate_up, b_gate_up, w_down,
         b_down, n_rows):
    n_experts, d, two_de = w_gate_up.shape
    de = two_de // 2
    chunks = d // LANES
    tn = MOE_COL_TILE
    n_col_steps = de // tn
    n_items = item_e.shape[0]
    gran = MOE_ROW_GRAN
    slots = MOE_WEIGHT_SLOTS

    bgu3 = b_gate_up.reshape(n_experts, 1, two_de)
    bd3 = b_down.reshape(n_experts, 1, d)
    per_expert = lambda width: pl.BlockSpec((None, 1, width), lambda s, e, *_: (e[s], 0, 0))
    grid_spec = pltpu.PrefetchScalarGridSpec(
        num_scalar_prefetch=5,
        grid=(n_items,),
        in_specs=[
            pl.BlockSpec(memory_space=pl.ANY),
            pl.BlockSpec(memory_space=pl.ANY),
            pl.BlockSpec(memory_space=pl.ANY),
            per_expert(two_de),
            per_expert(d),
        ],
        out_specs=pl.BlockSpec(memory_space=pl.ANY),
        scratch_shapes=[
            pltpu.VMEM((2, gran * chunks, LANES), F32),
            pltpu.VMEM((MOE_ITEM_ROWS, d), BF16),
            pltpu.VMEM((MOE_ITEM_ROWS, d), F32),
            pltpu.VMEM((slots, d, tn), F32),
            pltpu.VMEM((slots, d, tn), F32),
            pltpu.VMEM((slots, tn, d), F32),
            pltpu.VMEM((d, tn), BF16),
            pltpu.VMEM((d, tn), BF16),
            pltpu.VMEM((tn, d), BF16),
            pltpu.SemaphoreType.DMA((2,)),
            pltpu.SemaphoreType.DMA((2,)),
            pltpu.SemaphoreType.DMA((slots,)),
        ],
    )
    return pl.pallas_call(
        functools.partial(_moe_kernel, chunks=chunks, n_col_steps=n_col_steps,
                          n_row_blocks=n_rows // gran),
        grid_spec=grid_spec,
        out_shape=jax.ShapeDtypeStruct((n_rows * chunks, LANES), F32),
        compiler_params=pltpu.CompilerParams(
            dimension_semantics=("arbitrary",),
            vmem_limit_bytes=56 * 1024 * 1024),
        name="moe",
    )(item_e, item_row0, item_nblk, used_blocks, live_items, xs, w_gate_up, w_down, bgu3, bd3)


def _combine_kernel(dest_ref, ys_hbm, x1_ref, gate_ref, fg_ref, o_ref, gbuf, sem,
                    *, tile, chunks, final_norm):
    i = pl.program_id(0)
    slot = i % 2
    per_tile = tile * TOP_K
    stride = _token_row_stride(chunks)

    def gather(step, dst_slot):
        base = step * per_tile

        def issue(g, carry):
            for u in range(DMA_ISSUE_UNROLL):
                t = g * DMA_ISSUE_UNROLL + u
                for k in range(TOP_K):
                    pltpu.make_async_copy(
                        _token_rows(ys_hbm, dest_ref[base + t * TOP_K + k], 1, chunks),
                        gbuf.at[dst_slot, pl.ds(pl.multiple_of((k * tile + t) * stride, SUBLANES),
                                                chunks), :],
                        sem.at[dst_slot]).start()
            return carry

        lax.fori_loop(0, tile // DMA_ISSUE_UNROLL, issue, 0)

    @pl.when(i == 0)
    def _():
        gather(0, 0)

    @pl.when(i + 1 < pl.num_programs(0))
    def _():
        gather(i + 1, 1 - slot)

    counted = gbuf.at[slot, pl.ds(0, per_tile * chunks), :]
    pltpu.make_async_copy(counted, counted, sem.at[slot]).wait()

    gates = gate_ref[...]
    gk = [gates[:, k:k + 1] for k in range(TOP_K)]
    cols = []
    for c in range(chunks):
        acc = x1_ref[:, c * LANES:(c + 1) * LANES]
        for k in range(TOP_K):
            acc = acc + gk[k] * gbuf[slot, pl.ds(k * tile * stride + c, tile, stride=stride), :]
        cols.append(acc)
    out = jnp.concatenate(cols, axis=-1)
    if final_norm:
        out = _rms(out, fg_ref[...])
    o_ref[...] = out


def _combine(dest, ys, x1, gates, final_g, final_norm, tile=128):
    n, d = x1.shape
    chunks = d // LANES
    steps = n // tile
    grid_spec = pltpu.PrefetchScalarGridSpec(
        num_scalar_prefetch=1,
        grid=(steps,),
        in_specs=[
            pl.BlockSpec(memory_space=pl.ANY),
            pl.BlockSpec((tile, d), lambda i, dest: (i, 0)),
            pl.BlockSpec((tile, LANES), lambda i, dest: (i, 0)),
            pl.BlockSpec((1, d), lambda i, dest: (0, 0)),
        ],
        out_specs=pl.BlockSpec((tile, d), lambda i, dest: (i, 0)),
        scratch_shapes=[pltpu.VMEM((2, TOP_K * tile * _token_row_stride(chunks), LANES), F32),
                        pltpu.SemaphoreType.DMA((2,))],
    )
    return pl.pallas_call(
        functools.partial(_combine_kernel, tile=tile, chunks=chunks, final_norm=final_norm),
        grid_spec=grid_spec,
        out_shape=jax.ShapeDtypeStruct((n, d), F32),
        compiler_params=pltpu.CompilerParams(dimension_semantics=("arbitrary",),
                                             vmem_limit_bytes=40 * 1024 * 1024),
        name="combine",
    )(dest, ys, x1, gates, final_g)


def _routing_tables(counts, idx, rank, n_rows_cap):
    n_experts = counts.shape[0]
    gran, item_rows = MOE_ROW_GRAN, MOE_ITEM_ROWS
    padded = ((counts + gran - 1) // gran) * gran
    ends = jnp.cumsum(padded)
    offs = ends - padded
    dest = offs[idx] + rank
    items_per = (padded + item_rows - 1) // item_rows
    item_ends = jnp.cumsum(items_per)
    item_starts = item_ends - items_per
    n_items = n_experts + n_rows_cap // item_rows
    slot = jnp.arange(n_items, dtype=jnp.int32)
    total = item_ends[-1]
    live = slot < total
    live_slot = jnp.minimum(slot, total - 1)
    e_of = jnp.minimum(jnp.sum((live_slot[:, None] >= item_ends[None, :]).astype(jnp.int32), axis=1),
                       n_experts - 1).astype(jnp.int32)
    local = slot - item_starts[e_of]
    row0 = offs[e_of] + local * item_rows
    rows = jnp.clip(padded[e_of] - local * item_rows, 0, item_rows)
    nblk = jnp.where(live, rows // gran, 0).astype(jnp.int32)
    row0 = jnp.where(live, row0, 0).astype(jnp.int32)
    used_blocks = (ends[-1:] // gran).astype(jnp.int32)
    pad_start = (offs + counts).astype(jnp.int32)
    pad_rows = (padded - counts).astype(jnp.int32)
    live_items = total.reshape(1).astype(jnp.int32)
    return dest.astype(jnp.int32), e_of, row0, nblk, used_blocks, live_items, pad_start, pad_rows


def _layer(x3, mix_g, w_in, conv_w, conv_g, attn_g, w_out, ffn_g, w_router, b_router,
           w_gate_up, b_gate_up, w_down, b_down, final_g, final_norm):
    b, seq, d = x3.shape
    n = b * seq
    cw_cols = conv_w.shape[1]
    aw_cols = attn_g.shape[0]
    n_experts = w_router.shape[1]
    chunks = d // LANES

    proj = _in_proj(x3.reshape(n, d), mix_g.reshape(1, d), w_in.astype(BF16))
    proj3 = proj.reshape(b, seq, -1)
    y_attn = _attention(proj3, cw_cols, aw_cols)
    x1, h2_tm, idx, gates, rank, counts = _out_proj(
        proj3, y_attn, x3, conv_w, conv_g.reshape(1, -1), attn_g.reshape(1, -1),
        w_out.astype(BF16), ffn_g.reshape(1, d), w_router, b_router.reshape(1, -1))

    n_rows = n * TOP_K + n_experts * MOE_ROW_GRAN
    (dest, item_e, item_row0, item_nblk, used_blocks, live_items, pad_start,
     pad_rows) = _routing_tables(counts[0], idx[:, :TOP_K], rank[:, :TOP_K], n_rows)
    dest_flat = dest.reshape(n * TOP_K)
    xs = _dispatch(dest_flat, pad_start, pad_rows, used_blocks, h2_tm, n_rows)
    ys = _moe(item_e, item_row0, item_nblk, used_blocks, live_items, xs, w_gate_up, b_gate_up,
              w_down, b_down, n_rows)
    out = _combine(dest_flat, ys, x1.reshape(n, d), gates, final_g.reshape(1, d), final_norm)
    return out.reshape(b, seq, d)


def kernel(x, mix_norm_g, w_in, conv_w, conv_norm_g, attn_norm_g, w_out, ffn_norm_g, w_router,
           b_router, w_gate_up, b_gate_up, w_down, b_down, final_norm_g):
    depth = w_in.shape[0]
    for layer in range(depth):
        x = _layer(x, mix_norm_g[layer], w_in[layer], conv_w[layer], conv_norm_g[layer],
                   attn_norm_g[layer], w_out[layer], ffn_norm_g[layer], w_router[layer],
                   b_router[layer], w_gate_up[layer], b_gate_up[layer], w_down[layer],
                   b_down[layer], final_norm_g, layer == depth - 1)
    return x
```

```python
import functools

import jax
import jax.numpy as jnp
from jax import lax
from jax.experimental import pallas as pl
from jax.experimental.pallas import tpu as pltpu

HEAD_DIM = 64
CONV_K = 3
ROT_DIM = HEAD_DIM // 4
ROPE_THETA = 500000.0
DILATIONS = (1, 4, 16)
ATTN_BLOCK = 128
TOP_K = 4
SWIGLU_LIMIT = 7.0
SWIGLU_ALPHA = 1.702
EPS = 1e-5

LANES = 128
SUBLANES = 8

MOE_ROW_GRAN = 256
MOE_ITEM_ROWS = 1536
MOE_COL_TILE = 256
MOE_IO_SLOTS = 4
BF16 = jnp.bfloat16
F32 = jnp.float32


def _token_row_stride(chunks):
    return chunks + SUBLANES


def _token_rows(ref, first_token, n_tokens, chunks):
    return ref.at[pl.ds(pl.multiple_of(first_token * chunks, chunks), n_tokens * chunks), :]


def _rms(x, g):
    return x * lax.rsqrt(jnp.mean(x * x, axis=-1, keepdims=True) + EPS) * g


def _in_proj_kernel(x_ref, g_ref, w_ref, o_ref, xn_ref):
    @pl.when(pl.program_id(1) == 0)
    def _():
        xn_ref[...] = _rms(x_ref[...], g_ref[...]).astype(BF16)

    o_ref[...] = jnp.dot(xn_ref[...], w_ref[...], preferred_element_type=F32)


def _in_proj(x2, g, w_bf16, tm=1024, tn=512):
    n, d = x2.shape
    cols = w_bf16.shape[1]
    return pl.pallas_call(
        _in_proj_kernel,
        grid=(n // tm, cols // tn),
        in_specs=[
            pl.BlockSpec((tm, d), lambda i, j: (i, 0)),
            pl.BlockSpec((1, d), lambda i, j: (0, 0)),
            pl.BlockSpec((d, tn), lambda i, j: (0, j)),
        ],
        out_specs=pl.BlockSpec((tm, tn), lambda i, j: (i, j)),
        out_shape=jax.ShapeDtypeStruct((n, cols), F32),
        scratch_shapes=[pltpu.VMEM((tm, d), BF16)],
        compiler_params=pltpu.CompilerParams(
            dimension_semantics=("arbitrary", "arbitrary"),
            vmem_limit_bytes=40 * 1024 * 1024),
        name="in_proj",
    )(x2, g, w_bf16)


def _attn_kernel(q_ref, k_ref, v_ref, cos_ref, sa_ref, sb_ref, o_ref,
                 qn, kn, q4, q16, a1, d1, m1, a4, d4, m4, a16, d16, m16, acm, dcm, mcm,
                 k1, va1, vb1, k4, va4, vb4, k16, va16, vb16, bias_s, *, seq):
    nblk_total = seq // ATTN_BLOCK
    lane = lax.broadcasted_iota(jnp.int32, (1, LANES), 1)
    head0 = lane < HEAD_DIM

    qi = lax.broadcasted_iota(jnp.int32, (ATTN_BLOCK, 2 * ATTN_BLOCK), 0)
    kj = lax.broadcasted_iota(jnp.int32, (ATTN_BLOCK, 2 * ATTN_BLOCK), 1)
    band = (kj >= qi) & (kj <= qi + ATTN_BLOCK)
    neg = jnp.float32(-jnp.inf)
    bias_s[0] = jnp.where(band, 0.0, neg)
    bias_s[1] = jnp.where(band & (kj >= ATTN_BLOCK), 0.0, neg)

    def rot(x):
        return (x * cos_ref[...] + pltpu.roll(x, LANES - ROT_DIM // 2, 1) * sa_ref[...]
                + pltpu.roll(x, ROT_DIM // 2, 1) * sb_ref[...])

    qn[...] = rot(q_ref[...]) * (HEAD_DIM ** -0.5)
    kn[...] = rot(k_ref[...])

    def put_kv(dst_rows, kc, vac, vbc, k, v):
        kc[dst_rows, :] = k.astype(BF16)
        vac[dst_rows, :] = jnp.where(head0, v, 1.0).astype(BF16)
        vbc[dst_rows, :] = jnp.where(head0, 1.0, v).astype(BF16)

    put_kv(slice(None), k1, va1, vb1, kn[...], v_ref[...])
    for d, qc, kv in ((4, q4, (k4, va4, vb4)), (16, q16, (k16, va16, vb16))):
        ln = seq // d
        for r in range(d):
            rows = slice(r * ln, (r + 1) * ln)
            qc[rows, :] = qn[pl.ds(r, ln, stride=d), :]
            put_kv(rows, *kv, kn[pl.ds(r, ln, stride=d), :], v_ref[pl.ds(r, ln, stride=d), :])

    unroll = 16

    def run_branch(qsrc, kv, dsts, blocks_per_class):
        ksrc, vsrcs = kv[0], kv[1:]
        a_dst, d_dst, m_dst = dsts

        def attend(b, first):
            cur = pl.ds(pl.multiple_of(b * ATTN_BLOCK, ATTN_BLOCK), ATTN_BLOCK)
            qb = qsrc[cur, :]
            prv = pl.ds(pl.multiple_of(jnp.maximum(b - 1, 0) * ATTN_BLOCK, ATTN_BLOCK),
                        ATTN_BLOCK)
            bias = bias_s[int(first)] if isinstance(first, bool) else bias_s[first]
            kk = jnp.concatenate([ksrc[prv, :], ksrc[cur, :]], axis=0)
            res, mx = [], []
            for h in range(2):
                mine = head0 if h == 0 else jnp.logical_not(head0)
                qh = jnp.where(mine, qb, 0.0).astype(BF16)
                s = lax.dot_general(qh, kk, (((1,), (1,)), ((), ())), preferred_element_type=F32)
                s = s + bias
                m = jnp.max(s, axis=-1, keepdims=True)
                p = jnp.exp(s - m).astype(BF16)
                vh = jnp.concatenate([vsrcs[h][prv, :], vsrcs[h][cur, :]], axis=0)
                res.append(jnp.dot(p, vh, preferred_element_type=F32))
                mx.append(m)
            a_dst[cur, :] = jnp.where(head0, res[0], res[1])
            d_dst[cur, :] = pltpu.roll(jnp.where(head0, res[1], res[0]), HEAD_DIM, 1)
            m_dst[cur, :] = jnp.where(head0, mx[0], mx[1])

        def body(i, carry):
            for u in range(unroll):
                b = i * unroll + u
                if unroll % blocks_per_class == 0:
                    first = u % blocks_per_class == 0
                elif blocks_per_class % unroll == 0 and u != 0:
                    first = False
                else:
                    first = jnp.where(jnp.asarray(b % blocks_per_class == 0), 1, 0)
                attend(b, first)
            return carry

        lax.fori_loop(0, nblk_total // unroll, body, 0)

    res1, res4, res16, res_cm = (a1, d1, m1), (a4, d4, m4), (a16, d16, m16), (acm, dcm, mcm)
    run_branch(qn, (k1, va1, vb1), res1, nblk_total)
    for d, qc, kv, res in ((4, q4, (k4, va4, vb4), res4), (16, q16, (k16, va16, vb16), res16)):
        run_branch(qc, kv, res_cm, nblk_total // d)
        ln = seq // d
        for cm, nat in zip(res_cm, res):
            for r in range(d):
                nat[pl.ds(r, ln, stride=d), :] = cm[r * ln:(r + 1) * ln, :]

    ma, mb, mc = m1[...], m4[...], m16[...]
    mx = jnp.maximum(jnp.maximum(ma, mb), mc)
    ea, eb, ec = jnp.exp(ma - mx), jnp.exp(mb - mx), jnp.exp(mc - mx)
    o_ref[...] = ((ea * a1[...] + eb * a4[...] + ec * a16[...])
                  / (ea * d1[...] + eb * d4[...] + ec * d16[...]))


def _rotary_tables(seq):
    half = ROT_DIM // 2
    inv_freq = ROPE_THETA ** (-jnp.arange(0, ROT_DIM, 2, dtype=F32) / ROT_DIM)
    ang = jnp.arange(seq, dtype=F32)[:, None] * inv_freq[None, :]
    cos, sin = jnp.cos(ang), jnp.sin(ang)
    pos = jnp.arange(LANES) % HEAD_DIM
    fidx = pos % half
    in_lo = pos < half
    in_hi = (pos >= half) & (pos < ROT_DIM)
    cos_t = jnp.where((in_lo | in_hi)[None, :], cos[:, fidx], 1.0)
    sa = jnp.where(in_lo[None, :], -sin[:, fidx], 0.0)
    sb = jnp.where(in_hi[None, :], sin[:, fidx], 0.0)
    return cos_t.astype(F32), sa.astype(F32), sb.astype(F32)


def _attention(proj3, conv_w_cols, attn_w_cols):
    b, seq, _ = proj3.shape
    assert seq % (ATTN_BLOCK * DILATIONS[-1]) == 0
    n_pairs = attn_w_cols // LANES
    qoff = 3 * conv_w_cols // LANES
    cos_t, sa, sb = _rotary_tables(seq)
    blk = lambda off: pl.BlockSpec((None, seq, LANES), lambda i, j: (i, 0, off + j))
    tab = pl.BlockSpec((seq, LANES), lambda i, j: (0, 0))
    big = pltpu.VMEM((seq, LANES), F32)
    return pl.pallas_call(
        functools.partial(_attn_kernel, seq=seq),
        grid=(b, n_pairs),
        in_specs=[blk(qoff), blk(qoff + n_pairs), blk(qoff + 2 * n_pairs), tab, tab, tab],
        out_specs=pl.BlockSpec((None, seq, LANES), lambda i, j: (i, 0, j)),
        out_shape=jax.ShapeDtypeStruct((b, seq, attn_w_cols), F32),
        scratch_shapes=([big] * 16 + [pltpu.VMEM((seq, LANES), BF16)] * 9
                        + [pltpu.VMEM((2, ATTN_BLOCK, 2 * ATTN_BLOCK), F32)]),
        compiler_params=pltpu.CompilerParams(
            dimension_semantics=("arbitrary", "arbitrary"),
            vmem_limit_bytes=48 * 1024 * 1024),
        name="attention",
    )(proj3, proj3, proj3, cos_t, sa, sb)


def _out_proj_kernel(cx_ref, cb_ref, cc_ref, hx_ref, hc_ref, ya_ref, x_ref,
                     cw_ref, cg_ref, ag_ref, wo_ref, fg_ref, wrh_ref, wrl_ref, br_ref,
                     x1_ref, h2_ref, idx_ref, gate_ref, rank_ref, cnt_ref, carry,
                     *, tile, n_experts):
    first_tile_of_seq = pl.program_id(1) == 0
    first_step = (pl.program_id(0) == 0) & first_tile_of_seq

    @pl.when(first_step)
    def _():
        carry[...] = jnp.zeros_like(carry)

    u = cc_ref[...] * cx_ref[...]
    uh = jnp.where(first_tile_of_seq, 0.0, hc_ref[...] * hx_ref[...])
    row8 = lax.broadcasted_iota(jnp.int32, (SUBLANES, 1), 0)

    def shifted(k):
        r = pltpu.roll(u, k, 0)
        top = jnp.where(row8 < k, pltpu.roll(uh, k, 0), r[:SUBLANES])
        return jnp.concatenate([top, r[SUBLANES:]], axis=0)

    conv = cw_ref[2:3, :] * u + cw_ref[1:2, :] * shifted(1) + cw_ref[0:1, :] * shifted(2)
    y_conv = cb_ref[...] * conv

    mixed = jnp.concatenate(
        [_rms(y_conv, cg_ref[...]), _rms(ya_ref[...], ag_ref[...])], axis=-1).astype(BF16)
    x1 = x_ref[...] + jnp.dot(mixed, wo_ref[...], preferred_element_type=F32)
    x1_ref[...] = x1
    h2 = _rms(x1, fg_ref[...])
    d = h2.shape[-1]
    chunks = d // LANES
    for c in range(chunks):
        h2_ref[pl.ds(c, tile, stride=chunks), :] = h2[:, c * LANES:(c + 1) * LANES]

    h2_hi = h2.astype(BF16)
    h2_lo = (h2 - h2_hi.astype(F32)).astype(BF16)
    logits = (jnp.dot(h2_hi, wrh_ref[...], preferred_element_type=F32)
              + jnp.dot(h2_hi, wrl_ref[...], preferred_element_type=F32)
              + jnp.dot(h2_lo, wrh_ref[...], preferred_element_type=F32)) + br_ref[...]
    eio = lax.broadcasted_iota(jnp.int32, (tile, n_experts), 1).astype(F32)
    work = logits
    vals, idxs = [], []
    for _ in range(TOP_K):
        m = jnp.max(work, axis=1, keepdims=True)
        ik = jnp.min(jnp.where(work == m, eio, float(n_experts)), axis=1, keepdims=True)
        vals.append(m)
        idxs.append(ik)
        work = jnp.where(eio == ik, -jnp.inf, work)
    exps = [jnp.exp(v - vals[0]) for v in vals]
    tot = exps[0] + exps[1] + exps[2] + exps[3]

    onehot = jnp.zeros((tile, n_experts), F32)
    for ik in idxs:
        onehot = onehot + (eio == ik).astype(F32)
    ri = lax.broadcasted_iota(jnp.int32, (tile, tile), 0)
    ci = lax.broadcasted_iota(jnp.int32, (tile, tile), 1)
    tri = (ci < ri).astype(BF16)
    before = jnp.dot(tri, onehot.astype(BF16), preferred_element_type=F32) + carry[...]
    carry[...] = carry[...] + jnp.sum(onehot, axis=0, keepdims=True)
    cnt_ref[...] = carry[...].astype(jnp.int32)

    lio = lax.broadcasted_iota(jnp.int32, (tile, LANES), 1)
    idx_out = jnp.zeros((tile, LANES), jnp.int32)
    gate_out = jnp.zeros((tile, LANES), F32)
    rank_out = jnp.zeros((tile, LANES), jnp.int32)
    for k in range(TOP_K):
        rk = jnp.sum(jnp.where(eio == idxs[k], before, 0.0), axis=1, keepdims=True)
        idx_out = jnp.where(lio == k, idxs[k].astype(jnp.int32), idx_out)
        gate_out = jnp.where(lio == k, exps[k] / tot, gate_out)
        rank_out = jnp.where(lio == k, rk.astype(jnp.int32), rank_out)
    idx_ref[...] = idx_out
    gate_ref[...] = gate_out
    rank_ref[...] = rank_out


def _out_proj(proj3, y_attn, x3, conv_w, conv_g, attn_g, w_out_bf16, ffn_g, w_router, b_router,
              tile=256):
    b, seq, d = x3.shape
    cw_cols = conv_w.shape[1]
    aw_cols = y_attn.shape[2]
    n_experts = w_router.shape[1]
    w_router_hi = w_router.astype(BF16)
    w_router_lo = (w_router - w_router_hi.astype(F32)).astype(BF16)
    chunks = d // LANES
    tiles = seq // tile
    n = b * seq
    halo_blocks = tile // SUBLANES
    row_blk = lambda width, col: pl.BlockSpec((None, tile, width), lambda i, j: (i, j, col))
    halo = lambda col: pl.BlockSpec(
        (None, SUBLANES, cw_cols), lambda i, j: (i, jnp.maximum(j * halo_blocks - 1, 0), col))
    const = lambda shape: pl.BlockSpec(shape, lambda i, j: (0,) * len(shape))
    flat = lambda width: pl.BlockSpec((tile, width), lambda i, j: (i * tiles + j, 0))
    outs = pl.pallas_call(
        functools.partial(_out_proj_kernel, tile=tile, n_experts=n_experts),
        grid=(b, tiles),
        in_specs=[
            row_blk(cw_cols, 0), row_blk(cw_cols, 1), row_blk(cw_cols, 2), halo(0), halo(2),
            row_blk(aw_cols, 0), row_blk(d, 0),
            const((CONV_K, cw_cols)), const((1, cw_cols)), const((1, aw_cols)),
            const((d, d)), const((1, d)), const((d, n_experts)), const((d, n_experts)),
            const((1, n_experts)),
        ],
        out_specs=[
            row_blk(d, 0),
            pl.BlockSpec((tile * chunks, LANES), lambda i, j: (i * tiles + j, 0)),
            flat(LANES), flat(LANES), flat(LANES),
            const((1, n_experts)),
        ],
        out_shape=[
            jax.ShapeDtypeStruct((b, seq, d), F32),
            jax.ShapeDtypeStruct((n * chunks, LANES), F32),
            jax.ShapeDtypeStruct((n, LANES), jnp.int32),
            jax.ShapeDtypeStruct((n, LANES), F32),
            jax.ShapeDtypeStruct((n, LANES), jnp.int32),
            jax.ShapeDtypeStruct((1, n_experts), jnp.int32),
        ],
        scratch_shapes=[pltpu.VMEM((1, n_experts), F32)],
        compiler_params=pltpu.CompilerParams(
            dimension_semantics=("arbitrary", "arbitrary"),
            vmem_limit_bytes=48 * 1024 * 1024),
        name="out_proj",
    )(proj3, proj3, proj3, proj3, proj3, y_attn, x3, conv_w, conv_g, attn_g, w_out_bf16,
      ffn_g, w_router_hi, w_router_lo, b_router)
    return outs


DMA_ISSUE_UNROLL = 4


def _dispatch_kernel(dest_ref, pad_start, pad_rows, used_blocks, h2_ref, xs_hbm, zbuf, sem, zsem,
                     *, tile, chunks, n_experts, n_row_blocks):
    gran = MOE_ROW_GRAN

    @pl.when(pl.program_id(0) == 0)
    def _():
        zbuf[...] = jnp.zeros_like(zbuf)

        def zero_copy(first_row, n_rows):
            return pltpu.make_async_copy(_token_rows(zbuf, 0, n_rows, chunks),
                                         _token_rows(xs_hbm, first_row, n_rows, chunks), zsem)

        def for_each_zero_run(act):
            def per_expert(e, carry):
                row = pad_start[e]
                left = pad_rows[e]
                for bit in reversed(range(gran.bit_length() - 1)):
                    take = (left >> bit) & 1

                    @pl.when(take == 1)
                    def _():
                        act(zero_copy(row, 1 << bit))

                    row = row + (take << bit)
                return carry

            lax.fori_loop(0, n_experts, per_expert, 0)

            def tail(blk, carry):
                act(zero_copy(blk * gran, gran))
                return carry

            lax.fori_loop(used_blocks[0], n_row_blocks, tail, 0)

        for_each_zero_run(lambda cp: cp.start())
        for_each_zero_run(lambda cp: cp.wait())

    def issue(g, carry):
        for u in range(DMA_ISSUE_UNROLL):
            t = g * DMA_ISSUE_UNROLL + u
            for k in range(TOP_K):
                pltpu.make_async_copy(
                    _token_rows(h2_ref, t, 1, chunks),
                    _token_rows(xs_hbm, dest_ref[0, 0, t * TOP_K + k], 1, chunks), sem).start()
        return carry

    lax.fori_loop(0, tile // DMA_ISSUE_UNROLL, issue, 0)
    counted = _token_rows(xs_hbm, 0, tile * TOP_K, chunks)
    pltpu.make_async_copy(counted, counted, sem).wait()


def _dispatch(dest, pad_start, pad_rows, used_blocks, h2_tm, n_rows, tile=256):
    steps = dest.shape[0] // (tile * TOP_K)
    dest3 = dest.reshape(steps, 1, tile * TOP_K)
    chunks = h2_tm.shape[0] // (dest.shape[0] // TOP_K)
    gran = MOE_ROW_GRAN
    smem = pl.BlockSpec(memory_space=pltpu.SMEM)
    return pl.pallas_call(
        functools.partial(_dispatch_kernel, tile=tile, chunks=chunks,
                          n_experts=pad_start.shape[0], n_row_blocks=n_rows // gran),
        grid=(steps,),
        in_specs=[
            pl.BlockSpec((1, 1, tile * TOP_K), lambda i: (i, 0, 0), memory_space=pltpu.SMEM),
            smem, smem, smem,
            pl.BlockSpec((tile * chunks, LANES), lambda i: (i, 0)),
        ],
        out_specs=pl.BlockSpec(memory_space=pl.ANY),
        out_shape=jax.ShapeDtypeStruct((n_rows * chunks, LANES), F32),
        scratch_shapes=[pltpu.VMEM((gran * chunks, LANES), F32),
                        pltpu.SemaphoreType.DMA(()), pltpu.SemaphoreType.DMA(())],
        compiler_params=pltpu.CompilerParams(dimension_semantics=("arbitrary",)),
        name="dispatch",
    )(dest3, pad_start, pad_rows, used_blocks, h2_tm)


MOE_WEIGHT_SLOTS = 3


def _moe_kernel(item_e, item_row0, item_nblk, used_blocks, live_items, xs_hbm, wgu_hbm, wdn_hbm,
                bgu_ref, bd_ref, ys_hbm, iobuf, xb, yacc, wgf, wuf, wdf, wgb, wub, wdb,
                sem_io, sem_w, *, chunks, n_col_steps, n_row_blocks):
    s = pl.program_id(0)
    nb = item_nblk[s]
    row0 = item_row0[s]
    gran = MOE_ROW_GRAN
    tn = MOE_COL_TILE
    de = n_col_steps * tn
    total_chunks = live_items[0] * n_col_steps

    def staged(slot):
        return iobuf.at[slot]

    def ys_block(blk):
        return _token_rows(ys_hbm, blk * gran, gran, chunks)

    def weight_copies(g):
        e = item_e[g // n_col_steps]
        col = pl.multiple_of((g % n_col_steps) * tn, tn)
        slot = g % MOE_WEIGHT_SLOTS
        return (
            pltpu.make_async_copy(wgu_hbm.at[e, :, pl.ds(col, tn)], wgf.at[slot], sem_w.at[slot]),
            pltpu.make_async_copy(wgu_hbm.at[e, :, pl.ds(pl.multiple_of(de + col, tn), tn)],
                                  wuf.at[slot], sem_w.at[slot]),
            pltpu.make_async_copy(wdn_hbm.at[e, pl.ds(col, tn), :], wdf.at[slot], sem_w.at[slot]),
        )

    def fetch(g):
        @pl.when(g < total_chunks)
        def _():
            for cp in weight_copies(g):
                cp.start()

    @pl.when(s == 0)
    def _():
        for g in range(MOE_WEIGHT_SLOTS - 1):
            fetch(g)

    @pl.when(nb > 0)
    def _():
        def io_slot(sb):
            return sb % MOE_IO_SLOTS

        def x_copy(sb):
            return pltpu.make_async_copy(
                _token_rows(xs_hbm, row0 + sb * gran, gran, chunks), staged(io_slot(sb)),
                sem_io.at[io_slot(sb)])

        def y_copy(sb):
            return pltpu.make_async_copy(staged(io_slot(sb)), ys_block(row0 // gran + sb),
                                         sem_io.at[io_slot(sb)])

        for sb in range(MOE_IO_SLOTS):
            @pl.when(sb < nb)
            def _():
                x_copy(sb).start()

        def block_rows(sb):
            return pl.ds(pl.multiple_of(sb * gran, gran), gran)

        def column_step(j, phase):
            g = s * n_col_steps + j
            slot = g % MOE_WEIGHT_SLOTS
            for cp in weight_copies(g):
                cp.wait()
            fetch(g + MOE_WEIGHT_SLOTS - 1)
            col = pl.multiple_of(j * tn, tn)
            bg = bgu_ref[:, pl.ds(col, tn)]
            bu = bgu_ref[:, pl.ds(pl.multiple_of(de + col, tn), tn)]

            def cast_weights():
                wgb[...] = wgf[slot].astype(BF16)
                wub[...] = wuf[slot].astype(BF16)
                wdb[...] = wdf[slot].astype(BF16)

            def partial_out(sb):
                x = xb[block_rows(sb), :]
                gate = jnp.dot(x, wgb[...], preferred_element_type=F32) + bg
                up = jnp.dot(x, wub[...], preferred_element_type=F32) + bu
                gate = jnp.minimum(gate, SWIGLU_LIMIT)
                up = jnp.clip(up, -SWIGLU_LIMIT, SWIGLU_LIMIT)
                act = (up + 1.0) * (gate * jax.nn.sigmoid(SWIGLU_ALPHA * gate))
                return jnp.dot(act.astype(BF16), wdb[...], preferred_element_type=F32)

            def trip(first, count):
                blocks = [first + u for u in range(count)]
                if phase == "first":
                    for sb in blocks:
                        x_copy(sb).wait()
                    for sb in blocks:
                        for c in range(chunks):
                            xb[block_rows(sb), c * LANES:(c + 1) * LANES] = (
                                iobuf[io_slot(sb), pl.ds(c, gran, stride=chunks), :].astype(BF16))
                if phase == "last":
                    for sb in blocks:
                        @pl.when(sb >= MOE_IO_SLOTS)
                        def _():
                            y_copy(sb - MOE_IO_SLOTS).wait()
                parts = [partial_out(sb) for sb in blocks]
                for sb, part in zip(blocks, parts):
                    if phase == "first":
                        yacc[block_rows(sb), :] = part + bd_ref[...]
                    elif phase == "middle":
                        yacc[block_rows(sb), :] += part
                    else:
                        y = yacc[block_rows(sb), :] + part
                        for c in range(chunks):
                            iobuf[io_slot(sb), pl.ds(c, gran, stride=chunks), :] = (
                                y[:, c * LANES:(c + 1) * LANES])
                for sb in blocks:
                    if phase == "first":
                        @pl.when(sb + MOE_IO_SLOTS < nb)
                        def _():
                            x_copy(sb + MOE_IO_SLOTS).start()
                    if phase == "last":
                        y_copy(sb).start()

            @pl.when(nb >= 2)
            def _():
                cast_weights()
                trip(0, 2)

                def body(i, c):
                    trip(2 * i, 2)
                    return c

                lax.fori_loop(1, nb // 2, body, 0)

                @pl.when(nb % 2 == 1)
                def _():
                    trip(nb - 1, 1)

            @pl.when(nb == 1)
            def _():
                cast_weights()
                trip(0, 1)

        column_step(0, "first")

        def middle(j, carry):
            column_step(j, "middle")
            return carry

        lax.fori_loop(1, n_col_steps - 1, middle, 0)
        column_step(n_col_steps - 1, "last")

        for k in range(MOE_IO_SLOTS):
            @pl.when(nb > k)
            def _():
                y_copy(nb - 1 - k).wait()

    @pl.when(s == pl.num_programs(0) - 1)
    def _():
        iobuf[0] = jnp.zeros(iobuf.shape[1:], iobuf.dtype)

        def fill(blk, carry):
            cp = pltpu.make_async_copy(staged(0), ys_block(blk), sem_io.at[0])
            cp.start()
            cp.wait()
            return carry

        lax.fori_loop(used_blocks[0], n_row_blocks, fill, 0)


def _moe(item_e, item_row0, item_nblk, used_blocks, live_items, xs, w_gate_up, b_gate_up, w_down,
         b_down, n_rows):
    n_experts, d, two_de = w_gate_up.shape
    de = two_de // 2
    chunks = d // LANES
    tn = MOE_COL_TILE
    n_col_steps = de // tn
    assert n_col_steps >= 2
    n_items = item_e.shape[0]
    gran = MOE_ROW_GRAN
    slots = MOE_WEIGHT_SLOTS

    bgu3 = b_gate_up.reshape(n_experts, 1, two_de)
    bd3 = b_down.reshape(n_experts, 1, d)
    per_expert = lambda width: pl.BlockSpec((None, 1, width), lambda s, e, *_: (e[s], 0, 0))
    grid_spec = pltpu.PrefetchScalarGridSpec(
        num_scalar_prefetch=5,
        grid=(n_items,),
        in_specs=[
            pl.BlockSpec(memory_space=pl.ANY),
            pl.BlockSpec(memory_space=pl.ANY),
            pl.BlockSpec(memory_space=pl.ANY),
            per_expert(two_de),
            per_expert(d),
        ],
        out_specs=pl.BlockSpec(memory_space=pl.ANY),
        scratch_shapes=[
            pltpu.VMEM((MOE_IO_SLOTS, gran * chunks, LANES), F32),
            pltpu.VMEM((MOE_ITEM_ROWS, d), BF16),
            pltpu.VMEM((MOE_ITEM_ROWS, d), F32),
            pltpu.VMEM((slots, d, tn), F32),
            pltpu.VMEM((slots, d, tn), F32),
            pltpu.VMEM((slots, tn, d), F32),
            pltpu.VMEM((d, tn), BF16),
            pltpu.VMEM((d, tn), BF16),
            pltpu.VMEM((tn, d), BF16),
            pltpu.SemaphoreType.DMA((MOE_IO_SLOTS,)),
            pltpu.SemaphoreType.DMA((slots,)),
        ],
    )
    return pl.pallas_call(
        functools.partial(_moe_kernel, chunks=chunks, n_col_steps=n_col_steps,
                          n_row_blocks=n_rows // gran),
        grid_spec=grid_spec,
        out_shape=jax.ShapeDtypeStruct((n_rows * chunks, LANES), F32),
        compiler_params=pltpu.CompilerParams(
            dimension_semantics=("arbitrary",),
            vmem_limit_bytes=56 * 1024 * 1024),
        name="moe",
    )(item_e, item_row0, item_nblk, used_blocks, live_items, xs, w_gate_up, w_down, bgu3, bd3)


def _combine_kernel(dest_ref, ys_hbm, x1_ref, gate_ref, fg_ref, o_ref, gbuf, sem,
                    *, tile, chunks, final_norm):
    i = pl.program_id(0)
    slot = i % 2
    per_tile = tile * TOP_K
    stride = _token_row_stride(chunks)

    def gather(step, dst_slot):
        base = step * per_tile

        def issue(g, carry):
            for u in range(DMA_ISSUE_UNROLL):
                t = g * DMA_ISSUE_UNROLL + u
                for k in range(TOP_K):
                    pltpu.make_async_copy(
                        _token_rows(ys_hbm, dest_ref[base + t * TOP_K + k], 1, chunks),
                        gbuf.at[dst_slot, pl.ds(pl.multiple_of((k * tile + t) * stride, SUBLANES),
                                                chunks), :],
                        sem.at[dst_slot]).start()
            return carry

        lax.fori_loop(0, tile // DMA_ISSUE_UNROLL, issue, 0)

    @pl.when(i == 0)
    def _():
        gather(0, 0)

    @pl.when(i + 1 < pl.num_programs(0))
    def _():
        gather(i + 1, 1 - slot)

    counted = gbuf.at[slot, pl.ds(0, per_tile * chunks), :]
    pltpu.make_async_copy(counted, counted, sem.at[slot]).wait()

    gates = gate_ref[...]
    gk = [gates[:, k:k + 1] for k in range(TOP_K)]
    cols = []
    for c in range(chunks):
        acc = x1_ref[:, c * LANES:(c + 1) * LANES]
        for k in range(TOP_K):
            acc = acc + gk[k] * gbuf[slot, pl.ds(k * tile * stride + c, tile, stride=stride), :]
        cols.append(acc)
    out = jnp.concatenate(cols, axis=-1)
    if final_norm:
        out = _rms(out, fg_ref[...])
    o_ref[...] = out


def _combine(dest, ys, x1, gates, final_g, final_norm, tile=128):
    n, d = x1.shape
    chunks = d // LANES
    steps = n // tile
    grid_spec = pltpu.PrefetchScalarGridSpec(
        num_scalar_prefetch=1,
        grid=(steps,),
        in_specs=[
            pl.BlockSpec(memory_space=pl.ANY),
            pl.BlockSpec((tile, d), lambda i, dest: (i, 0)),
            pl.BlockSpec((tile, LANES), lambda i, dest: (i, 0)),
            pl.BlockSpec((1, d), lambda i, dest: (0, 0)),
        ],
        out_specs=pl.BlockSpec((tile, d), lambda i, dest: (i, 0)),
        scratch_shapes=[pltpu.VMEM((2, TOP_K * tile * _token_row_stride(chunks), LANES), F32),
                        pltpu.SemaphoreType.DMA((2,))],
    )
    return pl.pallas_call(
        functools.partial(_combine_kernel, tile=tile, chunks=chunks, final_norm=final_norm),
        grid_spec=grid_spec,
        out_shape=jax.ShapeDtypeStruct((n, d), F32),
        compiler_params=pltpu.CompilerParams(dimension_semantics=("arbitrary",),
                                             vmem_limit_bytes=40 * 1024 * 1024),
        name="combine",
    )(dest, ys, x1, gates, final_g)


def _routing_tables(counts, idx, rank, n_rows_cap):
    n_experts = counts.shape[0]
    gran, item_rows = MOE_ROW_GRAN, MOE_ITEM_ROWS
    padded = ((counts + gran - 1) // gran) * gran
    ends = jnp.cumsum(padded)
    offs = ends - padded
    dest = offs[idx] + rank
    items_per = (padded + item_rows - 1) // item_rows
    item_ends = jnp.cumsum(items_per)
    item_starts = item_ends - items_per
    n_items = n_experts + n_rows_cap // item_rows
    slot = jnp.arange(n_items, dtype=jnp.int32)
    total = item_ends[-1]
    live = slot < total
    live_slot = jnp.minimum(slot, total - 1)
    e_of = jnp.minimum(jnp.sum((live_slot[:, None] >= item_ends[None, :]).astype(jnp.int32), axis=1),
                       n_experts - 1).astype(jnp.int32)
    local = slot - item_starts[e_of]
    row0 = offs[e_of] + local * item_rows
    rows = jnp.clip(padded[e_of] - local * item_rows, 0, item_rows)
    nblk = jnp.where(live, rows // gran, 0).astype(jnp.int32)
    row0 = jnp.where(live, row0, 0).astype(jnp.int32)
    used_blocks = (ends[-1:] // gran).astype(jnp.int32)
    pad_start = (offs + counts).astype(jnp.int32)
    pad_rows = (padded - counts).astype(jnp.int32)
    live_items = total.reshape(1).astype(jnp.int32)
    return dest.astype(jnp.int32), e_of, row0, nblk, used_blocks, live_items, pad_start, pad_rows


def _layer(x3, mix_g, w_in, conv_w, conv_g, attn_g, w_out, ffn_g, w_router, b_router,
           w_gate_up, b_gate_up, w_down, b_down, final_g, final_norm):
    b, seq, d = x3.shape
    n = b * seq
    cw_cols = conv_w.shape[1]
    aw_cols = attn_g.shape[0]
    n_experts = w_router.shape[1]
    chunks = d // LANES

    proj = _in_proj(x3.reshape(n, d), mix_g.reshape(1, d), w_in.astype(BF16))
    proj3 = proj.reshape(b, seq, -1)
    y_attn = _attention(proj3, cw_cols, aw_cols)
    x1, h2_tm, idx, gates, rank, counts = _out_proj(
        proj3, y_attn, x3, conv_w, conv_g.reshape(1, -1), attn_g.reshape(1, -1),
        w_out.astype(BF16), ffn_g.reshape(1, d), w_router, b_router.reshape(1, -1))

    n_rows = n * TOP_K + n_experts * MOE_ROW_GRAN
    (dest, item_e, item_row0, item_nblk, used_blocks, live_items, pad_start,
     pad_rows) = _routing_tables(counts[0], idx[:, :TOP_K], rank[:, :TOP_K], n_rows)
    dest_flat = dest.reshape(n * TOP_K)
    xs = _dispatch(dest_flat, pad_start, pad_rows, used_blocks, h2_tm, n_rows)
    ys = _moe(item_e, item_row0, item_nblk, used_blocks, live_items, xs, w_gate_up, b_gate_up,
              w_down, b_down, n_rows)
    out = _combine(dest_flat, ys, x1.reshape(n, d), gates, final_g.reshape(1, d), final_norm)
    return out.reshape(b, seq, d)


def kernel(x, mix_norm_g, w_in, conv_w, conv_norm_g, attn_norm_g, w_out, ffn_norm_g, w_router,
           b_router, w_gate_up, b_gate_up, w_down, b_down, final_norm_g):
    depth = w_in.shape[0]
    for layer in range(depth):
        x = _layer(x, mix_norm_g[layer], w_in[layer], conv_w[layer], conv_norm_g[layer],
                   attn_norm_g[layer], w_out[layer], ffn_norm_g[layer], w_router[layer],
                   b_router[layer], w_gate_up[layer], b_gate_up[layer], w_down[layer],
                   b_down[layer], final_norm_g, layer == depth - 1)
    return x
```

```python
import functools

import jax
import jax.numpy as jnp
from jax import lax
from jax.experimental import pallas as pl
from jax.experimental.pallas import tpu as pltpu

HEAD_DIM = 64
CONV_K = 3
ROT_DIM = HEAD_DIM // 4
ROPE_THETA = 500000.0
DILATIONS = (1, 4, 16)
ATTN_BLOCK = 128
TOP_K = 4
SWIGLU_LIMIT = 7.0
SWIGLU_ALPHA = 1.702
EPS = 1e-5

LANES = 128
SUBLANES = 8

MOE_ROW_GRAN = 256
MOE_ITEM_ROWS = 1280
MOE_COL_TILE = 256
MOE_TRIP_BLOCKS = 2
MOE_IO_SLOTS = 2 * MOE_TRIP_BLOCKS
BF16 = jnp.bfloat16
F32 = jnp.float32


def _token_row_stride(chunks):
    return chunks + SUBLANES


def _token_rows(ref, first_token, n_tokens, chunks):
    return ref.at[pl.ds(pl.multiple_of(first_token * chunks, chunks), n_tokens * chunks), :]


def _rms(x, g):
    return x * lax.rsqrt(jnp.mean(x * x, axis=-1, keepdims=True) + EPS) * g


def _in_proj_kernel(x_ref, g_ref, w_ref, o_ref, xn_ref):
    @pl.when(pl.program_id(1) == 0)
    def _():
        xn_ref[...] = _rms(x_ref[...], g_ref[...]).astype(BF16)

    o_ref[...] = jnp.dot(xn_ref[...], w_ref[...], preferred_element_type=F32)


def _in_proj(x2, g, w_bf16, tm=1024, tn=512):
    n, d = x2.shape
    cols = w_bf16.shape[1]
    return pl.pallas_call(
        _in_proj_kernel,
        grid=(n // tm, cols // tn),
        in_specs=[
            pl.BlockSpec((tm, d), lambda i, j: (i, 0)),
            pl.BlockSpec((1, d), lambda i, j: (0, 0)),
            pl.BlockSpec((d, tn), lambda i, j: (0, j)),
        ],
        out_specs=pl.BlockSpec((tm, tn), lambda i, j: (i, j)),
        out_shape=jax.ShapeDtypeStruct((n, cols), F32),
        scratch_shapes=[pltpu.VMEM((tm, d), BF16)],
        compiler_params=pltpu.CompilerParams(
            dimension_semantics=("arbitrary", "arbitrary"),
            vmem_limit_bytes=40 * 1024 * 1024),
        name="in_proj",
    )(x2, g, w_bf16)


def _attn_kernel(q_ref, k_ref, v_ref, cos_ref, sa_ref, sb_ref, o_ref,
                 qn, kn, q4, q16, a1, d1, m1, a4, d4, m4, a16, d16, m16, acm, dcm, mcm,
                 k1, va1, vb1, k4, va4, vb4, k16, va16, vb16, bias_s, *, seq):
    nblk_total = seq // ATTN_BLOCK
    lane = lax.broadcasted_iota(jnp.int32, (1, LANES), 1)
    head0 = lane < HEAD_DIM

    qi = lax.broadcasted_iota(jnp.int32, (ATTN_BLOCK, 2 * ATTN_BLOCK), 0)
    kj = lax.broadcasted_iota(jnp.int32, (ATTN_BLOCK, 2 * ATTN_BLOCK), 1)
    band = (kj >= qi) & (kj <= qi + ATTN_BLOCK)
    neg = jnp.float32(-jnp.inf)
    bias_s[0] = jnp.where(band, 0.0, neg)
    bias_s[1] = jnp.where(band & (kj >= ATTN_BLOCK), 0.0, neg)

    def rot(x):
        return (x * cos_ref[...] + pltpu.roll(x, LANES - ROT_DIM // 2, 1) * sa_ref[...]
                + pltpu.roll(x, ROT_DIM // 2, 1) * sb_ref[...])

    qn[...] = rot(q_ref[...]) * (HEAD_DIM ** -0.5)
    kn[...] = rot(k_ref[...])

    def put_kv(dst_rows, kc, vac, vbc, k, v):
        kc[dst_rows, :] = k.astype(BF16)
        vac[dst_rows, :] = jnp.where(head0, v, 1.0).astype(BF16)
        vbc[dst_rows, :] = jnp.where(head0, 1.0, v).astype(BF16)

    put_kv(slice(None), k1, va1, vb1, kn[...], v_ref[...])
    for d, qc, kv in ((4, q4, (k4, va4, vb4)), (16, q16, (k16, va16, vb16))):
        ln = seq // d
        for r in range(d):
            rows = slice(r * ln, (r + 1) * ln)
            qc[rows, :] = qn[pl.ds(r, ln, stride=d), :]
            put_kv(rows, *kv, kn[pl.ds(r, ln, stride=d), :], v_ref[pl.ds(r, ln, stride=d), :])

    unroll = 16

    def run_branch(qsrc, kv, dsts, blocks_per_class):
        ksrc, vsrcs = kv[0], kv[1:]
        a_dst, d_dst, m_dst = dsts

        def attend(b, first):
            cur = pl.ds(pl.multiple_of(b * ATTN_BLOCK, ATTN_BLOCK), ATTN_BLOCK)
            qb = qsrc[cur, :]
            prv = pl.ds(pl.multiple_of(jnp.maximum(b - 1, 0) * ATTN_BLOCK, ATTN_BLOCK),
                        ATTN_BLOCK)
            bias = bias_s[int(first)] if isinstance(first, bool) else bias_s[first]
            kk = jnp.concatenate([ksrc[prv, :], ksrc[cur, :]], axis=0)
            res, mx = [], []
            for h in range(2):
                mine = head0 if h == 0 else jnp.logical_not(head0)
                qh = jnp.where(mine, qb, 0.0).astype(BF16)
                s = lax.dot_general(qh, kk, (((1,), (1,)), ((), ())), preferred_element_type=F32)
                s = s + bias
                m = jnp.max(s, axis=-1, keepdims=True)
                p = jnp.exp(s - m).astype(BF16)
                vh = jnp.concatenate([vsrcs[h][prv, :], vsrcs[h][cur, :]], axis=0)
                res.append(jnp.dot(p, vh, preferred_element_type=F32))
                mx.append(m)
            a_dst[cur, :] = jnp.where(head0, res[0], res[1])
            d_dst[cur, :] = pltpu.roll(jnp.where(head0, res[1], res[0]), HEAD_DIM, 1)
            m_dst[cur, :] = jnp.where(head0, mx[0], mx[1])

        def body(i, carry):
            for u in range(unroll):
                b = i * unroll + u
                if unroll % blocks_per_class == 0:
                    first = u % blocks_per_class == 0
                elif blocks_per_class % unroll == 0 and u != 0:
                    first = False
                else:
                    first = jnp.where(jnp.asarray(b % blocks_per_class == 0), 1, 0)
                attend(b, first)
            return carry

        lax.fori_loop(0, nblk_total // unroll, body, 0)

    res1, res4, res16, res_cm = (a1, d1, m1), (a4, d4, m4), (a16, d16, m16), (acm, dcm, mcm)
    run_branch(qn, (k1, va1, vb1), res1, nblk_total)
    for d, qc, kv, res in ((4, q4, (k4, va4, vb4), res4), (16, q16, (k16, va16, vb16), res16)):
        run_branch(qc, kv, res_cm, nblk_total // d)
        ln = seq // d
        for cm, nat in zip(res_cm, res):
            for r in range(d):
                nat[pl.ds(r, ln, stride=d), :] = cm[r * ln:(r + 1) * ln, :]

    ma, mb, mc = m1[...], m4[...], m16[...]
    mx = jnp.maximum(jnp.maximum(ma, mb), mc)
    ea, eb, ec = jnp.exp(ma - mx), jnp.exp(mb - mx), jnp.exp(mc - mx)
    o_ref[...] = ((ea * a1[...] + eb * a4[...] + ec * a16[...])
                  / (ea * d1[...] + eb * d4[...] + ec * d16[...]))


def _rotary_tables(seq):
    half = ROT_DIM // 2
    inv_freq = ROPE_THETA ** (-jnp.arange(0, ROT_DIM, 2, dtype=F32) / ROT_DIM)
    ang = jnp.arange(seq, dtype=F32)[:, None] * inv_freq[None, :]
    cos, sin = jnp.cos(ang), jnp.sin(ang)
    pos = jnp.arange(LANES) % HEAD_DIM
    fidx = pos % half
    in_lo = pos < half
    in_hi = (pos >= half) & (pos < ROT_DIM)
    cos_t = jnp.where((in_lo | in_hi)[None, :], cos[:, fidx], 1.0)
    sa = jnp.where(in_lo[None, :], -sin[:, fidx], 0.0)
    sb = jnp.where(in_hi[None, :], sin[:, fidx], 0.0)
    return cos_t.astype(F32), sa.astype(F32), sb.astype(F32)


def _attention(proj3, conv_w_cols, attn_w_cols):
    b, seq, _ = proj3.shape
    assert seq % (ATTN_BLOCK * DILATIONS[-1]) == 0
    n_pairs = attn_w_cols // LANES
    qoff = 3 * conv_w_cols // LANES
    cos_t, sa, sb = _rotary_tables(seq)
    blk = lambda off: pl.BlockSpec((None, seq, LANES), lambda i, j: (i, 0, off + j))
    tab = pl.BlockSpec((seq, LANES), lambda i, j: (0, 0))
    big = pltpu.VMEM((seq, LANES), F32)
    return pl.pallas_call(
        functools.partial(_attn_kernel, seq=seq),
        grid=(b, n_pairs),
        in_specs=[blk(qoff), blk(qoff + n_pairs), blk(qoff + 2 * n_pairs), tab, tab, tab],
        out_specs=pl.BlockSpec((None, seq, LANES), lambda i, j: (i, 0, j)),
        out_shape=jax.ShapeDtypeStruct((b, seq, attn_w_cols), F32),
        scratch_shapes=([big] * 16 + [pltpu.VMEM((seq, LANES), BF16)] * 9
                        + [pltpu.VMEM((2, ATTN_BLOCK, 2 * ATTN_BLOCK), F32)]),
        compiler_params=pltpu.CompilerParams(
            dimension_semantics=("arbitrary", "arbitrary"),
            vmem_limit_bytes=48 * 1024 * 1024),
        name="attention",
    )(proj3, proj3, proj3, cos_t, sa, sb)


def _out_proj_kernel(cx_ref, cb_ref, cc_ref, hx_ref, hc_ref, ya_ref, x_ref,
                     cw_ref, cg_ref, ag_ref, wo_ref, fg_ref, wrh_ref, wrl_ref, br_ref,
                     x1_ref, h2_ref, idx_ref, gate_ref, rank_ref, cnt_ref, carry,
                     *, tile, n_experts):
    first_tile_of_seq = pl.program_id(1) == 0
    first_step = (pl.program_id(0) == 0) & first_tile_of_seq

    @pl.when(first_step)
    def _():
        carry[...] = jnp.zeros_like(carry)

    u = cc_ref[...] * cx_ref[...]
    uh = jnp.where(first_tile_of_seq, 0.0, hc_ref[...] * hx_ref[...])
    row8 = lax.broadcasted_iota(jnp.int32, (SUBLANES, 1), 0)

    def shifted(k):
        r = pltpu.roll(u, k, 0)
        top = jnp.where(row8 < k, pltpu.roll(uh, k, 0), r[:SUBLANES])
        return jnp.concatenate([top, r[SUBLANES:]], axis=0)

    conv = cw_ref[2:3, :] * u + cw_ref[1:2, :] * shifted(1) + cw_ref[0:1, :] * shifted(2)
    y_conv = cb_ref[...] * conv

    mixed = jnp.concatenate(
        [_rms(y_conv, cg_ref[...]), _rms(ya_ref[...], ag_ref[...])], axis=-1).astype(BF16)
    x1 = x_ref[...] + jnp.dot(mixed, wo_ref[...], preferred_element_type=F32)
    x1_ref[...] = x1
    h2 = _rms(x1, fg_ref[...])
    d = h2.shape[-1]
    chunks = d // LANES
    for c in range(chunks):
        h2_ref[pl.ds(c, tile, stride=chunks), :] = h2[:, c * LANES:(c + 1) * LANES]

    h2_hi = h2.astype(BF16)
    h2_lo = (h2 - h2_hi.astype(F32)).astype(BF16)
    logits = (jnp.dot(h2_hi, wrh_ref[...], preferred_element_type=F32)
              + jnp.dot(h2_hi, wrl_ref[...], preferred_element_type=F32)
              + jnp.dot(h2_lo, wrh_ref[...], preferred_element_type=F32)) + br_ref[...]
    eio = lax.broadcasted_iota(jnp.int32, (tile, n_experts), 1).astype(F32)
    work = logits
    vals, idxs = [], []
    for _ in range(TOP_K):
        m = jnp.max(work, axis=1, keepdims=True)
        ik = jnp.min(jnp.where(work == m, eio, float(n_experts)), axis=1, keepdims=True)
        vals.append(m)
        idxs.append(ik)
        work = jnp.where(eio == ik, -jnp.inf, work)
    exps = [jnp.exp(v - vals[0]) for v in vals]
    tot = exps[0] + exps[1] + exps[2] + exps[3]

    onehot = jnp.zeros((tile, n_experts), F32)
    for ik in idxs:
        onehot = onehot + (eio == ik).astype(F32)
    ri = lax.broadcasted_iota(jnp.int32, (tile, tile), 0)
    ci = lax.broadcasted_iota(jnp.int32, (tile, tile), 1)
    tri = (ci < ri).astype(BF16)
    before = jnp.dot(tri, onehot.astype(BF16), preferred_element_type=F32) + carry[...]
    carry[...] = carry[...] + jnp.sum(onehot, axis=0, keepdims=True)
    cnt_ref[...] = carry[...].astype(jnp.int32)

    lio = lax.broadcasted_iota(jnp.int32, (tile, LANES), 1)
    idx_out = jnp.zeros((tile, LANES), jnp.int32)
    gate_out = jnp.zeros((tile, LANES), F32)
    rank_out = jnp.zeros((tile, LANES), jnp.int32)
    for k in range(TOP_K):
        rk = jnp.sum(jnp.where(eio == idxs[k], before, 0.0), axis=1, keepdims=True)
        idx_out = jnp.where(lio == k, idxs[k].astype(jnp.int32), idx_out)
        gate_out = jnp.where(lio == k, exps[k] / tot, gate_out)
        rank_out = jnp.where(lio == k, rk.astype(jnp.int32), rank_out)
    idx_ref[...] = idx_out
    gate_ref[...] = gate_out
    rank_ref[...] = rank_out


def _out_proj(proj3, y_attn, x3, conv_w, conv_g, attn_g, w_out_bf16, ffn_g, w_router, b_router,
              tile=256):
    b, seq, d = x3.shape
    cw_cols = conv_w.shape[1]
    aw_cols = y_attn.shape[2]
    n_experts = w_router.shape[1]
    w_router_hi = w_router.astype(BF16)
    w_router_lo = (w_router - w_router_hi.astype(F32)).astype(BF16)
    chunks = d // LANES
    tiles = seq // tile
    n = b * seq
    halo_blocks = tile // SUBLANES
    row_blk = lambda width, col: pl.BlockSpec((None, tile, width), lambda i, j: (i, j, col))
    halo = lambda col: pl.BlockSpec(
        (None, SUBLANES, cw_cols), lambda i, j: (i, jnp.maximum(j * halo_blocks - 1, 0), col))
    const = lambda shape: pl.BlockSpec(shape, lambda i, j: (0,) * len(shape))
    flat = lambda width: pl.BlockSpec((tile, width), lambda i, j: (i * tiles + j, 0))
    outs = pl.pallas_call(
        functools.partial(_out_proj_kernel, tile=tile, n_experts=n_experts),
        grid=(b, tiles),
        in_specs=[
            row_blk(cw_cols, 0), row_blk(cw_cols, 1), row_blk(cw_cols, 2), halo(0), halo(2),
            row_blk(aw_cols, 0), row_blk(d, 0),
            const((CONV_K, cw_cols)), const((1, cw_cols)), const((1, aw_cols)),
            const((d, d)), const((1, d)), const((d, n_experts)), const((d, n_experts)),
            const((1, n_experts)),
        ],
        out_specs=[
            row_blk(d, 0),
            pl.BlockSpec((tile * chunks, LANES), lambda i, j: (i * tiles + j, 0)),
            flat(LANES), flat(LANES), flat(LANES),
            const((1, n_experts)),
        ],
        out_shape=[
            jax.ShapeDtypeStruct((b, seq, d), F32),
            jax.ShapeDtypeStruct((n * chunks, LANES), F32),
            jax.ShapeDtypeStruct((n, LANES), jnp.int32),
            jax.ShapeDtypeStruct((n, LANES), F32),
            jax.ShapeDtypeStruct((n, LANES), jnp.int32),
            jax.ShapeDtypeStruct((1, n_experts), jnp.int32),
        ],
        scratch_shapes=[pltpu.VMEM((1, n_experts), F32)],
        compiler_params=pltpu.CompilerParams(
            dimension_semantics=("arbitrary", "arbitrary"),
            vmem_limit_bytes=48 * 1024 * 1024),
        name="out_proj",
    )(proj3, proj3, proj3, proj3, proj3, y_attn, x3, conv_w, conv_g, attn_g, w_out_bf16,
      ffn_g, w_router_hi, w_router_lo, b_router)
    return outs


DMA_ISSUE_UNROLL = 4


def _dispatch_kernel(dest_ref, pad_start, pad_rows, used_blocks, h2_ref, xs_hbm, zbuf, sem, zsem,
                     *, tile, chunks, n_experts, n_row_blocks):
    gran = MOE_ROW_GRAN

    @pl.when(pl.program_id(0) == 0)
    def _():
        zbuf[...] = jnp.zeros_like(zbuf)

        def zero_copy(first_row, n_rows):
            return pltpu.make_async_copy(_token_rows(zbuf, 0, n_rows, chunks),
                                         _token_rows(xs_hbm, first_row, n_rows, chunks), zsem)

        def for_each_zero_run(act):
            def per_expert(e, carry):
                row = pad_start[e]
                left = pad_rows[e]
                for bit in reversed(range(gran.bit_length() - 1)):
                    take = (left >> bit) & 1

                    @pl.when(take == 1)
                    def _():
                        act(zero_copy(row, 1 << bit))

                    row = row + (take << bit)
                return carry

            lax.fori_loop(0, n_experts, per_expert, 0)

            def tail(blk, carry):
                act(zero_copy(blk * gran, gran))
                return carry

            lax.fori_loop(used_blocks[0], n_row_blocks, tail, 0)

        for_each_zero_run(lambda cp: cp.start())
        for_each_zero_run(lambda cp: cp.wait())

    def issue(g, carry):
        for u in range(DMA_ISSUE_UNROLL):
            t = g * DMA_ISSUE_UNROLL + u
            for k in range(TOP_K):
                pltpu.make_async_copy(
                    _token_rows(h2_ref, t, 1, chunks),
                    _token_rows(xs_hbm, dest_ref[0, 0, t * TOP_K + k], 1, chunks), sem).start()
        return carry

    lax.fori_loop(0, tile // DMA_ISSUE_UNROLL, issue, 0)
    counted = _token_rows(xs_hbm, 0, tile * TOP_K, chunks)
    pltpu.make_async_copy(counted, counted, sem).wait()


def _dispatch(dest, pad_start, pad_rows, used_blocks, h2_tm, n_rows, tile=256):
    steps = dest.shape[0] // (tile * TOP_K)
    dest3 = dest.reshape(steps, 1, tile * TOP_K)
    chunks = h2_tm.shape[0] // (dest.shape[0] // TOP_K)
    gran = MOE_ROW_GRAN
    smem = pl.BlockSpec(memory_space=pltpu.SMEM)
    return pl.pallas_call(
        functools.partial(_dispatch_kernel, tile=tile, chunks=chunks,
                          n_experts=pad_start.shape[0], n_row_blocks=n_rows // gran),
        grid=(steps,),
        in_specs=[
            pl.BlockSpec((1, 1, tile * TOP_K), lambda i: (i, 0, 0), memory_space=pltpu.SMEM),
            smem, smem, smem,
            pl.BlockSpec((tile * chunks, LANES), lambda i: (i, 0)),
        ],
        out_specs=pl.BlockSpec(memory_space=pl.ANY),
        out_shape=jax.ShapeDtypeStruct((n_rows * chunks, LANES), F32),
        scratch_shapes=[pltpu.VMEM((gran * chunks, LANES), F32),
                        pltpu.SemaphoreType.DMA(()), pltpu.SemaphoreType.DMA(())],
        compiler_params=pltpu.CompilerParams(dimension_semantics=("arbitrary",)),
        name="dispatch",
    )(dest3, pad_start, pad_rows, used_blocks, h2_tm)


MOE_WEIGHT_SLOTS = 3
MOE_GU_SPAN = 2


def _moe_kernel(item_e, item_row0, item_nblk, used_blocks, live_items, xs_hbm, wgu_hbm, wdn_hbm,
                bgu_ref, bd_ref, ys_hbm, iobuf, xb, yacc, wgf, wuf, wdf, wgb, wub, wdb,
                sem_io, sem_w, sem_gu, *, chunks, n_col_steps, n_row_blocks):
    s = pl.program_id(0)
    nb = item_nblk[s]
    row0 = item_row0[s]
    gran = MOE_ROW_GRAN
    tn = MOE_COL_TILE
    de = n_col_steps * tn
    total_chunks = live_items[0] * n_col_steps

    def staged(slot):
        return iobuf.at[slot]

    def ys_block(blk):
        return _token_rows(ys_hbm, blk * gran, gran, chunks)

    span = MOE_GU_SPAN
    wide = span * tn

    def gate_up_copies(w):
        e = item_e[w // (n_col_steps // span)]
        col = pl.multiple_of((w % (n_col_steps // span)) * wide, wide)
        slot = w % 2
        return (
            pltpu.make_async_copy(wgu_hbm.at[e, :, pl.ds(col, wide)], wgf.at[slot],
                                  sem_gu.at[slot]),
            pltpu.make_async_copy(wgu_hbm.at[e, :, pl.ds(pl.multiple_of(de + col, wide), wide)],
                                  wuf.at[slot], sem_gu.at[slot]),
        )

    def down_copy(g):
        e = item_e[g // n_col_steps]
        row = pl.multiple_of((g % n_col_steps) * tn, tn)
        slot = g % MOE_WEIGHT_SLOTS
        return pltpu.make_async_copy(wdn_hbm.at[e, pl.ds(row, tn), :], wdf.at[slot],
                                     sem_w.at[slot])

    def fetch_gate_up(w):
        @pl.when(w * span < total_chunks)
        def _():
            for cp in gate_up_copies(w):
                cp.start()

    def fetch_down(g):
        @pl.when(g < total_chunks)
        def _():
            down_copy(g).start()

    @pl.when(s == 0)
    def _():
        fetch_gate_up(0)
        for g in range(MOE_WEIGHT_SLOTS - 1):
            fetch_down(g)

    @pl.when(nb > 0)
    def _():
        def io_slot(sb):
            return sb % MOE_IO_SLOTS

        def x_copy(sb):
            return pltpu.make_async_copy(
                _token_rows(xs_hbm, row0 + sb * gran, gran, chunks), staged(io_slot(sb)),
                sem_io.at[io_slot(sb)])

        def y_copy(sb):
            return pltpu.make_async_copy(staged(io_slot(sb)), ys_block(row0 // gran + sb),
                                         sem_io.at[io_slot(sb)])

        for sb in range(MOE_IO_SLOTS):
            @pl.when(sb < nb)
            def _():
                x_copy(sb).start()

        def block_rows(sb):
            return pl.ds(pl.multiple_of(sb * gran, gran), gran)

        def column_step(j, phase):
            g = s * n_col_steps + j
            w = g // span
            slot = g % MOE_WEIGHT_SLOTS

            @pl.when(g % span == 0)
            def _():
                for cp in gate_up_copies(w):
                    cp.wait()
                fetch_gate_up(w + 1)

            down_copy(g).wait()
            fetch_down(g + MOE_WEIGHT_SLOTS - 1)
            col = pl.multiple_of(j * tn, tn)
            bg = bgu_ref[:, pl.ds(col, tn)]
            bu = bgu_ref[:, pl.ds(pl.multiple_of(de + col, tn), tn)]
            sub = pl.ds(pl.multiple_of((g % span) * tn, tn), tn)

            def cast_weights():
                wgb[...] = wgf[w % 2, :, sub].astype(BF16)
                wub[...] = wuf[w % 2, :, sub].astype(BF16)
                wdb[...] = wdf[slot].astype(BF16)

            def partial_out(sb):
                x = xb[block_rows(sb), :]
                gate = jnp.dot(x, wgb[...], preferred_element_type=F32) + bg
                up = jnp.dot(x, wub[...], preferred_element_type=F32) + bu
                gate = jnp.minimum(gate, SWIGLU_LIMIT)
                up = jnp.clip(up, -SWIGLU_LIMIT, SWIGLU_LIMIT)
                act = (up + 1.0) * (gate * jax.nn.sigmoid(SWIGLU_ALPHA * gate))
                return jnp.dot(act.astype(BF16), wdb[...], preferred_element_type=F32)

            def trip(first, count):
                blocks = [first + u for u in range(count)]
                if phase == "first":
                    for sb in blocks:
                        x_copy(sb).wait()
                    for sb in blocks:
                        for c in range(chunks):
                            xb[block_rows(sb), c * LANES:(c + 1) * LANES] = (
                                iobuf[io_slot(sb), pl.ds(c, gran, stride=chunks), :].astype(BF16))
                if phase == "last":
                    for sb in blocks:
                        @pl.when(sb >= MOE_IO_SLOTS)
                        def _():
                            y_copy(sb - MOE_IO_SLOTS).wait()
                parts = [partial_out(sb) for sb in blocks]
                for sb, part in zip(blocks, parts):
                    if phase == "first":
                        yacc[block_rows(sb), :] = part + bd_ref[...]
                    elif phase == "middle":
                        yacc[block_rows(sb), :] += part
                    else:
                        y = yacc[block_rows(sb), :] + part
                        for c in range(chunks):
                            iobuf[io_slot(sb), pl.ds(c, gran, stride=chunks), :] = (
                                y[:, c * LANES:(c + 1) * LANES])
                for sb in blocks:
                    if phase == "first":
                        @pl.when(sb + MOE_IO_SLOTS < nb)
                        def _():
                            x_copy(sb + MOE_IO_SLOTS).start()
                    if phase == "last":
                        y_copy(sb).start()

            full = MOE_TRIP_BLOCKS

            def tail(left):
                count = full // 2
                while count >= 1:
                    @pl.when(left % (2 * count) >= count)
                    def _():
                        trip(nb - left % (2 * count), count)

                    count //= 2

            @pl.when(nb >= full)
            def _():
                cast_weights()
                trip(0, full)

                def body(i, c):
                    trip(i * full, full)
                    return c

                lax.fori_loop(1, nb // full, body, 0)
                tail(nb % full)

            @pl.when(nb < full)
            def _():
                cast_weights()
                tail(nb)

        column_step(0, "first")

        def middle(j, carry):
            column_step(j, "middle")
            return carry

        lax.fori_loop(1, n_col_steps - 1, middle, 0)
        column_step(n_col_steps - 1, "last")

        for k in range(MOE_IO_SLOTS):
            @pl.when(nb > k)
            def _():
                y_copy(nb - 1 - k).wait()

    @pl.when(s == pl.num_programs(0) - 1)
    def _():
        iobuf[0] = jnp.zeros(iobuf.shape[1:], iobuf.dtype)

        def fill(blk, carry):
            cp = pltpu.make_async_copy(staged(0), ys_block(blk), sem_io.at[0])
            cp.start()
            cp.wait()
            return carry

        lax.fori_loop(used_blocks[0], n_row_blocks, fill, 0)


def _moe(item_e, item_row0, item_nblk, used_blocks, live_items, xs, w_gate_up, b_gate_up, w_down,
         b_down, n_rows):
    n_experts, d, two_de = w_gate_up.shape
    de = two_de // 2
    chunks = d // LANES
    tn = MOE_COL_TILE
    n_col_steps = de // tn
    assert n_col_steps >= 2
    assert n_col_steps % MOE_GU_SPAN == 0
    n_items = item_e.shape[0]
    gran = MOE_ROW_GRAN
    slots = MOE_WEIGHT_SLOTS

    bgu3 = b_gate_up.reshape(n_experts, 1, two_de)
    bd3 = b_down.reshape(n_experts, 1, d)
    per_expert = lambda width: pl.BlockSpec((None, 1, width), lambda s, e, *_: (e[s], 0, 0))
    grid_spec = pltpu.PrefetchScalarGridSpec(
        num_scalar_prefetch=5,
        grid=(n_items,),
        in_specs=[
            pl.BlockSpec(memory_space=pl.ANY),
            pl.BlockSpec(memory_space=pl.ANY),
            pl.BlockSpec(memory_space=pl.ANY),
            per_expert(two_de),
            per_expert(d),
        ],
        out_specs=pl.BlockSpec(memory_space=pl.ANY),
        scratch_shapes=[
            pltpu.VMEM((MOE_IO_SLOTS, gran * chunks, LANES), F32),
            pltpu.VMEM((MOE_ITEM_ROWS, d), BF16),
            pltpu.VMEM((MOE_ITEM_ROWS, d), F32),
            pltpu.VMEM((2, d, MOE_GU_SPAN * tn), F32),
            pltpu.VMEM((2, d, MOE_GU_SPAN * tn), F32),
            pltpu.VMEM((slots, tn, d), F32),
            pltpu.VMEM((d, tn), BF16),
            pltpu.VMEM((d, tn), BF16),
            pltpu.VMEM((tn, d), BF16),
            pltpu.SemaphoreType.DMA((MOE_IO_SLOTS,)),
            pltpu.SemaphoreType.DMA((slots,)),
            pltpu.SemaphoreType.DMA((2,)),
        ],
    )
    return pl.pallas_call(
        functools.partial(_moe_kernel, chunks=chunks, n_col_steps=n_col_steps,
                          n_row_blocks=n_rows // gran),
        grid_spec=grid_spec,
        out_shape=jax.ShapeDtypeStruct((n_rows * chunks, LANES), F32),
        compiler_params=pltpu.CompilerParams(
            dimension_semantics=("arbitrary",),
            vmem_limit_bytes=56 * 1024 * 1024),
        name="moe",
    )(item_e, item_row0, item_nblk, used_blocks, live_items, xs, w_gate_up, w_down, bgu3, bd3)


def _combine_kernel(dest_ref, ys_hbm, x1_ref, gate_ref, fg_ref, o_ref, gbuf, sem,
                    *, tile, chunks, final_norm):
    i = pl.program_id(0)
    slot = i % 2
    per_tile = tile * TOP_K
    stride = _token_row_stride(chunks)

    def gather(step, dst_slot):
        base = step * per_tile

        def issue(g, carry):
            for u in range(DMA_ISSUE_UNROLL):
                t = g * DMA_ISSUE_UNROLL + u
                for k in range(TOP_K):
                    pltpu.make_async_copy(
                        _token_rows(ys_hbm, dest_ref[base + t * TOP_K + k], 1, chunks),
                        gbuf.at[dst_slot, pl.ds(pl.multiple_of((k * tile + t) * stride, SUBLANES),
                                                chunks), :],
                        sem.at[dst_slot]).start()
            return carry

        lax.fori_loop(0, tile // DMA_ISSUE_UNROLL, issue, 0)

    @pl.when(i == 0)
    def _():
        gather(0, 0)

    @pl.when(i + 1 < pl.num_programs(0))
    def _():
        gather(i + 1, 1 - slot)

    counted = gbuf.at[slot, pl.ds(0, per_tile * chunks), :]
    pltpu.make_async_copy(counted, counted, sem.at[slot]).wait()

    gates = gate_ref[...]
    gk = [gates[:, k:k + 1] for k in range(TOP_K)]
    cols = []
    for c in range(chunks):
        acc = x1_ref[:, c * LANES:(c + 1) * LANES]
        for k in range(TOP_K):
            acc = acc + gk[k] * gbuf[slot, pl.ds(k * tile * stride + c, tile, stride=stride), :]
        cols.append(acc)
    out = jnp.concatenate(cols, axis=-1)
    if final_norm:
        out = _rms(out, fg_ref[...])
    o_ref[...] = out


def _combine(dest, ys, x1, gates, final_g, final_norm, tile=128):
    n, d = x1.shape
    chunks = d // LANES
    steps = n // tile
    grid_spec = pltpu.PrefetchScalarGridSpec(
        num_scalar_prefetch=1,
        grid=(steps,),
        in_specs=[
            pl.BlockSpec(memory_space=pl.ANY),
            pl.BlockSpec((tile, d), lambda i, dest: (i, 0)),
            pl.BlockSpec((tile, LANES), lambda i, dest: (i, 0)),
            pl.BlockSpec((1, d), lambda i, dest: (0, 0)),
        ],
        out_specs=pl.BlockSpec((tile, d), lambda i, dest: (i, 0)),
        scratch_shapes=[pltpu.VMEM((2, TOP_K * tile * _token_row_stride(chunks), LANES), F32),
                        pltpu.SemaphoreType.DMA((2,))],
    )
    return pl.pallas_call(
        functools.partial(_combine_kernel, tile=tile, chunks=chunks, final_norm=final_norm),
        grid_spec=grid_spec,
        out_shape=jax.ShapeDtypeStruct((n, d), F32),
        compiler_params=pltpu.CompilerParams(dimension_semantics=("arbitrary",),
                                             vmem_limit_bytes=40 * 1024 * 1024),
        name="combine",
    )(dest, ys, x1, gates, final_g)


def _routing_tables(counts, idx, rank, n_rows_cap):
    n_experts = counts.shape[0]
    gran, item_rows = MOE_ROW_GRAN, MOE_ITEM_ROWS
    padded = ((counts + gran - 1) // gran) * gran
    ends = jnp.cumsum(padded)
    offs = ends - padded
    dest = offs[idx] + rank
    items_per = (padded + item_rows - 1) // item_rows
    item_ends = jnp.cumsum(items_per)
    item_starts = item_ends - items_per
    n_items = n_experts + n_rows_cap // item_rows
    slot = jnp.arange(n_items, dtype=jnp.int32)
    total = item_ends[-1]
    live = slot < total
    live_slot = jnp.minimum(slot, total - 1)
    e_of = jnp.minimum(jnp.sum((live_slot[:, None] >= item_ends[None, :]).astype(jnp.int32), axis=1),
                       n_experts - 1).astype(jnp.int32)
    local = slot - item_starts[e_of]
    row0 = offs[e_of] + local * item_rows
    rows = jnp.clip(padded[e_of] - local * item_rows, 0, item_rows)
    nblk = jnp.where(live, rows // gran, 0).astype(jnp.int32)
    row0 = jnp.where(live, row0, 0).astype(jnp.int32)
    used_blocks = (ends[-1:] // gran).astype(jnp.int32)
    pad_start = (offs + counts).astype(jnp.int32)
    pad_rows = (padded - counts).astype(jnp.int32)
    live_items = total.reshape(1).astype(jnp.int32)
    return dest.astype(jnp.int32), e_of, row0, nblk, used_blocks, live_items, pad_start, pad_rows


def _layer(x3, mix_g, w_in, conv_w, conv_g, attn_g, w_out, ffn_g, w_router, b_router,
           w_gate_up, b_gate_up, w_down, b_down, final_g, final_norm):
    b, seq, d = x3.shape
    n = b * seq
    cw_cols = conv_w.shape[1]
    aw_cols = attn_g.shape[0]
    n_experts = w_router.shape[1]
    chunks = d // LANES

    proj = _in_proj(x3.reshape(n, d), mix_g.reshape(1, d), w_in.astype(BF16))
    proj3 = proj.reshape(b, seq, -1)
    y_attn = _attention(proj3, cw_cols, aw_cols)
    x1, h2_tm, idx, gates, rank, counts = _out_proj(
        proj3, y_attn, x3, conv_w, conv_g.reshape(1, -1), attn_g.reshape(1, -1),
        w_out.astype(BF16), ffn_g.reshape(1, d), w_router, b_router.reshape(1, -1))

    n_rows = n * TOP_K + n_experts * MOE_ROW_GRAN
    (dest, item_e, item_row0, item_nblk, used_blocks, live_items, pad_start,
     pad_rows) = _routing_tables(counts[0], idx[:, :TOP_K], rank[:, :TOP_K], n_rows)
    dest_flat = dest.reshape(n * TOP_K)
    xs = _dispatch(dest_flat, pad_start, pad_rows, used_blocks, h2_tm, n_rows)
    ys = _moe(item_e, item_row0, item_nblk, used_blocks, live_items, xs, w_gate_up, b_gate_up,
              w_down, b_down, n_rows)
    out = _combine(dest_flat, ys, x1.reshape(n, d), gates, final_g.reshape(1, d), final_norm)
    return out.reshape(b, seq, d)


def kernel(x, mix_norm_g, w_in, conv_w, conv_norm_g, attn_norm_g, w_out, ffn_norm_g, w_router,
           b_router, w_gate_up, b_gate_up, w_down, b_down, final_norm_g):
    depth = w_in.shape[0]
    for layer in range(depth):
        x = _layer(x, mix_norm_g[layer], w_in[layer], conv_w[layer], conv_norm_g[layer],
                   attn_norm_g[layer], w_out[layer], ffn_norm_g[layer], w_router[layer],
                   b_router[layer], w_gate_up[layer], b_gate_up[layer], w_down[layer],
                   b_down[layer], final_norm_g, layer == depth - 1)
    return x
```

```python
import functools

import jax
import jax.numpy as jnp
from jax import lax
from jax.experimental import pallas as pl
from jax.experimental.pallas import tpu as pltpu

HEAD_DIM = 64
CONV_K = 3
ROT_DIM = HEAD_DIM // 4
ROPE_THETA = 500000.0
DILATIONS = (1, 4, 16)
ATTN_BLOCK = 128
TOP_K = 4
SWIGLU_LIMIT = 7.0
SWIGLU_ALPHA = 1.702
EPS = 1e-5

LANES = 128
SUBLANES = 8

MOE_ROW_GRAN = 256
MOE_ITEM_ROWS = 1536
MOE_COL_TILE = 256
MOE_TRIP_BLOCKS = 2
MOE_IO_SLOTS = 2 * MOE_TRIP_BLOCKS
BF16 = jnp.bfloat16
F32 = jnp.float32


def _token_row_stride(chunks):
    return chunks + SUBLANES


def _token_rows(ref, first_token, n_tokens, chunks):
    return ref.at[pl.ds(pl.multiple_of(first_token * chunks, chunks), n_tokens * chunks), :]


def _rms(x, g):
    return x * lax.rsqrt(jnp.mean(x * x, axis=-1, keepdims=True) + EPS) * g


def _in_proj_kernel(x_ref, g_ref, w_ref, o_ref, xn_ref):
    @pl.when(pl.program_id(1) == 0)
    def _():
        xn_ref[...] = _rms(x_ref[...], g_ref[...]).astype(BF16)

    o_ref[...] = jnp.dot(xn_ref[...], w_ref[...], preferred_element_type=F32)


def _in_proj(x2, g, w_bf16, tm=1024):
    n, d = x2.shape
    cols = w_bf16.shape[1]
    tn = 1024 if cols % 1024 == 0 else 512
    return pl.pallas_call(
        _in_proj_kernel,
        grid=(n // tm, cols // tn),
        in_specs=[
            pl.BlockSpec((tm, d), lambda i, j: (i, 0)),
            pl.BlockSpec((1, d), lambda i, j: (0, 0)),
            pl.BlockSpec((d, tn), lambda i, j: (0, j)),
        ],
        out_specs=pl.BlockSpec((tm, tn), lambda i, j: (i, j)),
        out_shape=jax.ShapeDtypeStruct((n, cols), F32),
        scratch_shapes=[pltpu.VMEM((tm, d), BF16)],
        compiler_params=pltpu.CompilerParams(
            dimension_semantics=("arbitrary", "arbitrary"),
            vmem_limit_bytes=48 * 1024 * 1024),
        name="in_proj",
    )(x2, g, w_bf16)


def _attn_kernel(q_ref, k_ref, v_ref, cos_ref, sa_ref, sb_ref, o_ref,
                 qn, kn, q4, q16, a1, d1, m1, a4, d4, m4, a16, d16, m16, acm, dcm, mcm,
                 k1, va1, vb1, k4, va4, vb4, k16, va16, vb16, bias_s, *, seq):
    nblk_total = seq // ATTN_BLOCK
    lane = lax.broadcasted_iota(jnp.int32, (1, LANES), 1)
    head0 = lane < HEAD_DIM

    qi = lax.broadcasted_iota(jnp.int32, (ATTN_BLOCK, 2 * ATTN_BLOCK), 0)
    kj = lax.broadcasted_iota(jnp.int32, (ATTN_BLOCK, 2 * ATTN_BLOCK), 1)
    band = (kj >= qi) & (kj <= qi + ATTN_BLOCK)
    neg = jnp.float32(-jnp.inf)
    bias_s[0] = jnp.where(band, 0.0, neg)
    bias_s[1] = jnp.where(band & (kj >= ATTN_BLOCK), 0.0, neg)

    def rot(x):
        return (x * cos_ref[...] + pltpu.roll(x, LANES - ROT_DIM // 2, 1) * sa_ref[...]
                + pltpu.roll(x, ROT_DIM // 2, 1) * sb_ref[...])

    qn[...] = rot(q_ref[...]) * (HEAD_DIM ** -0.5)
    kn[...] = rot(k_ref[...])

    def put_kv(dst_rows, kc, vac, vbc, k, v):
        kc[dst_rows, :] = k.astype(BF16)
        vac[dst_rows, :] = jnp.where(head0, v, 1.0).astype(BF16)
        vbc[dst_rows, :] = jnp.where(head0, 1.0, v).astype(BF16)

    put_kv(slice(None), k1, va1, vb1, kn[...], v_ref[...])
    for d, qc, kv in ((4, q4, (k4, va4, vb4)), (16, q16, (k16, va16, vb16))):
        ln = seq // d
        for r in range(d):
            rows = slice(r * ln, (r + 1) * ln)
            qc[rows, :] = qn[pl.ds(r, ln, stride=d), :]
            put_kv(rows, *kv, kn[pl.ds(r, ln, stride=d), :], v_ref[pl.ds(r, ln, stride=d), :])

    unroll = 16

    def run_branch(qsrc, kv, dsts, blocks_per_class):
        ksrc, vsrcs = kv[0], kv[1:]
        a_dst, d_dst, m_dst = dsts

        def attend(b, first):
            cur = pl.ds(pl.multiple_of(b * ATTN_BLOCK, ATTN_BLOCK), ATTN_BLOCK)
            qb = qsrc[cur, :]
            prv = pl.ds(pl.multiple_of(jnp.maximum(b - 1, 0) * ATTN_BLOCK, ATTN_BLOCK),
                        ATTN_BLOCK)
            bias = bias_s[int(first)] if isinstance(first, bool) else bias_s[first]
            kk = jnp.concatenate([ksrc[prv, :], ksrc[cur, :]], axis=0)
            res, mx = [], []
            for h in range(2):
                mine = head0 if h == 0 else jnp.logical_not(head0)
                qh = jnp.where(mine, qb, 0.0).astype(BF16)
                s = lax.dot_general(qh, kk, (((1,), (1,)), ((), ())), preferred_element_type=F32)
                s = s + bias
                m = jnp.max(s, axis=-1, keepdims=True)
                p = jnp.exp(s - m).astype(BF16)
                vh = jnp.concatenate([vsrcs[h][prv, :], vsrcs[h][cur, :]], axis=0)
                res.append(jnp.dot(p, vh, preferred_element_type=F32))
                mx.append(m)
            a_dst[cur, :] = jnp.where(head0, res[0], res[1])
            d_dst[cur, :] = pltpu.roll(jnp.where(head0, res[1], res[0]), HEAD_DIM, 1)
            m_dst[cur, :] = jnp.where(head0, mx[0], mx[1])

        def body(i, carry):
            for u in range(unroll):
                b = i * unroll + u
                if unroll % blocks_per_class == 0:
                    first = u % blocks_per_class == 0
                elif blocks_per_class % unroll == 0 and u != 0:
                    first = False
                else:
                    first = jnp.where(jnp.asarray(b % blocks_per_class == 0), 1, 0)
                attend(b, first)
            return carry

        lax.fori_loop(0, nblk_total // unroll, body, 0)

    res1, res4, res16, res_cm = (a1, d1, m1), (a4, d4, m4), (a16, d16, m16), (acm, dcm, mcm)
    run_branch(qn, (k1, va1, vb1), res1, nblk_total)
    for d, qc, kv, res in ((4, q4, (k4, va4, vb4), res4), (16, q16, (k16, va16, vb16), res16)):
        run_branch(qc, kv, res_cm, nblk_total // d)
        ln = seq // d
        for cm, nat in zip(res_cm, res):
            for r in range(d):
                nat[pl.ds(r, ln, stride=d), :] = cm[r * ln:(r + 1) * ln, :]

    ma, mb, mc = m1[...], m4[...], m16[...]
    mx = jnp.maximum(jnp.maximum(ma, mb), mc)
    ea, eb, ec = jnp.exp(ma - mx), jnp.exp(mb - mx), jnp.exp(mc - mx)
    o_ref[...] = ((ea * a1[...] + eb * a4[...] + ec * a16[...])
                  / (ea * d1[...] + eb * d4[...] + ec * d16[...]))


def _rotary_tables(seq):
    half = ROT_DIM // 2
    inv_freq = ROPE_THETA ** (-jnp.arange(0, ROT_DIM, 2, dtype=F32) / ROT_DIM)
    ang = jnp.arange(seq, dtype=F32)[:, None] * inv_freq[None, :]
    cos, sin = jnp.cos(ang), jnp.sin(ang)
    pos = jnp.arange(LANES) % HEAD_DIM
    fidx = pos % half
    in_lo = pos < half
    in_hi = (pos >= half) & (pos < ROT_DIM)
    cos_t = jnp.where((in_lo | in_hi)[None, :], cos[:, fidx], 1.0)
    sa = jnp.where(in_lo[None, :], -sin[:, fidx], 0.0)
    sb = jnp.where(in_hi[None, :], sin[:, fidx], 0.0)
    return cos_t.astype(F32), sa.astype(F32), sb.astype(F32)


def _attention(proj3, conv_w_cols, attn_w_cols):
    b, seq, _ = proj3.shape
    assert seq % (ATTN_BLOCK * DILATIONS[-1]) == 0
    n_pairs = attn_w_cols // LANES
    qoff = 3 * conv_w_cols // LANES
    cos_t, sa, sb = _rotary_tables(seq)
    blk = lambda off: pl.BlockSpec((None, seq, LANES), lambda i, j: (i, 0, off + j))
    tab = pl.BlockSpec((seq, LANES), lambda i, j: (0, 0))
    big = pltpu.VMEM((seq, LANES), F32)
    return pl.pallas_call(
        functools.partial(_attn_kernel, seq=seq),
        grid=(b, n_pairs),
        in_specs=[blk(qoff), blk(qoff + n_pairs), blk(qoff + 2 * n_pairs), tab, tab, tab],
        out_specs=pl.BlockSpec((None, seq, LANES), lambda i, j: (i, 0, j)),
        out_shape=jax.ShapeDtypeStruct((b, seq, attn_w_cols), F32),
        scratch_shapes=([big] * 16 + [pltpu.VMEM((seq, LANES), BF16)] * 9
                        + [pltpu.VMEM((2, ATTN_BLOCK, 2 * ATTN_BLOCK), F32)]),
        compiler_params=pltpu.CompilerParams(
            dimension_semantics=("arbitrary", "arbitrary"),
            vmem_limit_bytes=48 * 1024 * 1024),
        name="attention",
    )(proj3, proj3, proj3, cos_t, sa, sb)


def _out_proj_kernel(cx_ref, cb_ref, cc_ref, hx_ref, hc_ref, ya_ref, x_ref,
                     cw_ref, cg_ref, ag_ref, wo_ref, fg_ref, wrh_ref, wrl_ref, br_ref,
                     x1_ref, h2_ref, idx_ref, gate_ref, rank_ref, cnt_ref, carry,
                     *, tile, n_experts):
    first_tile_of_seq = pl.program_id(1) == 0
    first_step = (pl.program_id(0) == 0) & first_tile_of_seq

    @pl.when(first_step)
    def _():
        carry[...] = jnp.zeros_like(carry)

    u = cc_ref[...] * cx_ref[...]
    uh = jnp.where(first_tile_of_seq, 0.0, hc_ref[...] * hx_ref[...])
    row8 = lax.broadcasted_iota(jnp.int32, (SUBLANES, 1), 0)

    def shifted(k):
        r = pltpu.roll(u, k, 0)
        top = jnp.where(row8 < k, pltpu.roll(uh, k, 0), r[:SUBLANES])
        return jnp.concatenate([top, r[SUBLANES:]], axis=0)

    conv = cw_ref[2:3, :] * u + cw_ref[1:2, :] * shifted(1) + cw_ref[0:1, :] * shifted(2)
    y_conv = cb_ref[...] * conv

    mixed = jnp.concatenate(
        [_rms(y_conv, cg_ref[...]), _rms(ya_ref[...], ag_ref[...])], axis=-1).astype(BF16)
    x1 = x_ref[...] + jnp.dot(mixed, wo_ref[...], preferred_element_type=F32)
    x1_ref[...] = x1
    h2 = _rms(x1, fg_ref[...])
    d = h2.shape[-1]
    chunks = d // LANES
    for c in range(chunks):
        h2_ref[pl.ds(c, tile, stride=chunks), :] = h2[:, c * LANES:(c + 1) * LANES]

    h2_hi = h2.astype(BF16)
    h2_lo = (h2 - h2_hi.astype(F32)).astype(BF16)
    logits = (jnp.dot(h2_hi, wrh_ref[...], preferred_element_type=F32)
              + jnp.dot(h2_hi, wrl_ref[...], preferred_element_type=F32)
              + jnp.dot(h2_lo, wrh_ref[...], preferred_element_type=F32)) + br_ref[...]
    eio = lax.broadcasted_iota(jnp.int32, (tile, n_experts), 1).astype(F32)
    work = logits
    vals, idxs = [], []
    for _ in range(TOP_K):
        m = jnp.max(work, axis=1, keepdims=True)
        ik = jnp.min(jnp.where(work == m, eio, float(n_experts)), axis=1, keepdims=True)
        vals.append(m)
        idxs.append(ik)
        work = jnp.where(eio == ik, -jnp.inf, work)
    exps = [jnp.exp(v - vals[0]) for v in vals]
    tot = exps[0] + exps[1] + exps[2] + exps[3]

    onehot = jnp.zeros((tile, n_experts), F32)
    for ik in idxs:
        onehot = onehot + (eio == ik).astype(F32)
    ri = lax.broadcasted_iota(jnp.int32, (tile, tile), 0)
    ci = lax.broadcasted_iota(jnp.int32, (tile, tile), 1)
    tri = (ci < ri).astype(BF16)
    before = jnp.dot(tri, onehot.astype(BF16), preferred_element_type=F32) + carry[...]
    carry[...] = carry[...] + jnp.sum(onehot, axis=0, keepdims=True)
    cnt_ref[...] = carry[...].astype(jnp.int32)

    lio = lax.broadcasted_iota(jnp.int32, (tile, LANES), 1)
    idx_out = jnp.zeros((tile, LANES), jnp.int32)
    gate_out = jnp.zeros((tile, LANES), F32)
    rank_out = jnp.zeros((tile, LANES), jnp.int32)
    for k in range(TOP_K):
        rk = jnp.sum(jnp.where(eio == idxs[k], before, 0.0), axis=1, keepdims=True)
        idx_out = jnp.where(lio == k, idxs[k].astype(jnp.int32), idx_out)
        gate_out = jnp.where(lio == k, exps[k] / tot, gate_out)
        rank_out = jnp.where(lio == k, rk.astype(jnp.int32), rank_out)
    idx_ref[...] = idx_out
    gate_ref[...] = gate_out
    rank_ref[...] = rank_out


def _out_proj(proj3, y_attn, x3, conv_w, conv_g, attn_g, w_out_bf16, ffn_g, w_router, b_router,
              tile=256):
    b, seq, d = x3.shape
    cw_cols = conv_w.shape[1]
    aw_cols = y_attn.shape[2]
    n_experts = w_router.shape[1]
    w_router_hi = w_router.astype(BF16)
    w_router_lo = (w_router - w_router_hi.astype(F32)).astype(BF16)
    chunks = d // LANES
    tiles = seq // tile
    n = b * seq
    halo_blocks = tile // SUBLANES
    row_blk = lambda width, col: pl.BlockSpec((None, tile, width), lambda i, j: (i, j, col))
    halo = lambda col: pl.BlockSpec(
        (None, SUBLANES, cw_cols), lambda i, j: (i, jnp.maximum(j * halo_blocks - 1, 0), col))
    const = lambda shape: pl.BlockSpec(shape, lambda i, j: (0,) * len(shape))
    flat = lambda width: pl.BlockSpec((tile, width), lambda i, j: (i * tiles + j, 0))
    outs = pl.pallas_call(
        functools.partial(_out_proj_kernel, tile=tile, n_experts=n_experts),
        grid=(b, tiles),
        in_specs=[
            row_blk(cw_cols, 0), row_blk(cw_cols, 1), row_blk(cw_cols, 2), halo(0), halo(2),
            row_blk(aw_cols, 0), row_blk(d, 0),
            const((CONV_K, cw_cols)), const((1, cw_cols)), const((1, aw_cols)),
            const((d, d)), const((1, d)), const((d, n_experts)), const((d, n_experts)),
            const((1, n_experts)),
        ],
        out_specs=[
            row_blk(d, 0),
            pl.BlockSpec((tile * chunks, LANES), lambda i, j: (i * tiles + j, 0)),
            flat(LANES), flat(LANES), flat(LANES),
            const((1, n_experts)),
        ],
        out_shape=[
            jax.ShapeDtypeStruct((b, seq, d), F32),
            jax.ShapeDtypeStruct((n * chunks, LANES), F32),
            jax.ShapeDtypeStruct((n, LANES), jnp.int32),
            jax.ShapeDtypeStruct((n, LANES), F32),
            jax.ShapeDtypeStruct((n, LANES), jnp.int32),
            jax.ShapeDtypeStruct((1, n_experts), jnp.int32),
        ],
        scratch_shapes=[pltpu.VMEM((1, n_experts), F32)],
        compiler_params=pltpu.CompilerParams(
            dimension_semantics=("arbitrary", "arbitrary"),
            vmem_limit_bytes=48 * 1024 * 1024),
        name="out_proj",
    )(proj3, proj3, proj3, proj3, proj3, y_attn, x3, conv_w, conv_g, attn_g, w_out_bf16,
      ffn_g, w_router_hi, w_router_lo, b_router)
    return outs


DMA_ISSUE_UNROLL = 4


def _dispatch_kernel(idx_ref, rank_ref, offs_ref, pad_start, pad_rows, used_blocks, h2_ref, xs_hbm,
                     zbuf, sem, zsem, *, tile, chunks, n_experts, n_row_blocks):
    gran = MOE_ROW_GRAN

    @pl.when(pl.program_id(0) == 0)
    def _():
        zbuf[...] = jnp.zeros_like(zbuf)

        def zero_copy(first_row, n_rows):
            return pltpu.make_async_copy(_token_rows(zbuf, 0, n_rows, chunks),
                                         _token_rows(xs_hbm, first_row, n_rows, chunks), zsem)

        def for_each_zero_run(act):
            def per_expert(e, carry):
                row = pad_start[e]
                left = pad_rows[e]
                for bit in reversed(range(gran.bit_length() - 1)):
                    take = (left >> bit) & 1

                    @pl.when(take == 1)
                    def _():
                        act(zero_copy(row, 1 << bit))

                    row = row + (take << bit)
                return carry

            lax.fori_loop(0, n_experts, per_expert, 0)

            def tail(blk, carry):
                act(zero_copy(blk * gran, gran))
                return carry

            lax.fori_loop(used_blocks[0], n_row_blocks, tail, 0)

        for_each_zero_run(lambda cp: cp.start())
        for_each_zero_run(lambda cp: cp.wait())

    def issue(g, carry):
        for u in range(DMA_ISSUE_UNROLL):
            t = g * DMA_ISSUE_UNROLL + u
            for k in range(TOP_K):
                a = t * TOP_K + k
                dest = offs_ref[idx_ref[0, 0, a]] + rank_ref[0, 0, a]
                pltpu.make_async_copy(_token_rows(h2_ref, t, 1, chunks),
                                      _token_rows(xs_hbm, dest, 1, chunks), sem).start()
        return carry

    lax.fori_loop(0, tile // DMA_ISSUE_UNROLL, issue, 0)
    counted = _token_rows(xs_hbm, 0, tile * TOP_K, chunks)
    pltpu.make_async_copy(counted, counted, sem).wait()


def _dispatch(idx_flat, rank_flat, offs, pad_start, pad_rows, used_blocks, h2_tm, n_rows, tile=256):
    steps = idx_flat.shape[0] // (tile * TOP_K)
    per_tile = lambda a: a.reshape(steps, 1, tile * TOP_K)
    chunks = h2_tm.shape[0] // (idx_flat.shape[0] // TOP_K)
    gran = MOE_ROW_GRAN
    smem = pl.BlockSpec(memory_space=pltpu.SMEM)
    tile_smem = pl.BlockSpec((1, 1, tile * TOP_K), lambda i: (i, 0, 0), memory_space=pltpu.SMEM)
    return pl.pallas_call(
        functools.partial(_dispatch_kernel, tile=tile, chunks=chunks,
                          n_experts=pad_start.shape[0], n_row_blocks=n_rows // gran),
        grid=(steps,),
        in_specs=[
            tile_smem, tile_smem, smem, smem, smem, smem,
            pl.BlockSpec((tile * chunks, LANES), lambda i: (i, 0)),
        ],
        out_specs=pl.BlockSpec(memory_space=pl.ANY),
        out_shape=jax.ShapeDtypeStruct((n_rows * chunks, LANES), F32),
        scratch_shapes=[pltpu.VMEM((gran * chunks, LANES), F32),
                        pltpu.SemaphoreType.DMA(()), pltpu.SemaphoreType.DMA(())],
        compiler_params=pltpu.CompilerParams(dimension_semantics=("arbitrary",)),
        name="dispatch",
    )(per_tile(idx_flat), per_tile(rank_flat), offs, pad_start, pad_rows, used_blocks, h2_tm)


MOE_WEIGHT_SLOTS = 3


def _moe_kernel(item_e, item_row0, item_nblk, used_blocks, live_items, xs_hbm, wgu_hbm, wdn_hbm,
                bgu_ref, bd_ref, ys_hbm, iobuf, xb, yacc, wgf, wuf, wdf, wgb, wub, wdb,
                sem_io, sem_w, *, chunks, n_col_steps, n_row_blocks):
    s = pl.program_id(0)
    nb = item_nblk[s]
    row0 = item_row0[s]
    gran = MOE_ROW_GRAN
    tn = MOE_COL_TILE
    de = n_col_steps * tn
    total_chunks = live_items[0] * n_col_steps

    def staged(slot):
        return iobuf.at[slot]

    def ys_block(blk):
        return _token_rows(ys_hbm, blk * gran, gran, chunks)

    def weight_copies(g):
        e = item_e[g // n_col_steps]
        col = pl.multiple_of((g % n_col_steps) * tn, tn)
        slot = g % MOE_WEIGHT_SLOTS
        return (
            pltpu.make_async_copy(wgu_hbm.at[e, :, pl.ds(col, tn)], wgf.at[slot], sem_w.at[slot]),
            pltpu.make_async_copy(wgu_hbm.at[e, :, pl.ds(pl.multiple_of(de + col, tn), tn)],
                                  wuf.at[slot], sem_w.at[slot]),
            pltpu.make_async_copy(wdn_hbm.at[e, pl.ds(col, tn), :], wdf.at[slot], sem_w.at[slot]),
        )

    def fetch(g):
        @pl.when(g < total_chunks)
        def _():
            for cp in weight_copies(g):
                cp.start()

    @pl.when(s == 0)
    def _():
        for g in range(MOE_WEIGHT_SLOTS - 1):
            fetch(g)

    @pl.when(nb > 0)
    def _():
        def io_slot(sb):
            return sb % MOE_IO_SLOTS

        def x_copy(sb):
            return pltpu.make_async_copy(
                _token_rows(xs_hbm, row0 + sb * gran, gran, chunks), staged(io_slot(sb)),
                sem_io.at[io_slot(sb)])

        def y_copy(sb):
            return pltpu.make_async_copy(staged(io_slot(sb)), ys_block(row0 // gran + sb),
                                         sem_io.at[io_slot(sb)])

        for sb in range(MOE_IO_SLOTS):
            @pl.when(sb < nb)
            def _():
                x_copy(sb).start()

        def block_rows(sb):
            return pl.ds(pl.multiple_of(sb * gran, gran), gran)

        def column_step(j, phase):
            g = s * n_col_steps + j
            slot = g % MOE_WEIGHT_SLOTS
            for cp in weight_copies(g):
                cp.wait()
            fetch(g + MOE_WEIGHT_SLOTS - 1)
            col = pl.multiple_of(j * tn, tn)
            bg = bgu_ref[:, pl.ds(col, tn)]
            bu = bgu_ref[:, pl.ds(pl.multiple_of(de + col, tn), tn)]

            def cast_weights():
                wgb[...] = wgf[slot].astype(BF16)
                wub[...] = wuf[slot].astype(BF16)
                wdb[...] = wdf[slot].astype(BF16)

            def partial_out(sb):
                x = xb[block_rows(sb), :]
                gate = jnp.dot(x, wgb[...], preferred_element_type=F32) + bg
                up = jnp.dot(x, wub[...], preferred_element_type=F32) + bu
                gate = jnp.minimum(gate, SWIGLU_LIMIT)
                up = jnp.clip(up, -SWIGLU_LIMIT, SWIGLU_LIMIT)
                act = (up + 1.0) * (gate * jax.nn.sigmoid(SWIGLU_ALPHA * gate))
                return jnp.dot(act.astype(BF16), wdb[...], preferred_element_type=F32)

            def trip(first, count):
                blocks = [first + u for u in range(count)]
                if phase == "first":
                    for sb in blocks:
                        x_copy(sb).wait()
                    for sb in blocks:
                        for c in range(chunks):
                            xb[block_rows(sb), c * LANES:(c + 1) * LANES] = (
                                iobuf[io_slot(sb), pl.ds(c, gran, stride=chunks), :].astype(BF16))
                if phase == "last":
                    for sb in blocks:
                        @pl.when(sb >= MOE_IO_SLOTS)
                        def _():
                            y_copy(sb - MOE_IO_SLOTS).wait()
                parts = [partial_out(sb) for sb in blocks]
                for sb, part in zip(blocks, parts):
                    if phase == "first":
                        yacc[block_rows(sb), :] = part + bd_ref[...]
                    elif phase == "middle":
                        yacc[block_rows(sb), :] += part
                    else:
                        y = yacc[block_rows(sb), :] + part
                        for c in range(chunks):
                            iobuf[io_slot(sb), pl.ds(c, gran, stride=chunks), :] = (
                                y[:, c * LANES:(c + 1) * LANES])
                for sb in blocks:
                    if phase == "first":
                        @pl.when(sb + MOE_IO_SLOTS < nb)
                        def _():
                            x_copy(sb + MOE_IO_SLOTS).start()
                    if phase == "last":
                        y_copy(sb).start()

            full = MOE_TRIP_BLOCKS

            def tail(left):
                count = full // 2
                while count >= 1:
                    @pl.when(left % (2 * count) >= count)
                    def _():
                        trip(nb - left % (2 * count), count)

                    count //= 2

            @pl.when(nb >= full)
            def _():
                cast_weights()
                trip(0, full)

                def body(i, c):
                    trip(i * full, full)
                    return c

                lax.fori_loop(1, nb // full, body, 0)
                tail(nb % full)

            @pl.when(nb < full)
            def _():
                cast_weights()
                tail(nb)

        column_step(0, "first")

        def middle(j, carry):
            column_step(j, "middle")
            return carry

        lax.fori_loop(1, n_col_steps - 1, middle, 0)
        column_step(n_col_steps - 1, "last")

        for k in range(MOE_IO_SLOTS):
            @pl.when(nb > k)
            def _():
                y_copy(nb - 1 - k).wait()

    @pl.when(s == pl.num_programs(0) - 1)
    def _():
        iobuf[0] = jnp.zeros(iobuf.shape[1:], iobuf.dtype)

        def fill(blk, carry):
            cp = pltpu.make_async_copy(staged(0), ys_block(blk), sem_io.at[0])
            cp.start()
            cp.wait()
            return carry

        lax.fori_loop(used_blocks[0], n_row_blocks, fill, 0)


def _moe(item_e, item_row0, item_nblk, used_blocks, live_items, xs, w_gate_up, b_gate_up, w_down,
         b_down, n_rows):
    n_experts, d, two_de = w_gate_up.shape
    de = two_de // 2
    chunks = d // LANES
    tn = MOE_COL_TILE
    n_col_steps = de // tn
    assert n_col_steps >= 2
    n_items = item_e.shape[0]
    gran = MOE_ROW_GRAN
    slots = MOE_WEIGHT_SLOTS

    bgu3 = b_gate_up.reshape(n_experts, 1, two_de)
    bd3 = b_down.reshape(n_experts, 1, d)
    per_expert = lambda width: pl.BlockSpec((None, 1, width), lambda s, e, *_: (e[s], 0, 0))
    grid_spec = pltpu.PrefetchScalarGridSpec(
        num_scalar_prefetch=5,
        grid=(n_items,),
        in_specs=[
            pl.BlockSpec(memory_space=pl.ANY),
            pl.BlockSpec(memory_space=pl.ANY),
            pl.BlockSpec(memory_space=pl.ANY),
            per_expert(two_de),
            per_expert(d),
        ],
        out_specs=pl.BlockSpec(memory_space=pl.ANY),
        scratch_shapes=[
            pltpu.VMEM((MOE_IO_SLOTS, gran * chunks, LANES), F32),
            pltpu.VMEM((MOE_ITEM_ROWS, d), BF16),
            pltpu.VMEM((MOE_ITEM_ROWS, d), F32),
            pltpu.VMEM((slots, d, tn), F32),
            pltpu.VMEM((slots, d, tn), F32),
            pltpu.VMEM((slots, tn, d), F32),
            pltpu.VMEM((d, tn), BF16),
            pltpu.VMEM((d, tn), BF16),
            pltpu.VMEM((tn, d), BF16),
            pltpu.SemaphoreType.DMA((MOE_IO_SLOTS,)),
            pltpu.SemaphoreType.DMA((slots,)),
        ],
    )
    return pl.pallas_call(
        functools.partial(_moe_kernel, chunks=chunks, n_col_steps=n_col_steps,
                          n_row_blocks=n_rows // gran),
        grid_spec=grid_spec,
        out_shape=jax.ShapeDtypeStruct((n_rows * chunks, LANES), F32),
        compiler_params=pltpu.CompilerParams(
            dimension_semantics=("arbitrary",),
            vmem_limit_bytes=56 * 1024 * 1024),
        name="moe",
    )(item_e, item_row0, item_nblk, used_blocks, live_items, xs, w_gate_up, w_down, bgu3, bd3)


def _combine_kernel(idx_ref, rank_ref, offs_ref, ys_hbm, x1_ref, gate_ref, fg_ref, o_ref, gbuf,
                    sem, *, tile, chunks, final_norm):
    i = pl.program_id(0)
    slot = i % 2
    per_tile = tile * TOP_K
    stride = _token_row_stride(chunks)

    def gather(step, dst_slot):
        base = step * per_tile

        def issue(g, carry):
            for u in range(DMA_ISSUE_UNROLL):
                t = g * DMA_ISSUE_UNROLL + u
                for k in range(TOP_K):
                    a = base + t * TOP_K + k
                    pltpu.make_async_copy(
                        _token_rows(ys_hbm, offs_ref[idx_ref[a]] + rank_ref[a], 1, chunks),
                        gbuf.at[dst_slot, pl.ds(pl.multiple_of((k * tile + t) * stride, SUBLANES),
                                                chunks), :],
                        sem.at[dst_slot]).start()
            return carry

        lax.fori_loop(0, tile // DMA_ISSUE_UNROLL, issue, 0)

    @pl.when(i == 0)
    def _():
        gather(0, 0)

    @pl.when(i + 1 < pl.num_programs(0))
    def _():
        gather(i + 1, 1 - slot)

    counted = gbuf.at[slot, pl.ds(0, per_tile * chunks), :]
    pltpu.make_async_copy(counted, counted, sem.at[slot]).wait()

    gates = gate_ref[...]
    gk = [gates[:, k:k + 1] for k in range(TOP_K)]
    cols = []
    for c in range(chunks):
        acc = x1_ref[:, c * LANES:(c + 1) * LANES]
        for k in range(TOP_K):
            acc = acc + gk[k] * gbuf[slot, pl.ds(k * tile * stride + c, tile, stride=stride), :]
        cols.append(acc)
    out = jnp.concatenate(cols, axis=-1)
    if final_norm:
        out = _rms(out, fg_ref[...])
    o_ref[...] = out


def _combine(idx_flat, rank_flat, offs, ys, x1, gates, final_g, final_norm, tile=128):
    n, d = x1.shape
    chunks = d // LANES
    steps = n // tile
    grid_spec = pltpu.PrefetchScalarGridSpec(
        num_scalar_prefetch=3,
        grid=(steps,),
        in_specs=[
            pl.BlockSpec(memory_space=pl.ANY),
            pl.BlockSpec((tile, d), lambda i, *_: (i, 0)),
            pl.BlockSpec((tile, LANES), lambda i, *_: (i, 0)),
            pl.BlockSpec((1, d), lambda i, *_: (0, 0)),
        ],
        out_specs=pl.BlockSpec((tile, d), lambda i, *_: (i, 0)),
        scratch_shapes=[pltpu.VMEM((2, TOP_K * tile * _token_row_stride(chunks), LANES), F32),
                        pltpu.SemaphoreType.DMA((2,))],
    )
    return pl.pallas_call(
        functools.partial(_combine_kernel, tile=tile, chunks=chunks, final_norm=final_norm),
        grid_spec=grid_spec,
        out_shape=jax.ShapeDtypeStruct((n, d), F32),
        compiler_params=pltpu.CompilerParams(dimension_semantics=("arbitrary",),
                                             vmem_limit_bytes=40 * 1024 * 1024),
        name="combine",
    )(idx_flat, rank_flat, offs, ys, x1, gates, final_g)


def _routing_tables(counts, n_rows_cap):
    n_experts = counts.shape[0]
    gran, item_rows = MOE_ROW_GRAN, MOE_ITEM_ROWS
    padded = ((counts + gran - 1) // gran) * gran
    ends = jnp.cumsum(padded)
    offs = ends - padded
    items_per = (padded + item_rows - 1) // item_rows
    item_ends = jnp.cumsum(items_per)
    item_starts = item_ends - items_per
    n_items = n_experts + n_rows_cap // item_rows
    slot = jnp.arange(n_items, dtype=jnp.int32)
    total = item_ends[-1]
    live = slot < total
    live_slot = jnp.minimum(slot, total - 1)
    e_of = jnp.minimum(jnp.sum((live_slot[:, None] >= item_ends[None, :]).astype(jnp.int32), axis=1),
                       n_experts - 1).astype(jnp.int32)
    local = slot - item_starts[e_of]
    row0 = offs[e_of] + local * item_rows
    rows = jnp.clip(padded[e_of] - local * item_rows, 0, item_rows)
    nblk = jnp.where(live, rows // gran, 0).astype(jnp.int32)
    row0 = jnp.where(live, row0, 0).astype(jnp.int32)
    used_blocks = (ends[-1:] // gran).astype(jnp.int32)
    pad_start = (offs + counts).astype(jnp.int32)
    pad_rows = (padded - counts).astype(jnp.int32)
    live_items = total.reshape(1).astype(jnp.int32)
    return (offs.astype(jnp.int32), e_of, row0, nblk, used_blocks, live_items, pad_start,
            pad_rows)


def _layer(x3, mix_g, w_in, conv_w, conv_g, attn_g, w_out, ffn_g, w_router, b_router,
           w_gate_up, b_gate_up, w_down, b_down, final_g, final_norm):
    b, seq, d = x3.shape
    n = b * seq
    cw_cols = conv_w.shape[1]
    aw_cols = attn_g.shape[0]
    n_experts = w_router.shape[1]
    chunks = d // LANES

    proj = _in_proj(x3.reshape(n, d), mix_g.reshape(1, d), w_in.astype(BF16))
    proj3 = proj.reshape(b, seq, -1)
    y_attn = _attention(proj3, cw_cols, aw_cols)
    x1, h2_tm, idx, gates, rank, counts = _out_proj(
        proj3, y_attn, x3, conv_w, conv_g.reshape(1, -1), attn_g.reshape(1, -1),
        w_out.astype(BF16), ffn_g.reshape(1, d), w_router, b_router.reshape(1, -1))

    n_rows = n * TOP_K + n_experts * MOE_ROW_GRAN
    (offs, item_e, item_row0, item_nblk, used_blocks, live_items, pad_start,
     pad_rows) = _routing_tables(counts[0], n_rows)
    idx_flat = idx[:, :TOP_K].reshape(n * TOP_K)
    rank_flat = rank[:, :TOP_K].reshape(n * TOP_K)
    xs = _dispatch(idx_flat, rank_flat, offs, pad_start, pad_rows, used_blocks, h2_tm, n_rows)
    ys = _moe(item_e, item_row0, item_nblk, used_blocks, live_items, xs, w_gate_up, b_gate_up,
              w_down, b_down, n_rows)
    out = _combine(idx_flat, rank_flat, offs, ys, x1.reshape(n, d), gates, final_g.reshape(1, d),
                   final_norm)
    return out.reshape(b, seq, d)


def kernel(x, mix_norm_g, w_in, conv_w, conv_norm_g, attn_norm_g, w_out, ffn_norm_g, w_router,
           b_router, w_gate_up, b_gate_up, w_down, b_down, final_norm_g):
    depth = w_in.shape[0]
    for layer in range(depth):
        x = _layer(x, mix_norm_g[layer], w_in[layer], conv_w[layer], conv_norm_g[layer],
                   attn_norm_g[layer], w_out[layer], ffn_norm_g[layer], w_router[layer],
                   b_router[layer], w_gate_up[layer], b_gate_up[layer], w_down[layer],
                   b_down[layer], final_norm_g, layer == depth - 1)
    return x
```

```python
import functools

import jax
import jax.numpy as jnp
from jax import lax
from jax.experimental import pallas as pl
from jax.experimental.pallas import tpu as pltpu

HEAD_DIM = 64
CONV_K = 3
ROT_DIM = HEAD_DIM // 4
ROPE_THETA = 500000.0
DILATIONS = (1, 4, 16)
ATTN_BLOCK = 128
TOP_K = 4
SWIGLU_LIMIT = 7.0
SWIGLU_ALPHA = 1.702
EPS = 1e-5

LANES = 128
SUBLANES = 8

MOE_ROW_GRAN = 256
MOE_ITEM_ROWS = 1536
MOE_COL_TILE = 256
MOE_TRIP_BLOCKS = 2
MOE_IO_SLOTS = 2 * MOE_TRIP_BLOCKS
BF16 = jnp.bfloat16
F32 = jnp.float32


def _token_row_stride(chunks):
    return chunks + SUBLANES


def _token_rows(ref, first_token, n_tokens, chunks):
    return ref.at[pl.ds(pl.multiple_of(first_token * chunks, chunks), n_tokens * chunks), :]


def _rms(x, g):
    return x * lax.rsqrt(jnp.mean(x * x, axis=-1, keepdims=True) + EPS) * g


def _in_proj_kernel(x_ref, g_ref, w_ref, o_ref, xn_ref):
    @pl.when(pl.program_id(1) == 0)
    def _():
        xn_ref[...] = _rms(x_ref[...], g_ref[...]).astype(BF16)

    o_ref[...] = jnp.dot(xn_ref[...], w_ref[...], preferred_element_type=F32)


def _in_proj(x2, g, w_bf16, tm=1024):
    n, d = x2.shape
    cols = w_bf16.shape[1]
    tn = 1024 if cols % 1024 == 0 else 512
    return pl.pallas_call(
        _in_proj_kernel,
        grid=(n // tm, cols // tn),
        in_specs=[
            pl.BlockSpec((tm, d), lambda i, j: (i, 0)),
            pl.BlockSpec((1, d), lambda i, j: (0, 0)),
            pl.BlockSpec((d, tn), lambda i, j: (0, j)),
        ],
        out_specs=pl.BlockSpec((tm, tn), lambda i, j: (i, j)),
        out_shape=jax.ShapeDtypeStruct((n, cols), F32),
        scratch_shapes=[pltpu.VMEM((tm, d), BF16)],
        compiler_params=pltpu.CompilerParams(
            dimension_semantics=("arbitrary", "arbitrary"),
            vmem_limit_bytes=48 * 1024 * 1024),
        name="in_proj",
    )(x2, g, w_bf16)


def _attn_kernel(q_ref, k_ref, v_ref, cos_ref, sa_ref, sb_ref, o_ref,
                 qn, kn, q4, q16, a1, d1, m1, a4, d4, m4, a16, d16, m16, acm, dcm, mcm,
                 k1, va1, vb1, k4, va4, vb4, k16, va16, vb16, bias_s, *, seq):
    nblk_total = seq // ATTN_BLOCK
    lane = lax.broadcasted_iota(jnp.int32, (1, LANES), 1)
    head0 = lane < HEAD_DIM

    qi = lax.broadcasted_iota(jnp.int32, (ATTN_BLOCK, 2 * ATTN_BLOCK), 0)
    kj = lax.broadcasted_iota(jnp.int32, (ATTN_BLOCK, 2 * ATTN_BLOCK), 1)
    band = (kj >= qi) & (kj <= qi + ATTN_BLOCK)
    neg = jnp.float32(-jnp.inf)
    bias_s[0] = jnp.where(band, 0.0, neg)
    bias_s[1] = jnp.where(band & (kj >= ATTN_BLOCK), 0.0, neg)

    def rot(x):
        return (x * cos_ref[...] + pltpu.roll(x, LANES - ROT_DIM // 2, 1) * sa_ref[...]
                + pltpu.roll(x, ROT_DIM // 2, 1) * sb_ref[...])

    qn[...] = rot(q_ref[...]) * (HEAD_DIM ** -0.5)
    kn[...] = rot(k_ref[...])

    def put_kv(dst_rows, kc, vac, vbc, k, v):
        kc[dst_rows, :] = k.astype(BF16)
        vac[dst_rows, :] = jnp.where(head0, v, 1.0).astype(BF16)
        vbc[dst_rows, :] = jnp.where(head0, 1.0, v).astype(BF16)

    put_kv(slice(None), k1, va1, vb1, kn[...], v_ref[...])
    for d, qc, kv in ((4, q4, (k4, va4, vb4)), (16, q16, (k16, va16, vb16))):
        ln = seq // d
        for r in range(d):
            rows = slice(r * ln, (r + 1) * ln)
            qc[rows, :] = qn[pl.ds(r, ln, stride=d), :]
            put_kv(rows, *kv, kn[pl.ds(r, ln, stride=d), :], v_ref[pl.ds(r, ln, stride=d), :])

    unroll = 16

    def run_branch(qsrc, kv, dsts, blocks_per_class):
        ksrc, vsrcs = kv[0], kv[1:]
        a_dst, d_dst, m_dst = dsts

        def attend(b, first):
            cur = pl.ds(pl.multiple_of(b * ATTN_BLOCK, ATTN_BLOCK), ATTN_BLOCK)
            qb = qsrc[cur, :]
            prv = pl.ds(pl.multiple_of(jnp.maximum(b - 1, 0) * ATTN_BLOCK, ATTN_BLOCK),
                        ATTN_BLOCK)
            bias = bias_s[int(first)] if isinstance(first, bool) else bias_s[first]
            kk = jnp.concatenate([ksrc[prv, :], ksrc[cur, :]], axis=0)
            res, mx = [], []
            for h in range(2):
                mine = head0 if h == 0 else jnp.logical_not(head0)
                qh = jnp.where(mine, qb, 0.0).astype(BF16)
                s = lax.dot_general(qh, kk, (((1,), (1,)), ((), ())), preferred_element_type=F32)
                s = s + bias
                m = jnp.max(s, axis=-1, keepdims=True)
                p = jnp.exp(s - m).astype(BF16)
                vh = jnp.concatenate([vsrcs[h][prv, :], vsrcs[h][cur, :]], axis=0)
                res.append(jnp.dot(p, vh, preferred_element_type=F32))
                mx.append(m)
            a_dst[cur, :] = jnp.where(head0, res[0], res[1])
            d_dst[cur, :] = pltpu.roll(jnp.where(head0, res[1], res[0]), HEAD_DIM, 1)
            m_dst[cur, :] = jnp.where(head0, mx[0], mx[1])

        def body(i, carry):
            for u in range(unroll):
                b = i * unroll + u
                if unroll % blocks_per_class == 0:
                    first = u % blocks_per_class == 0
                elif blocks_per_class % unroll == 0 and u != 0:
                    first = False
                else:
                    first = jnp.where(jnp.asarray(b % blocks_per_class == 0), 1, 0)
                attend(b, first)
            return carry

        lax.fori_loop(0, nblk_total // unroll, body, 0)

    res1, res4, res16, res_cm = (a1, d1, m1), (a4, d4, m4), (a16, d16, m16), (acm, dcm, mcm)
    run_branch(qn, (k1, va1, vb1), res1, nblk_total)
    for d, qc, kv, res in ((4, q4, (k4, va4, vb4), res4), (16, q16, (k16, va16, vb16), res16)):
        run_branch(qc, kv, res_cm, nblk_total // d)
        ln = seq // d
        for cm, nat in zip(res_cm, res):
            for r in range(d):
                nat[pl.ds(r, ln, stride=d), :] = cm[r * ln:(r + 1) * ln, :]

    ma, mb, mc = m1[...], m4[...], m16[...]
    mx = jnp.maximum(jnp.maximum(ma, mb), mc)
    ea, eb, ec = jnp.exp(ma - mx), jnp.exp(mb - mx), jnp.exp(mc - mx)
    o_ref[...] = ((ea * a1[...] + eb * a4[...] + ec * a16[...])
                  / (ea * d1[...] + eb * d4[...] + ec * d16[...]))


def _rotary_tables(seq):
    half = ROT_DIM // 2
    inv_freq = ROPE_THETA ** (-jnp.arange(0, ROT_DIM, 2, dtype=F32) / ROT_DIM)
    ang = jnp.arange(seq, dtype=F32)[:, None] * inv_freq[None, :]
    cos, sin = jnp.cos(ang), jnp.sin(ang)
    pos = jnp.arange(LANES) % HEAD_DIM
    fidx = pos % half
    in_lo = pos < half
    in_hi = (pos >= half) & (pos < ROT_DIM)
    cos_t = jnp.where((in_lo | in_hi)[None, :], cos[:, fidx], 1.0)
    sa = jnp.where(in_lo[None, :], -sin[:, fidx], 0.0)
    sb = jnp.where(in_hi[None, :], sin[:, fidx], 0.0)
    return cos_t.astype(F32), sa.astype(F32), sb.astype(F32)


def _attention(proj3, conv_w_cols, attn_w_cols):
    b, seq, _ = proj3.shape
    assert seq % (ATTN_BLOCK * DILATIONS[-1]) == 0
    n_pairs = attn_w_cols // LANES
    qoff = 3 * conv_w_cols // LANES
    cos_t, sa, sb = _rotary_tables(seq)
    blk = lambda off: pl.BlockSpec((None, seq, LANES), lambda i, j: (i, 0, off + j))
    tab = pl.BlockSpec((seq, LANES), lambda i, j: (0, 0))
    big = pltpu.VMEM((seq, LANES), F32)
    return pl.pallas_call(
        functools.partial(_attn_kernel, seq=seq),
        grid=(b, n_pairs),
        in_specs=[blk(qoff), blk(qoff + n_pairs), blk(qoff + 2 * n_pairs), tab, tab, tab],
        out_specs=pl.BlockSpec((None, seq, LANES), lambda i, j: (i, 0, j)),
        out_shape=jax.ShapeDtypeStruct((b, seq, attn_w_cols), F32),
        scratch_shapes=([big] * 16 + [pltpu.VMEM((seq, LANES), BF16)] * 9
                        + [pltpu.VMEM((2, ATTN_BLOCK, 2 * ATTN_BLOCK), F32)]),
        compiler_params=pltpu.CompilerParams(
            dimension_semantics=("arbitrary", "arbitrary"),
            vmem_limit_bytes=48 * 1024 * 1024),
        name="attention",
    )(proj3, proj3, proj3, cos_t, sa, sb)


def _out_proj_kernel(cx_ref, cb_ref, cc_ref, hx_ref, hc_ref, ya_ref, x_ref,
                     cw_ref, cg_ref, ag_ref, wo_ref, fg_ref, wrh_ref, wrl_ref, br_ref,
                     x1_ref, h2_ref, idx_ref, gate_ref, rank_ref, cnt_ref, carry,
                     *, tile, n_experts):
    first_tile_of_seq = pl.program_id(1) == 0
    first_step = (pl.program_id(0) == 0) & first_tile_of_seq

    @pl.when(first_step)
    def _():
        carry[...] = jnp.zeros_like(carry)

    u = cc_ref[...] * cx_ref[...]
    uh = jnp.where(first_tile_of_seq, 0.0, hc_ref[...] * hx_ref[...])
    row8 = lax.broadcasted_iota(jnp.int32, (SUBLANES, 1), 0)

    def shifted(k):
        r = pltpu.roll(u, k, 0)
        top = jnp.where(row8 < k, pltpu.roll(uh, k, 0), r[:SUBLANES])
        return jnp.concatenate([top, r[SUBLANES:]], axis=0)

    conv = cw_ref[2:3, :] * u + cw_ref[1:2, :] * shifted(1) + cw_ref[0:1, :] * shifted(2)
    y_conv = cb_ref[...] * conv

    mixed = jnp.concatenate(
        [_rms(y_conv, cg_ref[...]), _rms(ya_ref[...], ag_ref[...])], axis=-1).astype(BF16)
    x1 = x_ref[...] + jnp.dot(mixed, wo_ref[...], preferred_element_type=F32)
    x1_ref[...] = x1
    h2 = _rms(x1, fg_ref[...])
    d = h2.shape[-1]
    chunks = d // LANES
    for c in range(chunks):
        h2_ref[pl.ds(c, tile, stride=chunks), :] = h2[:, c * LANES:(c + 1) * LANES]

    h2_hi = h2.astype(BF16)
    h2_lo = (h2 - h2_hi.astype(F32)).astype(BF16)
    logits = (jnp.dot(h2_hi, wrh_ref[...], preferred_element_type=F32)
              + jnp.dot(h2_hi, wrl_ref[...], preferred_element_type=F32)
              + jnp.dot(h2_lo, wrh_ref[...], preferred_element_type=F32)) + br_ref[...]
    eio = lax.broadcasted_iota(jnp.int32, (tile, n_experts), 1).astype(F32)
    work = logits
    vals, idxs = [], []
    for _ in range(TOP_K):
        m = jnp.max(work, axis=1, keepdims=True)
        ik = jnp.min(jnp.where(work == m, eio, float(n_experts)), axis=1, keepdims=True)
        vals.append(m)
        idxs.append(ik)
        work = jnp.where(eio == ik, -jnp.inf, work)
    exps = [jnp.exp(v - vals[0]) for v in vals]
    tot = exps[0] + exps[1] + exps[2] + exps[3]

    onehot = jnp.zeros((tile, n_experts), F32)
    for ik in idxs:
        onehot = onehot + (eio == ik).astype(F32)
    ri = lax.broadcasted_iota(jnp.int32, (tile, tile), 0)
    ci = lax.broadcasted_iota(jnp.int32, (tile, tile), 1)
    tri = (ci < ri).astype(BF16)
    before = jnp.dot(tri, onehot.astype(BF16), preferred_element_type=F32) + carry[...]
    carry[...] = carry[...] + jnp.sum(onehot, axis=0, keepdims=True)
    cnt_ref[...] = carry[...].astype(jnp.int32)

    lio = lax.broadcasted_iota(jnp.int32, (tile, LANES), 1)
    idx_out = jnp.zeros((tile, LANES), jnp.int32)
    gate_out = jnp.zeros((tile, LANES), F32)
    rank_out = jnp.zeros((tile, LANES), jnp.int32)
    for k in range(TOP_K):
        rk = jnp.sum(jnp.where(eio == idxs[k], before, 0.0), axis=1, keepdims=True)
        idx_out = jnp.where(lio == k, idxs[k].astype(jnp.int32), idx_out)
        gate_out = jnp.where(lio == k, exps[k] / tot, gate_out)
        rank_out = jnp.where(lio == k, rk.astype(jnp.int32), rank_out)
    idx_ref[...] = idx_out
    gate_ref[...] = gate_out
    rank_ref[...] = rank_out


def _out_proj(proj3, y_attn, x3, conv_w, conv_g, attn_g, w_out_bf16, ffn_g, w_router, b_router,
              tile=256):
    b, seq, d = x3.shape
    cw_cols = conv_w.shape[1]
    aw_cols = y_attn.shape[2]
    n_experts = w_router.shape[1]
    w_router_hi = w_router.astype(BF16)
    w_router_lo = (w_router - w_router_hi.astype(F32)).astype(BF16)
    chunks = d // LANES
    tiles = seq // tile
    n = b * seq
    halo_blocks = tile // SUBLANES
    row_blk = lambda width, col: pl.BlockSpec((None, tile, width), lambda i, j: (i, j, col))
    halo = lambda col: pl.BlockSpec(
        (None, SUBLANES, cw_cols), lambda i, j: (i, jnp.maximum(j * halo_blocks - 1, 0), col))
    const = lambda shape: pl.BlockSpec(shape, lambda i, j: (0,) * len(shape))
    flat = lambda width: pl.BlockSpec((tile, width), lambda i, j: (i * tiles + j, 0))
    outs = pl.pallas_call(
        functools.partial(_out_proj_kernel, tile=tile, n_experts=n_experts),
        grid=(b, tiles),
        in_specs=[
            row_blk(cw_cols, 0), row_blk(cw_cols, 1), row_blk(cw_cols, 2), halo(0), halo(2),
            row_blk(aw_cols, 0), row_blk(d, 0),
            const((CONV_K, cw_cols)), const((1, cw_cols)), const((1, aw_cols)),
            const((d, d)), const((1, d)), const((d, n_experts)), const((d, n_experts)),
            const((1, n_experts)),
        ],
        out_specs=[
            row_blk(d, 0),
            pl.BlockSpec((tile * chunks, LANES), lambda i, j: (i * tiles + j, 0)),
            flat(LANES), flat(LANES), flat(LANES),
            const((1, n_experts)),
        ],
        out_shape=[
            jax.ShapeDtypeStruct((b, seq, d), F32),
            jax.ShapeDtypeStruct((n * chunks, LANES), F32),
            jax.ShapeDtypeStruct((n, LANES), jnp.int32),
            jax.ShapeDtypeStruct((n, LANES), F32),
            jax.ShapeDtypeStruct((n, LANES), jnp.int32),
            jax.ShapeDtypeStruct((1, n_experts), jnp.int32),
        ],
        scratch_shapes=[pltpu.VMEM((1, n_experts), F32)],
        compiler_params=pltpu.CompilerParams(
            dimension_semantics=("arbitrary", "arbitrary"),
            vmem_limit_bytes=48 * 1024 * 1024),
        name="out_proj",
    )(proj3, proj3, proj3, proj3, proj3, y_attn, x3, conv_w, conv_g, attn_g, w_out_bf16,
      ffn_g, w_router_hi, w_router_lo, b_router)
    return outs


DMA_ISSUE_UNROLL = 4


def _dispatch_kernel(idx_ref, rank_ref, offs_ref, pad_start, pad_rows, used_blocks, h2_ref, xs_hbm,
                     dest_ref, zbuf, sem, zsem, *, tile, chunks, n_experts, n_row_blocks):
    gran = MOE_ROW_GRAN

    @pl.when(pl.program_id(0) == 0)
    def _():
        zbuf[...] = jnp.zeros_like(zbuf)

        def zero_copy(first_row, n_rows):
            return pltpu.make_async_copy(_token_rows(zbuf, 0, n_rows, chunks),
                                         _token_rows(xs_hbm, first_row, n_rows, chunks), zsem)

        def for_each_zero_run(act):
            def per_expert(e, carry):
                row = pad_start[e]
                left = pad_rows[e]
                for bit in reversed(range(gran.bit_length() - 1)):
                    take = (left >> bit) & 1

                    @pl.when(take == 1)
                    def _():
                        act(zero_copy(row, 1 << bit))

                    row = row + (take << bit)
                return carry

            lax.fori_loop(0, n_experts, per_expert, 0)

            def tail(blk, carry):
                act(zero_copy(blk * gran, gran))
                return carry

            lax.fori_loop(used_blocks[0], n_row_blocks, tail, 0)

        for_each_zero_run(lambda cp: cp.start())
        for_each_zero_run(lambda cp: cp.wait())

    def issue(g, carry):
        for u in range(DMA_ISSUE_UNROLL):
            t = g * DMA_ISSUE_UNROLL + u
            for k in range(TOP_K):
                a = t * TOP_K + k
                dest = offs_ref[idx_ref[0, 0, a]] + rank_ref[0, 0, a]
                dest_ref[0, 0, a] = dest
                pltpu.make_async_copy(_token_rows(h2_ref, t, 1, chunks),
                                      _token_rows(xs_hbm, dest, 1, chunks), sem).start()
        return carry

    lax.fori_loop(0, tile // DMA_ISSUE_UNROLL, issue, 0)
    counted = _token_rows(xs_hbm, 0, tile * TOP_K, chunks)
    pltpu.make_async_copy(counted, counted, sem).wait()


def _dispatch(idx_flat, rank_flat, offs, pad_start, pad_rows, used_blocks, h2_tm, n_rows, tile=256):
    steps = idx_flat.shape[0] // (tile * TOP_K)
    per_tile = lambda a: a.reshape(steps, 1, tile * TOP_K)
    chunks = h2_tm.shape[0] // (idx_flat.shape[0] // TOP_K)
    gran = MOE_ROW_GRAN
    smem = pl.BlockSpec(memory_space=pltpu.SMEM)
    tile_smem = pl.BlockSpec((1, 1, tile * TOP_K), lambda i: (i, 0, 0), memory_space=pltpu.SMEM)
    xs, dest = pl.pallas_call(
        functools.partial(_dispatch_kernel, tile=tile, chunks=chunks,
                          n_experts=pad_start.shape[0], n_row_blocks=n_rows // gran),
        grid=(steps,),
        in_specs=[
            tile_smem, tile_smem, smem, smem, smem, smem,
            pl.BlockSpec((tile * chunks, LANES), lambda i: (i, 0)),
        ],
        out_specs=[pl.BlockSpec(memory_space=pl.ANY), tile_smem],
        out_shape=[jax.ShapeDtypeStruct((n_rows * chunks, LANES), F32),
                   jax.ShapeDtypeStruct((steps, 1, tile * TOP_K), jnp.int32)],
        scratch_shapes=[pltpu.VMEM((gran * chunks, LANES), F32),
                        pltpu.SemaphoreType.DMA(()), pltpu.SemaphoreType.DMA(())],
        compiler_params=pltpu.CompilerParams(dimension_semantics=("arbitrary",)),
        name="dispatch",
    )(per_tile(idx_flat), per_tile(rank_flat), offs, pad_start, pad_rows, used_blocks, h2_tm)
    return xs, dest.reshape(-1)


MOE_WEIGHT_SLOTS = 3
MOE_WEIGHT_SPLIT = 4


def _moe_kernel(item_e, item_row0, item_nblk, used_blocks, live_items, xs_hbm, wgu_hbm, wdn_hbm,
                bgu_ref, bd_ref, ys_hbm, iobuf, xb, yacc, wgf, wuf, wdf, wgb, wub, wdb,
                sem_io, sem_w, *, chunks, n_col_steps, n_row_blocks):
    s = pl.program_id(0)
    nb = item_nblk[s]
    row0 = item_row0[s]
    gran = MOE_ROW_GRAN
    tn = MOE_COL_TILE
    de = n_col_steps * tn
    total_chunks = live_items[0] * n_col_steps

    def staged(slot):
        return iobuf.at[slot]

    def ys_block(blk):
        return _token_rows(ys_hbm, blk * gran, gran, chunks)

    def weight_copies(g):
        e = item_e[g // n_col_steps]
        col = pl.multiple_of((g % n_col_steps) * tn, tn)
        slot = g % MOE_WEIGHT_SLOTS
        d = wgf.shape[1]
        copies = []
        for part in range(MOE_WEIGHT_SPLIT):
            rows = pl.ds(part * (d // MOE_WEIGHT_SPLIT), d // MOE_WEIGHT_SPLIT)
            copies.append(pltpu.make_async_copy(
                wgu_hbm.at[e, rows, pl.ds(col, tn)], wgf.at[slot, rows, :], sem_w.at[slot]))
            copies.append(pltpu.make_async_copy(
                wgu_hbm.at[e, rows, pl.ds(pl.multiple_of(de + col, tn), tn)],
                wuf.at[slot, rows, :], sem_w.at[slot]))
            sub = tn // MOE_WEIGHT_SPLIT
            copies.append(pltpu.make_async_copy(
                wdn_hbm.at[e, pl.ds(pl.multiple_of(col + part * sub, sub), sub), :],
                wdf.at[slot, pl.ds(part * sub, sub), :], sem_w.at[slot]))
        return copies

    def fetch(g):
        @pl.when(g < total_chunks)
        def _():
            for cp in weight_copies(g):
                cp.start()

    @pl.when(s == 0)
    def _():
        for g in range(MOE_WEIGHT_SLOTS - 1):
            fetch(g)

    @pl.when(nb > 0)
    def _():
        def io_slot(sb):
            return sb % MOE_IO_SLOTS

        def x_copy(sb):
            return pltpu.make_async_copy(
                _token_rows(xs_hbm, row0 + sb * gran, gran, chunks), staged(io_slot(sb)),
                sem_io.at[io_slot(sb)])

        def y_copy(sb):
            return pltpu.make_async_copy(staged(io_slot(sb)), ys_block(row0 // gran + sb),
                                         sem_io.at[io_slot(sb)])

        for sb in range(MOE_IO_SLOTS):
            @pl.when(sb < nb)
            def _():
                x_copy(sb).start()

        def block_rows(sb):
            return pl.ds(pl.multiple_of(sb * gran, gran), gran)

        def column_step(j, phase):
            g = s * n_col_steps + j
            slot = g % MOE_WEIGHT_SLOTS
            for cp in weight_copies(g):
                cp.wait()
            fetch(g + MOE_WEIGHT_SLOTS - 1)
            col = pl.multiple_of(j * tn, tn)
            bg = bgu_ref[:, pl.ds(col, tn)]
            bu = bgu_ref[:, pl.ds(pl.multiple_of(de + col, tn), tn)]

            def cast_weights():
                wgb[...] = wgf[slot].astype(BF16)
                wub[...] = wuf[slot].astype(BF16)
                wdb[...] = wdf[slot].astype(BF16)

            def partial_out(sb):
                x = xb[block_rows(sb), :]
                gate = jnp.dot(x, wgb[...], preferred_element_type=F32) + bg
                up = jnp.dot(x, wub[...], preferred_element_type=F32) + bu
                gate = jnp.minimum(gate, SWIGLU_LIMIT)
                up = jnp.clip(up, -SWIGLU_LIMIT, SWIGLU_LIMIT)
                act = (up + 1.0) * (gate * jax.nn.sigmoid(SWIGLU_ALPHA * gate))
                return jnp.dot(act.astype(BF16), wdb[...], preferred_element_type=F32)

            def trip(first, count):
                blocks = [first + u for u in range(count)]
                if phase == "first":
                    for sb in blocks:
                        x_copy(sb).wait()
                    for sb in blocks:
                        for c in range(chunks):
                            xb[block_rows(sb), c * LANES:(c + 1) * LANES] = (
                                iobuf[io_slot(sb), pl.ds(c, gran, stride=chunks), :].astype(BF16))
                if phase == "last":
                    for sb in blocks:
                        @pl.when(sb >= MOE_IO_SLOTS)
                        def _():
                            y_copy(sb - MOE_IO_SLOTS).wait()
                parts = [partial_out(sb) for sb in blocks]
                for sb, part in zip(blocks, parts):
                    if phase == "first":
                        yacc[block_rows(sb), :] = part + bd_ref[...]
                    elif phase == "middle":
                        yacc[block_rows(sb), :] += part
                    else:
                        y = yacc[block_rows(sb), :] + part
                        for c in range(chunks):
                            iobuf[io_slot(sb), pl.ds(c, gran, stride=chunks), :] = (
                                y[:, c * LANES:(c + 1) * LANES])
                for sb in blocks:
                    if phase == "first":
                        @pl.when(sb + MOE_IO_SLOTS < nb)
                        def _():
                            x_copy(sb + MOE_IO_SLOTS).start()
                    if phase == "last":
                        y_copy(sb).start()

            full = MOE_TRIP_BLOCKS

            def tail(left):
                count = full // 2
                while count >= 1:
                    @pl.when(left % (2 * count) >= count)
                    def _():
                        trip(nb - left % (2 * count), count)

                    count //= 2

            @pl.when(nb >= full)
            def _():
                cast_weights()
                trip(0, full)

                def body(i, c):
                    trip(i * full, full)
                    return c

                lax.fori_loop(1, nb // full, body, 0)
                tail(nb % full)

            @pl.when(nb < full)
            def _():
                cast_weights()
                tail(nb)

        column_step(0, "first")

        def middle(j, carry):
            column_step(j, "middle")
            return carry

        lax.fori_loop(1, n_col_steps - 1, middle, 0)
        column_step(n_col_steps - 1, "last")

        for k in range(MOE_IO_SLOTS):
            @pl.when(nb > k)
            def _():
                y_copy(nb - 1 - k).wait()

    @pl.when(s == pl.num_programs(0) - 1)
    def _():
        iobuf[0] = jnp.zeros(iobuf.shape[1:], iobuf.dtype)

        def fill(blk, carry):
            cp = pltpu.make_async_copy(staged(0), ys_block(blk), sem_io.at[0])
            cp.start()
            cp.wait()
            return carry

        lax.fori_loop(used_blocks[0], n_row_blocks, fill, 0)


def _moe(item_e, item_row0, item_nblk, used_blocks, live_items, xs, w_gate_up, b_gate_up, w_down,
         b_down, n_rows):
    n_experts, d, two_de = w_gate_up.shape
    de = two_de // 2
    chunks = d // LANES
    tn = MOE_COL_TILE
    n_col_steps = de // tn
    assert n_col_steps >= 2
    n_items = item_e.shape[0]
    gran = MOE_ROW_GRAN
    slots = MOE_WEIGHT_SLOTS

    bgu3 = b_gate_up.reshape(n_experts, 1, two_de)
    bd3 = b_down.reshape(n_experts, 1, d)
    per_expert = lambda width: pl.BlockSpec((None, 1, width), lambda s, e, *_: (e[s], 0, 0))
    grid_spec = pltpu.PrefetchScalarGridSpec(
        num_scalar_prefetch=5,
        grid=(n_items,),
        in_specs=[
            pl.BlockSpec(memory_space=pl.ANY),
            pl.BlockSpec(memory_space=pl.ANY),
            pl.BlockSpec(memory_space=pl.ANY),
            per_expert(two_de),
            per_expert(d),
        ],
        out_specs=pl.BlockSpec(memory_space=pl.ANY),
        scratch_shapes=[
            pltpu.VMEM((MOE_IO_SLOTS, gran * chunks, LANES), F32),
            pltpu.VMEM((MOE_ITEM_ROWS, d), BF16),
            pltpu.VMEM((MOE_ITEM_ROWS, d), F32),
            pltpu.VMEM((slots, d, tn), F32),
            pltpu.VMEM((slots, d, tn), F32),
            pltpu.VMEM((slots, tn, d), F32),
            pltpu.VMEM((d, tn), BF16),
            pltpu.VMEM((d, tn), BF16),
            pltpu.VMEM((tn, d), BF16),
            pltpu.SemaphoreType.DMA((MOE_IO_SLOTS,)),
            pltpu.SemaphoreType.DMA((slots,)),
        ],
    )
    return pl.pallas_call(
        functools.partial(_moe_kernel, chunks=chunks, n_col_steps=n_col_steps,
                          n_row_blocks=n_rows // gran),
        grid_spec=grid_spec,
        out_shape=jax.ShapeDtypeStruct((n_rows * chunks, LANES), F32),
        compiler_params=pltpu.CompilerParams(
            dimension_semantics=("arbitrary",),
            vmem_limit_bytes=56 * 1024 * 1024),
        name="moe",
    )(item_e, item_row0, item_nblk, used_blocks, live_items, xs, w_gate_up, w_down, bgu3, bd3)


def _combine_kernel(dest_ref, ys_hbm, x1_ref, gate_ref, fg_ref, o_ref, gbuf, sem,
                    *, tile, chunks, final_norm):
    i = pl.program_id(0)
    slot = i % 2
    per_tile = tile * TOP_K
    stride = _token_row_stride(chunks)

    def gather(step, dst_slot):
        base = step * per_tile

        def issue(g, carry):
            for u in range(DMA_ISSUE_UNROLL):
                t = g * DMA_ISSUE_UNROLL + u
                for k in range(TOP_K):
                    pltpu.make_async_copy(
                        _token_rows(ys_hbm, dest_ref[base + t * TOP_K + k], 1, chunks),
                        gbuf.at[dst_slot, pl.ds(pl.multiple_of((k * tile + t) * stride, SUBLANES),
                                                chunks), :],
                        sem.at[dst_slot]).start()
            return carry

        lax.fori_loop(0, tile // DMA_ISSUE_UNROLL, issue, 0)

    @pl.when(i == 0)
    def _():
        gather(0, 0)

    @pl.when(i + 1 < pl.num_programs(0))
    def _():
        gather(i + 1, 1 - slot)

    counted = gbuf.at[slot, pl.ds(0, per_tile * chunks), :]
    pltpu.make_async_copy(counted, counted, sem.at[slot]).wait()

    gates = gate_ref[...]
    gk = [gates[:, k:k + 1] for k in range(TOP_K)]
    cols = []
    for c in range(chunks):
        acc = x1_ref[:, c * LANES:(c + 1) * LANES]
        for k in range(TOP_K):
            acc = acc + gk[k] * gbuf[slot, pl.ds(k * tile * stride + c, tile, stride=stride), :]
        cols.append(acc)
    out = jnp.concatenate(cols, axis=-1)
    if final_norm:
        out = _rms(out, fg_ref[...])
    o_ref[...] = out


def _combine(dest, ys, x1, gates, final_g, final_norm, tile=128):
    n, d = x1.shape
    chunks = d // LANES
    steps = n // tile
    grid_spec = pltpu.PrefetchScalarGridSpec(
        num_scalar_prefetch=1,
        grid=(steps,),
        in_specs=[
            pl.BlockSpec(memory_space=pl.ANY),
            pl.BlockSpec((tile, d), lambda i, *_: (i, 0)),
            pl.BlockSpec((tile, LANES), lambda i, *_: (i, 0)),
            pl.BlockSpec((1, d), lambda i, *_: (0, 0)),
        ],
        out_specs=pl.BlockSpec((tile, d), lambda i, *_: (i, 0)),
        scratch_shapes=[pltpu.VMEM((2, TOP_K * tile * _token_row_stride(chunks), LANES), F32),
                        pltpu.SemaphoreType.DMA((2,))],
    )
    return pl.pallas_call(
        functools.partial(_combine_kernel, tile=tile, chunks=chunks, final_norm=final_norm),
        grid_spec=grid_spec,
        out_shape=jax.ShapeDtypeStruct((n, d), F32),
        compiler_params=pltpu.CompilerParams(dimension_semantics=("arbitrary",),
                                             vmem_limit_bytes=40 * 1024 * 1024),
        name="combine",
    )(dest, ys, x1, gates, final_g)


def _routing_tables(counts, n_rows_cap):
    n_experts = counts.shape[0]
    gran, item_rows = MOE_ROW_GRAN, MOE_ITEM_ROWS
    padded = ((counts + gran - 1) // gran) * gran
    ends = jnp.cumsum(padded)
    offs = ends - padded
    items_per = (padded + item_rows - 1) // item_rows
    item_ends = jnp.cumsum(items_per)
    item_starts = item_ends - items_per
    n_items = n_experts + n_rows_cap // item_rows
    slot = jnp.arange(n_items, dtype=jnp.int32)
    total = item_ends[-1]
    live = slot < total
    live_slot = jnp.minimum(slot, total - 1)
    e_of = jnp.minimum(jnp.sum((live_slot[:, None] >= item_ends[None, :]).astype(jnp.int32), axis=1),
                       n_experts - 1).astype(jnp.int32)
    local = slot - item_starts[e_of]
    row0 = offs[e_of] + local * item_rows
    rows = jnp.clip(padded[e_of] - local * item_rows, 0, item_rows)
    nblk = jnp.where(live, rows // gran, 0).astype(jnp.int32)
    row0 = jnp.where(live, row0, 0).astype(jnp.int32)
    used_blocks = (ends[-1:] // gran).astype(jnp.int32)
    pad_start = (offs + counts).astype(jnp.int32)
    pad_rows = (padded - counts).astype(jnp.int32)
    live_items = total.reshape(1).astype(jnp.int32)
    return (offs.astype(jnp.int32), e_of, row0, nblk, used_blocks, live_items, pad_start,
            pad_rows)


def _layer(x3, mix_g, w_in, conv_w, conv_g, attn_g, w_out, ffn_g, w_router, b_router,
           w_gate_up, b_gate_up, w_down, b_down, final_g, final_norm):
    b, seq, d = x3.shape
    n = b * seq
    cw_cols = conv_w.shape[1]
    aw_cols = attn_g.shape[0]
    n_experts = w_router.shape[1]
    chunks = d // LANES

    proj = _in_proj(x3.reshape(n, d), mix_g.reshape(1, d), w_in.astype(BF16))
    proj3 = proj.reshape(b, seq, -1)
    y_attn = _attention(proj3, cw_cols, aw_cols)
    x1, h2_tm, idx, gates, rank, counts = _out_proj(
        proj3, y_attn, x3, conv_w, conv_g.reshape(1, -1), attn_g.reshape(1, -1),
        w_out.astype(BF16), ffn_g.reshape(1, d), w_router, b_router.reshape(1, -1))

    n_rows = n * TOP_K + n_experts * MOE_ROW_GRAN
    (offs, item_e, item_row0, item_nblk, used_blocks, live_items, pad_start,
     pad_rows) = _routing_tables(counts[0], n_rows)
    idx_flat = idx[:, :TOP_K].reshape(n * TOP_K)
    rank_flat = rank[:, :TOP_K].reshape(n * TOP_K)
    xs, dest = _dispatch(idx_flat, rank_flat, offs, pad_start, pad_rows, used_blocks, h2_tm,
                         n_rows)
    ys = _moe(item_e, item_row0, item_nblk, used_blocks, live_items, xs, w_gate_up, b_gate_up,
              w_down, b_down, n_rows)
    out = _combine(dest, ys, x1.reshape(n, d), gates, final_g.reshape(1, d), final_norm)
    return out.reshape(b, seq, d)


def kernel(x, mix_norm_g, w_in, conv_w, conv_norm_g, attn_norm_g, w_out, ffn_norm_g, w_router,
           b_router, w_gate_up, b_gate_up, w_down, b_down, final_norm_g):
    depth = w_in.shape[0]
    for layer in range(depth):
        x = _layer(x, mix_norm_g[layer], w_in[layer], conv_w[layer], conv_norm_g[layer],
                   attn_norm_g[layer], w_out[layer], ffn_norm_g[layer], w_router[layer],
                   b_router[layer], w_gate_up[layer], b_gate_up[layer], w_down[layer],
                   b_down[layer], final_norm_g, layer == depth - 1)
    return x
```

```python
import functools

import jax
import jax.numpy as jnp
from jax import lax
from jax.experimental import pallas as pl
from jax.experimental.pallas import tpu as pltpu

HEAD_DIM = 64
CONV_K = 3
ROT_DIM = HEAD_DIM // 4
ROPE_THETA = 500000.0
DILATIONS = (1, 4, 16)
ATTN_BLOCK = 128
TOP_K = 4
SWIGLU_LIMIT = 7.0
SWIGLU_ALPHA = 1.702
EPS = 1e-5

LANES = 128
SUBLANES = 8

MOE_ROW_GRAN = 256
MOE_ITEM_ROWS = 1536
MOE_COL_TILE = 256
MOE_TRIP_BLOCKS = 2
MOE_IO_SLOTS = 2 * MOE_TRIP_BLOCKS
BF16 = jnp.bfloat16
F32 = jnp.float32


def _token_row_stride(chunks):
    return chunks + SUBLANES


def _token_rows(ref, first_token, n_tokens, chunks):
    return ref.at[pl.ds(pl.multiple_of(first_token * chunks, chunks), n_tokens * chunks), :]


def _rms(x, g):
    return x * lax.rsqrt(jnp.mean(x * x, axis=-1, keepdims=True) + EPS) * g


def _in_proj_kernel(x_ref, g_ref, w_ref, o_ref, xn_ref):
    @pl.when(pl.program_id(1) == 0)
    def _():
        xn_ref[...] = _rms(x_ref[...], g_ref[...]).astype(BF16)

    o_ref[...] = jnp.dot(xn_ref[...], w_ref[...], preferred_element_type=F32)


def _in_proj(x2, g, w_bf16, tm=1024):
    n, d = x2.shape
    cols = w_bf16.shape[1]
    tn = 1024 if cols % 1024 == 0 else 512
    return pl.pallas_call(
        _in_proj_kernel,
        grid=(n // tm, cols // tn),
        in_specs=[
            pl.BlockSpec((tm, d), lambda i, j: (i, 0)),
            pl.BlockSpec((1, d), lambda i, j: (0, 0)),
            pl.BlockSpec((d, tn), lambda i, j: (0, j)),
        ],
        out_specs=pl.BlockSpec((tm, tn), lambda i, j: (i, j)),
        out_shape=jax.ShapeDtypeStruct((n, cols), F32),
        scratch_shapes=[pltpu.VMEM((tm, d), BF16)],
        compiler_params=pltpu.CompilerParams(
            dimension_semantics=("arbitrary", "arbitrary"),
            vmem_limit_bytes=48 * 1024 * 1024),
        name="in_proj",
    )(x2, g, w_bf16)


def _attn_kernel(q_ref, k_ref, v_ref, cos_ref, sa_ref, sb_ref, o_ref,
                 qn, kn, q4, q16, a1, d1, m1, a4, d4, m4, a16, d16, m16, acm, dcm, mcm,
                 k1, va1, vb1, k4, va4, vb4, k16, va16, vb16, bias_s, *, seq):
    nblk_total = seq // ATTN_BLOCK
    lane = lax.broadcasted_iota(jnp.int32, (1, LANES), 1)
    head0 = lane < HEAD_DIM

    qi = lax.broadcasted_iota(jnp.int32, (ATTN_BLOCK, 2 * ATTN_BLOCK), 0)
    kj = lax.broadcasted_iota(jnp.int32, (ATTN_BLOCK, 2 * ATTN_BLOCK), 1)
    band = (kj >= qi) & (kj <= qi + ATTN_BLOCK)
    neg = jnp.float32(-jnp.inf)
    bias_s[0] = jnp.where(band, 0.0, neg)
    bias_s[1] = jnp.where(band & (kj >= ATTN_BLOCK), 0.0, neg)

    def rot(x):
        return (x * cos_ref[...] + pltpu.roll(x, LANES - ROT_DIM // 2, 1) * sa_ref[...]
                + pltpu.roll(x, ROT_DIM // 2, 1) * sb_ref[...])

    qn[...] = rot(q_ref[...]) * (HEAD_DIM ** -0.5)
    kn[...] = rot(k_ref[...])

    def put_kv(dst_rows, kc, vac, vbc, k, v):
        kc[dst_rows, :] = k.astype(BF16)
        vac[dst_rows, :] = jnp.where(head0, v, 1.0).astype(BF16)
        vbc[dst_rows, :] = jnp.where(head0, 1.0, v).astype(BF16)

    put_kv(slice(None), k1, va1, vb1, kn[...], v_ref[...])
    for d, qc, kv in ((4, q4, (k4, va4, vb4)), (16, q16, (k16, va16, vb16))):
        ln = seq // d
        for r in range(d):
            rows = slice(r * ln, (r + 1) * ln)
            qc[rows, :] = qn[pl.ds(r, ln, stride=d), :]
            put_kv(rows, *kv, kn[pl.ds(r, ln, stride=d), :], v_ref[pl.ds(r, ln, stride=d), :])

    unroll = 16

    def run_branch(qsrc, kv, dsts, blocks_per_class):
        ksrc, vsrcs = kv[0], kv[1:]
        a_dst, d_dst, m_dst = dsts

        def attend(b, first):
            cur = pl.ds(pl.multiple_of(b * ATTN_BLOCK, ATTN_BLOCK), ATTN_BLOCK)
            qb = qsrc[cur, :]
            prv = pl.ds(pl.multiple_of(jnp.maximum(b - 1, 0) * ATTN_BLOCK, ATTN_BLOCK),
                        ATTN_BLOCK)
            bias = bias_s[int(first)] if isinstance(first, bool) else bias_s[first]
            kk = jnp.concatenate([ksrc[prv, :], ksrc[cur, :]], axis=0)
            res, mx = [], []
            for h in range(2):
                mine = head0 if h == 0 else jnp.logical_not(head0)
                qh = jnp.where(mine, qb, 0.0).astype(BF16)
                s = lax.dot_general(qh, kk, (((1,), (1,)), ((), ())), preferred_element_type=F32)
                s = s + bias
                m = jnp.max(s, axis=-1, keepdims=True)
                p = jnp.exp(s - m).astype(BF16)
                vh = jnp.concatenate([vsrcs[h][prv, :], vsrcs[h][cur, :]], axis=0)
                res.append(jnp.dot(p, vh, preferred_element_type=F32))
                mx.append(m)
            a_dst[cur, :] = jnp.where(head0, res[0], res[1])
            d_dst[cur, :] = pltpu.roll(jnp.where(head0, res[1], res[0]), HEAD_DIM, 1)
            m_dst[cur, :] = jnp.where(head0, mx[0], mx[1])

        def body(i, carry):
            for u in range(unroll):
                b = i * unroll + u
                if unroll % blocks_per_class == 0:
                    first = u % blocks_per_class == 0
                elif blocks_per_class % unroll == 0 and u != 0:
                    first = False
                else:
                    first = jnp.where(jnp.asarray(b % blocks_per_class == 0), 1, 0)
                attend(b, first)
            return carry

        lax.fori_loop(0, nblk_total // unroll, body, 0)

    res1, res4, res16, res_cm = (a1, d1, m1), (a4, d4, m4), (a16, d16, m16), (acm, dcm, mcm)
    run_branch(qn, (k1, va1, vb1), res1, nblk_total)
    for d, qc, kv, res in ((4, q4, (k4, va4, vb4), res4), (16, q16, (k16, va16, vb16), res16)):
        run_branch(qc, kv, res_cm, nblk_total // d)
        ln = seq // d
        for cm, nat in zip(res_cm, res):
            for r in range(d):
                nat[pl.ds(r, ln, stride=d), :] = cm[r * ln:(r + 1) * ln, :]

    ma, mb, mc = m1[...], m4[...], m16[...]
    mx = jnp.maximum(jnp.maximum(ma, mb), mc)
    ea, eb, ec = jnp.exp(ma - mx), jnp.exp(mb - mx), jnp.exp(mc - mx)
    o_ref[...] = ((ea * a1[...] + eb * a4[...] + ec * a16[...])
                  / (ea * d1[...] + eb * d4[...] + ec * d16[...]))


def _rotary_tables(seq):
    half = ROT_DIM // 2
    inv_freq = ROPE_THETA ** (-jnp.arange(0, ROT_DIM, 2, dtype=F32) / ROT_DIM)
    ang = jnp.arange(seq, dtype=F32)[:, None] * inv_freq[None, :]
    cos, sin = jnp.cos(ang), jnp.sin(ang)
    pos = jnp.arange(LANES) % HEAD_DIM
    fidx = pos % half
    in_lo = pos < half
    in_hi = (pos >= half) & (pos < ROT_DIM)
    cos_t = jnp.where((in_lo | in_hi)[None, :], cos[:, fidx], 1.0)
    sa = jnp.where(in_lo[None, :], -sin[:, fidx], 0.0)
    sb = jnp.where(in_hi[None, :], sin[:, fidx], 0.0)
    return cos_t.astype(F32), sa.astype(F32), sb.astype(F32)


def _attention(proj3, conv_w_cols, attn_w_cols):
    b, seq, _ = proj3.shape
    assert seq % (ATTN_BLOCK * DILATIONS[-1]) == 0
    n_pairs = attn_w_cols // LANES
    qoff = 3 * conv_w_cols // LANES
    cos_t, sa, sb = _rotary_tables(seq)
    blk = lambda off: pl.BlockSpec((None, seq, LANES), lambda i, j: (i, 0, off + j))
    tab = pl.BlockSpec((seq, LANES), lambda i, j: (0, 0))
    big = pltpu.VMEM((seq, LANES), F32)
    return pl.pallas_call(
        functools.partial(_attn_kernel, seq=seq),
        grid=(b, n_pairs),
        in_specs=[blk(qoff), blk(qoff + n_pairs), blk(qoff + 2 * n_pairs), tab, tab, tab],
        out_specs=pl.BlockSpec((None, seq, LANES), lambda i, j: (i, 0, j)),
        out_shape=jax.ShapeDtypeStruct((b, seq, attn_w_cols), F32),
        scratch_shapes=([big] * 16 + [pltpu.VMEM((seq, LANES), BF16)] * 9
                        + [pltpu.VMEM((2, ATTN_BLOCK, 2 * ATTN_BLOCK), F32)]),
        compiler_params=pltpu.CompilerParams(
            dimension_semantics=("arbitrary", "arbitrary"),
            vmem_limit_bytes=48 * 1024 * 1024),
        name="attention",
    )(proj3, proj3, proj3, cos_t, sa, sb)


def _out_proj_kernel(cx_ref, cb_ref, cc_ref, hx_ref, hc_ref, ya_ref, x_ref,
                     cw_ref, cg_ref, ag_ref, wo_ref, fg_ref, wrh_ref, wrl_ref, br_ref,
                     x1_ref, h2_ref, idx_ref, gate_ref, rank_ref, cnt_ref, carry,
                     *, tile, n_experts):
    first_tile_of_seq = pl.program_id(1) == 0
    first_step = (pl.program_id(0) == 0) & first_tile_of_seq

    @pl.when(first_step)
    def _():
        carry[...] = jnp.zeros_like(carry)

    u = cc_ref[...] * cx_ref[...]
    uh = jnp.where(first_tile_of_seq, 0.0, hc_ref[...] * hx_ref[...])
    row8 = lax.broadcasted_iota(jnp.int32, (SUBLANES, 1), 0)

    def shifted(k):
        r = pltpu.roll(u, k, 0)
        top = jnp.where(row8 < k, pltpu.roll(uh, k, 0), r[:SUBLANES])
        return jnp.concatenate([top, r[SUBLANES:]], axis=0)

    conv = cw_ref[2:3, :] * u + cw_ref[1:2, :] * shifted(1) + cw_ref[0:1, :] * shifted(2)
    y_conv = cb_ref[...] * conv

    mixed = jnp.concatenate(
        [_rms(y_conv, cg_ref[...]), _rms(ya_ref[...], ag_ref[...])], axis=-1).astype(BF16)
    x1 = x_ref[...] + jnp.dot(mixed, wo_ref[...], preferred_element_type=F32)
    x1_ref[...] = x1
    h2 = _rms(x1, fg_ref[...])
    d = h2.shape[-1]
    chunks = d // LANES
    for c in range(chunks):
        h2_ref[pl.ds(c, tile, stride=chunks), :] = h2[:, c * LANES:(c + 1) * LANES]

    h2_hi = h2.astype(BF16)
    h2_lo = (h2 - h2_hi.astype(F32)).astype(BF16)
    logits = (jnp.dot(h2_hi, wrh_ref[...], preferred_element_type=F32)
              + jnp.dot(h2_hi, wrl_ref[...], preferred_element_type=F32)
              + jnp.dot(h2_lo, wrh_ref[...], preferred_element_type=F32)) + br_ref[...]
    eio = lax.broadcasted_iota(jnp.int32, (tile, n_experts), 1).astype(F32)
    work = logits
    vals, idxs = [], []
    for _ in range(TOP_K):
        m = jnp.max(work, axis=1, keepdims=True)
        ik = jnp.min(jnp.where(work == m, eio, float(n_experts)), axis=1, keepdims=True)
        vals.append(m)
        idxs.append(ik)
        work = jnp.where(eio == ik, -jnp.inf, work)
    exps = [jnp.exp(v - vals[0]) for v in vals]
    tot = exps[0] + exps[1] + exps[2] + exps[3]

    onehot = jnp.zeros((tile, n_experts), F32)
    for ik in idxs:
        onehot = onehot + (eio == ik).astype(F32)
    ri = lax.broadcasted_iota(jnp.int32, (tile, tile), 0)
    ci = lax.broadcasted_iota(jnp.int32, (tile, tile), 1)
    tri = (ci < ri).astype(BF16)
    before = jnp.dot(tri, onehot.astype(BF16), preferred_element_type=F32) + carry[...]
    carry[...] = carry[...] + jnp.sum(onehot, axis=0, keepdims=True)
    cnt_ref[...] = carry[...].astype(jnp.int32)

    lio = lax.broadcasted_iota(jnp.int32, (tile, LANES), 1)
    idx_out = jnp.zeros((tile, LANES), jnp.int32)
    gate_out = jnp.zeros((tile, LANES), F32)
    rank_out = jnp.zeros((tile, LANES), jnp.int32)
    for k in range(TOP_K):
        rk = jnp.sum(jnp.where(eio == idxs[k], before, 0.0), axis=1, keepdims=True)
        idx_out = jnp.where(lio == k, idxs[k].astype(jnp.int32), idx_out)
        gate_out = jnp.where(lio == k, exps[k] / tot, gate_out)
        rank_out = jnp.where(lio == k, rk.astype(jnp.int32), rank_out)
    idx_ref[...] = idx_out
    gate_ref[...] = gate_out
    rank_ref[...] = rank_out


def _out_proj(proj3, y_attn, x3, conv_w, conv_g, attn_g, w_out_bf16, ffn_g, w_router, b_router,
              tile=256):
    b, seq, d = x3.shape
    cw_cols = conv_w.shape[1]
    aw_cols = y_attn.shape[2]
    n_experts = w_router.shape[1]
    w_router_hi = w_router.astype(BF16)
    w_router_lo = (w_router - w_router_hi.astype(F32)).astype(BF16)
    chunks = d // LANES
    tiles = seq // tile
    n = b * seq
    halo_blocks = tile // SUBLANES
    row_blk = lambda width, col: pl.BlockSpec((None, tile, width), lambda i, j: (i, j, col))
    halo = lambda col: pl.BlockSpec(
        (None, SUBLANES, cw_cols), lambda i, j: (i, jnp.maximum(j * halo_blocks - 1, 0), col))
    const = lambda shape: pl.BlockSpec(shape, lambda i, j: (0,) * len(shape))
    flat = lambda width: pl.BlockSpec((tile, width), lambda i, j: (i * tiles + j, 0))
    outs = pl.pallas_call(
        functools.partial(_out_proj_kernel, tile=tile, n_experts=n_experts),
        grid=(b, tiles),
        in_specs=[
            row_blk(cw_cols, 0), row_blk(cw_cols, 1), row_blk(cw_cols, 2), halo(0), halo(2),
            row_blk(aw_cols, 0), row_blk(d, 0),
            const((CONV_K, cw_cols)), const((1, cw_cols)), const((1, aw_cols)),
            const((d, d)), const((1, d)), const((d, n_experts)), const((d, n_experts)),
            const((1, n_experts)),
        ],
        out_specs=[
            row_blk(d, 0),
            pl.BlockSpec((tile * chunks, LANES), lambda i, j: (i * tiles + j, 0)),
            flat(LANES), flat(LANES), flat(LANES),
            const((1, n_experts)),
        ],
        out_shape=[
            jax.ShapeDtypeStruct((b, seq, d), F32),
            jax.ShapeDtypeStruct((n * chunks, LANES), F32),
            jax.ShapeDtypeStruct((n, LANES), jnp.int32),
            jax.ShapeDtypeStruct((n, LANES), F32),
            jax.ShapeDtypeStruct((n, LANES), jnp.int32),
            jax.ShapeDtypeStruct((1, n_experts), jnp.int32),
        ],
        scratch_shapes=[pltpu.VMEM((1, n_experts), F32)],
        compiler_params=pltpu.CompilerParams(
            dimension_semantics=("arbitrary", "arbitrary"),
            vmem_limit_bytes=48 * 1024 * 1024),
        name="out_proj",
    )(proj3, proj3, proj3, proj3, proj3, y_attn, x3, conv_w, conv_g, attn_g, w_out_bf16,
      ffn_g, w_router_hi, w_router_lo, b_router)
    return outs


DMA_ISSUE_UNROLL = 4


def _dispatch_kernel(idx_ref, rank_ref, offs_ref, pad_start, pad_rows, used_blocks, h2_ref, xs_hbm,
                     dest_ref, zbuf, sem, zsem, *, tile, chunks, n_experts, n_row_blocks):
    gran = MOE_ROW_GRAN

    @pl.when(pl.program_id(0) == 0)
    def _():
        zbuf[...] = jnp.zeros_like(zbuf)

        def zero_copy(first_row, n_rows):
            return pltpu.make_async_copy(_token_rows(zbuf, 0, n_rows, chunks),
                                         _token_rows(xs_hbm, first_row, n_rows, chunks), zsem)

        def for_each_zero_run(act):
            def per_expert(e, carry):
                row = pad_start[e]
                left = pad_rows[e]
                for bit in reversed(range(gran.bit_length() - 1)):
                    take = (left >> bit) & 1

                    @pl.when(take == 1)
                    def _():
                        act(zero_copy(row, 1 << bit))

                    row = row + (take << bit)
                return carry

            lax.fori_loop(0, n_experts, per_expert, 0)

            def tail(blk, carry):
                act(zero_copy(blk * gran, gran))
                return carry

            lax.fori_loop(used_blocks[0], n_row_blocks, tail, 0)

        for_each_zero_run(lambda cp: cp.start())
        for_each_zero_run(lambda cp: cp.wait())

    def issue(g, carry):
        for u in range(DMA_ISSUE_UNROLL):
            t = g * DMA_ISSUE_UNROLL + u
            for k in range(TOP_K):
                a = t * TOP_K + k
                dest = offs_ref[idx_ref[0, 0, a]] + rank_ref[0, 0, a]
                dest_ref[0, 0, a] = dest
                pltpu.make_async_copy(_token_rows(h2_ref, t, 1, chunks),
                                      _token_rows(xs_hbm, dest, 1, chunks), sem).start()
        return carry

    lax.fori_loop(0, tile // DMA_ISSUE_UNROLL, issue, 0)
    counted = _token_rows(xs_hbm, 0, tile * TOP_K, chunks)
    pltpu.make_async_copy(counted, counted, sem).wait()


def _dispatch(idx_flat, rank_flat, offs, pad_start, pad_rows, used_blocks, h2_tm, n_rows, tile=256):
    steps = idx_flat.shape[0] // (tile * TOP_K)
    per_tile = lambda a: a.reshape(steps, 1, tile * TOP_K)
    chunks = h2_tm.shape[0] // (idx_flat.shape[0] // TOP_K)
    gran = MOE_ROW_GRAN
    smem = pl.BlockSpec(memory_space=pltpu.SMEM)
    tile_smem = pl.BlockSpec((1, 1, tile * TOP_K), lambda i: (i, 0, 0), memory_space=pltpu.SMEM)
    xs, dest = pl.pallas_call(
        functools.partial(_dispatch_kernel, tile=tile, chunks=chunks,
                          n_experts=pad_start.shape[0], n_row_blocks=n_rows // gran),
        grid=(steps,),
        in_specs=[
            tile_smem, tile_smem, smem, smem, smem, smem,
            pl.BlockSpec((tile * chunks, LANES), lambda i: (i, 0)),
        ],
        out_specs=[pl.BlockSpec(memory_space=pl.ANY), tile_smem],
        out_shape=[jax.ShapeDtypeStruct((n_rows * chunks, LANES), F32),
                   jax.ShapeDtypeStruct((steps, 1, tile * TOP_K), jnp.int32)],
        scratch_shapes=[pltpu.VMEM((gran * chunks, LANES), F32),
                        pltpu.SemaphoreType.DMA(()), pltpu.SemaphoreType.DMA(())],
        compiler_params=pltpu.CompilerParams(dimension_semantics=("arbitrary",)),
        name="dispatch",
    )(per_tile(idx_flat), per_tile(rank_flat), offs, pad_start, pad_rows, used_blocks, h2_tm)
    return xs, dest.reshape(-1)


MOE_WEIGHT_SLOTS = 3


def _moe_kernel(item_e, item_row0, item_nblk, used_blocks, live_items, xs_hbm, wgu_hbm, wdn_hbm,
                bgu_ref, bd_ref, ys_hbm, iobuf, xb, yacc, wgf, wuf, wdf, wgb, wub, wdb,
                sem_io, sem_w, *, chunks, n_col_steps, n_row_blocks):
    s = pl.program_id(0)
    nb = item_nblk[s]
    row0 = item_row0[s]
    gran = MOE_ROW_GRAN
    tn = MOE_COL_TILE
    de = n_col_steps * tn
    total_chunks = live_items[0] * n_col_steps

    def staged(slot):
        return iobuf.at[slot]

    def ys_block(blk):
        return _token_rows(ys_hbm, blk * gran, gran, chunks)

    def weight_copies(g):
        e = item_e[g // n_col_steps]
        col = pl.multiple_of((g % n_col_steps) * tn, tn)
        slot = g % MOE_WEIGHT_SLOTS
        return (
            pltpu.make_async_copy(wgu_hbm.at[e, :, pl.ds(col, tn)], wgf.at[slot], sem_w.at[slot]),
            pltpu.make_async_copy(wgu_hbm.at[e, :, pl.ds(pl.multiple_of(de + col, tn), tn)],
                                  wuf.at[slot], sem_w.at[slot]),
            pltpu.make_async_copy(wdn_hbm.at[e, pl.ds(col, tn), :], wdf.at[slot], sem_w.at[slot]),
        )

    def fetch(g):
        @pl.when(g < total_chunks)
        def _():
            for cp in weight_copies(g):
                cp.start()

    @pl.when(s == 0)
    def _():
        for g in range(MOE_WEIGHT_SLOTS - 1):
            fetch(g)

    @pl.when(nb > 0)
    def _():
        def io_slot(sb):
            return sb % MOE_IO_SLOTS

        def x_copy(sb):
            return pltpu.make_async_copy(
                _token_rows(xs_hbm, row0 + sb * gran, gran, chunks), staged(io_slot(sb)),
                sem_io.at[io_slot(sb)])

        def y_copy(sb):
            return pltpu.make_async_copy(staged(io_slot(sb)), ys_block(row0 // gran + sb),
                                         sem_io.at[io_slot(sb)])

        for sb in range(MOE_IO_SLOTS):
            @pl.when(sb < nb)
            def _():
                x_copy(sb).start()

        def block_rows(sb):
            return pl.ds(pl.multiple_of(sb * gran, gran), gran)

        def column_step(j, phase):
            g = s * n_col_steps + j
            slot = g % MOE_WEIGHT_SLOTS
            for cp in weight_copies(g):
                cp.wait()
            fetch(g + MOE_WEIGHT_SLOTS - 1)
            col = pl.multiple_of(j * tn, tn)
            bg = bgu_ref[:, pl.ds(col, tn)]
            bu = bgu_ref[:, pl.ds(pl.multiple_of(de + col, tn), tn)]

            def cast_weights():
                wgb[...] = wgf[slot].astype(BF16)
                wub[...] = wuf[slot].astype(BF16)
                wdb[...] = wdf[slot].astype(BF16)

            def partial_out(sb):
                x = xb[block_rows(sb), :]
                gate = jnp.dot(x, wgb[...], preferred_element_type=F32) + bg
                up = jnp.dot(x, wub[...], preferred_element_type=F32) + bu
                gate = jnp.minimum(gate, SWIGLU_LIMIT)
                up = jnp.clip(up, -SWIGLU_LIMIT, SWIGLU_LIMIT)
                act = (up + 1.0) * (gate * jax.nn.sigmoid(SWIGLU_ALPHA * gate))
                return jnp.dot(act.astype(BF16), wdb[...], preferred_element_type=F32)

            def trip(first, count):
                blocks = [first + u for u in range(count)]
                if phase == "first":
                    for sb in blocks:
                        x_copy(sb).wait()
                if phase == "last":
                    for sb in blocks:
                        @pl.when(sb >= MOE_IO_SLOTS)
                        def _():
                            y_copy(sb - MOE_IO_SLOTS).wait()
                parts = []
                for sb in blocks:
                    if phase == "first":
                        for c in range(chunks):
                            xb[block_rows(sb), c * LANES:(c + 1) * LANES] = (
                                iobuf[io_slot(sb), pl.ds(c, gran, stride=chunks), :].astype(BF16))
                    parts.append(partial_out(sb))
                for sb, part in zip(blocks, parts):
                    if phase == "first":
                        yacc[block_rows(sb), :] = part + bd_ref[...]
                    elif phase == "middle":
                        yacc[block_rows(sb), :] += part
                    else:
                        y = yacc[block_rows(sb), :] + part
                        for c in range(chunks):
                            iobuf[io_slot(sb), pl.ds(c, gran, stride=chunks), :] = (
                                y[:, c * LANES:(c + 1) * LANES])
                for sb in blocks:
                    if phase == "first":
                        @pl.when(sb + MOE_IO_SLOTS < nb)
                        def _():
                            x_copy(sb + MOE_IO_SLOTS).start()
                    if phase == "last":
                        y_copy(sb).start()

            full = MOE_TRIP_BLOCKS

            def tail(left):
                count = full // 2
                while count >= 1:
                    @pl.when(left % (2 * count) >= count)
                    def _():
                        trip(nb - left % (2 * count), count)

                    count //= 2

            @pl.when(nb >= full)
            def _():
                cast_weights()
                trip(0, full)

                def body(i, c):
                    trip(i * full, full)
                    return c

                lax.fori_loop(1, nb // full, body, 0)
                tail(nb % full)

            @pl.when(nb < full)
            def _():
                cast_weights()
                tail(nb)

        column_step(0, "first")

        def middle(j, carry):
            column_step(j, "middle")
            return carry

        lax.fori_loop(1, n_col_steps - 1, middle, 0)
        column_step(n_col_steps - 1, "last")

        for k in range(MOE_IO_SLOTS):
            @pl.when(nb > k)
            def _():
                y_copy(nb - 1 - k).wait()

    @pl.when(s == pl.num_programs(0) - 1)
    def _():
        iobuf[0] = jnp.zeros(iobuf.shape[1:], iobuf.dtype)

        def fill(blk, carry):
            cp = pltpu.make_async_copy(staged(0), ys_block(blk), sem_io.at[0])
            cp.start()
            cp.wait()
            return carry

        lax.fori_loop(used_blocks[0], n_row_blocks, fill, 0)


def _moe(item_e, item_row0, item_nblk, used_blocks, live_items, xs, w_gate_up, b_gate_up, w_down,
         b_down, n_rows):
    n_experts, d, two_de = w_gate_up.shape
    de = two_de // 2
    chunks = d // LANES
    tn = MOE_COL_TILE
    n_col_steps = de // tn
    assert n_col_steps >= 2
    n_items = item_e.shape[0]
    gran = MOE_ROW_GRAN
    slots = MOE_WEIGHT_SLOTS

    bgu3 = b_gate_up.reshape(n_experts, 1, two_de)
    bd3 = b_down.reshape(n_experts, 1, d)
    per_expert = lambda width: pl.BlockSpec((None, 1, width), lambda s, e, *_: (e[s], 0, 0))
    grid_spec = pltpu.PrefetchScalarGridSpec(
        num_scalar_prefetch=5,
        grid=(n_items,),
        in_specs=[
            pl.BlockSpec(memory_space=pl.ANY),
            pl.BlockSpec(memory_space=pl.ANY),
            pl.BlockSpec(memory_space=pl.ANY),
            per_expert(two_de),
            per_expert(d),
        ],
        out_specs=pl.BlockSpec(memory_space=pl.ANY),
        scratch_shapes=[
            pltpu.VMEM((MOE_IO_SLOTS, gran * chunks, LANES), F32),
            pltpu.VMEM((MOE_ITEM_ROWS, d), BF16),
            pltpu.VMEM((MOE_ITEM_ROWS, d), F32),
            pltpu.VMEM((slots, d, tn), F32),
            pltpu.VMEM((slots, d, tn), F32),
            pltpu.VMEM((slots, tn, d), F32),
            pltpu.VMEM((d, tn), BF16),
            pltpu.VMEM((d, tn), BF16),
            pltpu.VMEM((tn, d), BF16),
            pltpu.SemaphoreType.DMA((MOE_IO_SLOTS,)),
            pltpu.SemaphoreType.DMA((slots,)),
        ],
    )
    return pl.pallas_call(
        functools.partial(_moe_kernel, chunks=chunks, n_col_steps=n_col_steps,
                          n_row_blocks=n_rows // gran),
        grid_spec=grid_spec,
        out_shape=jax.ShapeDtypeStruct((n_rows * chunks, LANES), F32),
        compiler_params=pltpu.CompilerParams(
            dimension_semantics=("arbitrary",),
            vmem_limit_bytes=56 * 1024 * 1024),
        name="moe",
    )(item_e, item_row0, item_nblk, used_blocks, live_items, xs, w_gate_up, w_down, bgu3, bd3)


def _combine_kernel(dest_ref, ys_hbm, x1_ref, gate_ref, fg_ref, o_ref, gbuf, sem,
                    *, tile, chunks, final_norm):
    i = pl.program_id(0)
    slot = i % 2
    per_tile = tile * TOP_K
    stride = _token_row_stride(chunks)

    def gather(step, dst_slot):
        base = step * per_tile

        def issue(g, carry):
            for u in range(DMA_ISSUE_UNROLL):
                t = g * DMA_ISSUE_UNROLL + u
                for k in range(TOP_K):
                    pltpu.make_async_copy(
                        _token_rows(ys_hbm, dest_ref[base + t * TOP_K + k], 1, chunks),
                        gbuf.at[dst_slot, pl.ds(pl.multiple_of((k * tile + t) * stride, SUBLANES),
                                                chunks), :],
                        sem.at[dst_slot]).start()
            return carry

        lax.fori_loop(0, tile // DMA_ISSUE_UNROLL, issue, 0)

    @pl.when(i == 0)
    def _():
        gather(0, 0)

    @pl.when(i + 1 < pl.num_programs(0))
    def _():
        gather(i + 1, 1 - slot)

    counted = gbuf.at[slot, pl.ds(0, per_tile * chunks), :]
    pltpu.make_async_copy(counted, counted, sem.at[slot]).wait()

    gates = gate_ref[...]
    gk = [gates[:, k:k + 1] for k in range(TOP_K)]
    cols = []
    for c in range(chunks):
        acc = x1_ref[:, c * LANES:(c + 1) * LANES]
        for k in range(TOP_K):
            acc = acc + gk[k] * gbuf[slot, pl.ds(k * tile * stride + c, tile, stride=stride), :]
        cols.append(acc)
    out = jnp.concatenate(cols, axis=-1)
    if final_norm:
        out = _rms(out, fg_ref[...])
    o_ref[...] = out


def _combine(dest, ys, x1, gates, final_g, final_norm, tile=128):
    n, d = x1.shape
    chunks = d // LANES
    steps = n // tile
    grid_spec = pltpu.PrefetchScalarGridSpec(
        num_scalar_prefetch=1,
        grid=(steps,),
        in_specs=[
            pl.BlockSpec(memory_space=pl.ANY),
            pl.BlockSpec((tile, d), lambda i, *_: (i, 0)),
            pl.BlockSpec((tile, LANES), lambda i, *_: (i, 0)),
            pl.BlockSpec((1, d), lambda i, *_: (0, 0)),
        ],
        out_specs=pl.BlockSpec((tile, d), lambda i, *_: (i, 0)),
        scratch_shapes=[pltpu.VMEM((2, TOP_K * tile * _token_row_stride(chunks), LANES), F32),
                        pltpu.SemaphoreType.DMA((2,))],
    )
    return pl.pallas_call(
        functools.partial(_combine_kernel, tile=tile, chunks=chunks, final_norm=final_norm),
        grid_spec=grid_spec,
        out_shape=jax.ShapeDtypeStruct((n, d), F32),
        compiler_params=pltpu.CompilerParams(dimension_semantics=("arbitrary",),
                                             vmem_limit_bytes=40 * 1024 * 1024),
        name="combine",
    )(dest, ys, x1, gates, final_g)


def _routing_tables(counts, n_rows_cap):
    n_experts = counts.shape[0]
    gran, item_rows = MOE_ROW_GRAN, MOE_ITEM_ROWS
    padded = ((counts + gran - 1) // gran) * gran
    ends = jnp.cumsum(padded)
    offs = ends - padded
    items_per = (padded + item_rows - 1) // item_rows
    item_ends = jnp.cumsum(items_per)
    item_starts = item_ends - items_per
    n_items = n_experts + n_rows_cap // item_rows
    slot = jnp.arange(n_items, dtype=jnp.int32)
    total = item_ends[-1]
    live = slot < total
    live_slot = jnp.minimum(slot, total - 1)
    e_of = jnp.minimum(jnp.sum((live_slot[:, None] >= item_ends[None, :]).astype(jnp.int32), axis=1),
                       n_experts - 1).astype(jnp.int32)
    local = slot - item_starts[e_of]
    row0 = offs[e_of] + local * item_rows
    rows = jnp.clip(padded[e_of] - local * item_rows, 0, item_rows)
    nblk = jnp.where(live, rows // gran, 0).astype(jnp.int32)
    row0 = jnp.where(live, row0, 0).astype(jnp.int32)
    used_blocks = (ends[-1:] // gran).astype(jnp.int32)
    pad_start = (offs + counts).astype(jnp.int32)
    pad_rows = (padded - counts).astype(jnp.int32)
    live_items = total.reshape(1).astype(jnp.int32)
    return (offs.astype(jnp.int32), e_of, row0, nblk, used_blocks, live_items, pad_start,
            pad_rows)


def _layer(x3, mix_g, w_in, conv_w, conv_g, attn_g, w_out, ffn_g, w_router, b_router,
           w_gate_up, b_gate_up, w_down, b_down, final_g, final_norm):
    b, seq, d = x3.shape
    n = b * seq
    cw_cols = conv_w.shape[1]
    aw_cols = attn_g.shape[0]
    n_experts = w_router.shape[1]
    chunks = d // LANES

    proj = _in_proj(x3.reshape(n, d), mix_g.reshape(1, d), w_in.astype(BF16))
    proj3 = proj.reshape(b, seq, -1)
    y_attn = _attention(proj3, cw_cols, aw_cols)
    x1, h2_tm, idx, gates, rank, counts = _out_proj(
        proj3, y_attn, x3, conv_w, conv_g.reshape(1, -1), attn_g.reshape(1, -1),
        w_out.astype(BF16), ffn_g.reshape(1, d), w_router, b_router.reshape(1, -1))

    n_rows = n * TOP_K + n_experts * MOE_ROW_GRAN
    (offs, item_e, item_row0, item_nblk, used_blocks, live_items, pad_start,
     pad_rows) = _routing_tables(counts[0], n_rows)
    idx_flat = idx[:, :TOP_K].reshape(n * TOP_K)
    rank_flat = rank[:, :TOP_K].reshape(n * TOP_K)
    xs, dest = _dispatch(idx_flat, rank_flat, offs, pad_start, pad_rows, used_blocks, h2_tm,
                         n_rows)
    ys = _moe(item_e, item_row0, item_nblk, used_blocks, live_items, xs, w_gate_up, b_gate_up,
              w_down, b_down, n_rows)
    out = _combine(dest, ys, x1.reshape(n, d), gates, final_g.reshape(1, d), final_norm)
    return out.reshape(b, seq, d)


def kernel(x, mix_norm_g, w_in, conv_w, conv_norm_g, attn_norm_g, w_out, ffn_norm_g, w_router,
           b_router, w_gate_up, b_gate_up, w_down, b_down, final_norm_g):
    depth = w_in.shape[0]
    for layer in range(depth):
        x = _layer(x, mix_norm_g[layer], w_in[layer], conv_w[layer], conv_norm_g[layer],
                   attn_norm_g[layer], w_out[layer], ffn_norm_g[layer], w_router[layer],
                   b_router[layer], w_gate_up[layer], b_gate_up[layer], w_down[layer],
                   b_down[layer], final_norm_g, layer == depth - 1)
    return x
```

```python
import functools

import jax
import jax.numpy as jnp
from jax import lax
from jax.experimental import pallas as pl
from jax.experimental.pallas import tpu as pltpu

HEAD_DIM = 64
CONV_K = 3
ROT_DIM = HEAD_DIM // 4
ROPE_THETA = 500000.0
DILATIONS = (1, 4, 16)
ATTN_BLOCK = 128
TOP_K = 4
SWIGLU_LIMIT = 7.0
SWIGLU_ALPHA = 1.702
EPS = 1e-5

LANES = 128
SUBLANES = 8

MOE_ROW_GRAN = 256
MOE_ITEM_ROWS = 1280
MOE_COL_TILE = 256
MOE_TRIP_BLOCKS = 2
MOE_IO_SLOTS = 2 * MOE_TRIP_BLOCKS
BF16 = jnp.bfloat16
F32 = jnp.float32


def _token_row_stride(chunks):
    return chunks + SUBLANES


def _token_rows(ref, first_token, n_tokens, chunks):
    return ref.at[pl.ds(pl.multiple_of(first_token * chunks, chunks), n_tokens * chunks), :]


def _rms(x, g):
    return x * lax.rsqrt(jnp.mean(x * x, axis=-1, keepdims=True) + EPS) * g


def _in_proj_kernel(x_ref, g_ref, w_ref, o_ref, xn_ref):
    @pl.when(pl.program_id(1) == 0)
    def _():
        xn_ref[...] = _rms(x_ref[...], g_ref[...]).astype(BF16)

    o_ref[...] = jnp.dot(xn_ref[...], w_ref[...], preferred_element_type=F32)


def _in_proj(x2, g, w_bf16, tm=1024):
    n, d = x2.shape
    cols = w_bf16.shape[1]
    tn = 1024 if cols % 1024 == 0 else 512
    return pl.pallas_call(
        _in_proj_kernel,
        grid=(n // tm, cols // tn),
        in_specs=[
            pl.BlockSpec((tm, d), lambda i, j: (i, 0)),
            pl.BlockSpec((1, d), lambda i, j: (0, 0)),
            pl.BlockSpec((d, tn), lambda i, j: (0, j)),
        ],
        out_specs=pl.BlockSpec((tm, tn), lambda i, j: (i, j)),
        out_shape=jax.ShapeDtypeStruct((n, cols), F32),
        scratch_shapes=[pltpu.VMEM((tm, d), BF16)],
        compiler_params=pltpu.CompilerParams(
            dimension_semantics=("arbitrary", "arbitrary"),
            vmem_limit_bytes=48 * 1024 * 1024),
        name="in_proj",
    )(x2, g, w_bf16)


def _attn_kernel(q_ref, k_ref, v_ref, cos_ref, sa_ref, sb_ref, o_ref,
                 qn, kn, q4, q16, a1, d1, m1, a4, d4, m4, a16, d16, m16, acm, dcm, mcm,
                 k1, va1, vb1, k4, va4, vb4, k16, va16, vb16, bias_s, *, seq):
    nblk_total = seq // ATTN_BLOCK
    lane = lax.broadcasted_iota(jnp.int32, (1, LANES), 1)
    head0 = lane < HEAD_DIM

    qi = lax.broadcasted_iota(jnp.int32, (ATTN_BLOCK, 2 * ATTN_BLOCK), 0)
    kj = lax.broadcasted_iota(jnp.int32, (ATTN_BLOCK, 2 * ATTN_BLOCK), 1)
    band = (kj >= qi) & (kj <= qi + ATTN_BLOCK)
    neg = jnp.float32(-jnp.inf)
    bias_s[0] = jnp.where(band, 0.0, neg)
    bias_s[1] = jnp.where(band & (kj >= ATTN_BLOCK), 0.0, neg)

    def rot(x):
        return (x * cos_ref[...] + pltpu.roll(x, LANES - ROT_DIM // 2, 1) * sa_ref[...]
                + pltpu.roll(x, ROT_DIM // 2, 1) * sb_ref[...])

    qn[...] = rot(q_ref[...]) * (HEAD_DIM ** -0.5)
    kn[...] = rot(k_ref[...])

    def put_kv(dst_rows, kc, vac, vbc, k, v):
        kc[dst_rows, :] = k.astype(BF16)
        vac[dst_rows, :] = jnp.where(head0, v, 1.0).astype(BF16)
        vbc[dst_rows, :] = jnp.where(head0, 1.0, v).astype(BF16)

    put_kv(slice(None), k1, va1, vb1, kn[...], v_ref[...])
    for d, qc, kv in ((4, q4, (k4, va4, vb4)), (16, q16, (k16, va16, vb16))):
        ln = seq // d
        for r in range(d):
            rows = slice(r * ln, (r + 1) * ln)
            qc[rows, :] = qn[pl.ds(r, ln, stride=d), :]
            put_kv(rows, *kv, kn[pl.ds(r, ln, stride=d), :], v_ref[pl.ds(r, ln, stride=d), :])

    unroll = 16

    def run_branch(qsrc, kv, dsts, blocks_per_class):
        ksrc, vsrcs = kv[0], kv[1:]
        a_dst, d_dst, m_dst = dsts

        def attend(b, first):
            cur = pl.ds(pl.multiple_of(b * ATTN_BLOCK, ATTN_BLOCK), ATTN_BLOCK)
            qb = qsrc[cur, :]
            prv = pl.ds(pl.multiple_of(jnp.maximum(b - 1, 0) * ATTN_BLOCK, ATTN_BLOCK),
                        ATTN_BLOCK)
            bias = bias_s[int(first)] if isinstance(first, bool) else bias_s[first]
            kk = jnp.concatenate([ksrc[prv, :], ksrc[cur, :]], axis=0)
            res, mx = [], []
            for h in range(2):
                mine = head0 if h == 0 else jnp.logical_not(head0)
                qh = jnp.where(mine, qb, 0.0).astype(BF16)
                s = lax.dot_general(qh, kk, (((1,), (1,)), ((), ())), preferred_element_type=F32)
                s = s + bias
                m = jnp.max(s, axis=-1, keepdims=True)
                p = jnp.exp(s - m).astype(BF16)
                vh = jnp.concatenate([vsrcs[h][prv, :], vsrcs[h][cur, :]], axis=0)
                res.append(jnp.dot(p, vh, preferred_element_type=F32))
                mx.append(m)
            a_dst[cur, :] = jnp.where(head0, res[0], res[1])
            d_dst[cur, :] = pltpu.roll(jnp.where(head0, res[1], res[0]), HEAD_DIM, 1)
            m_dst[cur, :] = jnp.where(head0, mx[0], mx[1])

        def body(i, carry):
            for u in range(unroll):
                b = i * unroll + u
                if unroll % blocks_per_class == 0:
                    first = u % blocks_per_class == 0
                elif blocks_per_class % unroll == 0 and u != 0:
                    first = False
                else:
                    first = jnp.where(jnp.asarray(b % blocks_per_class == 0), 1, 0)
                attend(b, first)
            return carry

        lax.fori_loop(0, nblk_total // unroll, body, 0)

    res1, res4, res16, res_cm = (a1, d1, m1), (a4, d4, m4), (a16, d16, m16), (acm, dcm, mcm)
    run_branch(qn, (k1, va1, vb1), res1, nblk_total)
    for d, qc, kv, res in ((4, q4, (k4, va4, vb4), res4), (16, q16, (k16, va16, vb16), res16)):
        run_branch(qc, kv, res_cm, nblk_total // d)
        ln = seq // d
        for cm, nat in zip(res_cm, res):
            for r in range(d):
                nat[pl.ds(r, ln, stride=d), :] = cm[r * ln:(r + 1) * ln, :]

    ma, mb, mc = m1[...], m4[...], m16[...]
    mx = jnp.maximum(jnp.maximum(ma, mb), mc)
    ea, eb, ec = jnp.exp(ma - mx), jnp.exp(mb - mx), jnp.exp(mc - mx)
    o_ref[...] = ((ea * a1[...] + eb * a4[...] + ec * a16[...])
                  / (ea * d1[...] + eb * d4[...] + ec * d16[...]))


def _rotary_tables(seq):
    half = ROT_DIM // 2
    inv_freq = ROPE_THETA ** (-jnp.arange(0, ROT_DIM, 2, dtype=F32) / ROT_DIM)
    ang = jnp.arange(seq, dtype=F32)[:, None] * inv_freq[None, :]
    cos, sin = jnp.cos(ang), jnp.sin(ang)
    pos = jnp.arange(LANES) % HEAD_DIM
    fidx = pos % half
    in_lo = pos < half
    in_hi = (pos >= half) & (pos < ROT_DIM)
    cos_t = jnp.where((in_lo | in_hi)[None, :], cos[:, fidx], 1.0)
    sa = jnp.where(in_lo[None, :], -sin[:, fidx], 0.0)
    sb = jnp.where(in_hi[None, :], sin[:, fidx], 0.0)
    return cos_t.astype(F32), sa.astype(F32), sb.astype(F32)


def _attention(proj3, conv_w_cols, attn_w_cols):
    b, seq, _ = proj3.shape
    assert seq % (ATTN_BLOCK * DILATIONS[-1]) == 0
    n_pairs = attn_w_cols // LANES
    qoff = 3 * conv_w_cols // LANES
    cos_t, sa, sb = _rotary_tables(seq)
    blk = lambda off: pl.BlockSpec((None, seq, LANES), lambda i, j: (i, 0, off + j))
    tab = pl.BlockSpec((seq, LANES), lambda i, j: (0, 0))
    big = pltpu.VMEM((seq, LANES), F32)
    return pl.pallas_call(
        functools.partial(_attn_kernel, seq=seq),
        grid=(b, n_pairs),
        in_specs=[blk(qoff), blk(qoff + n_pairs), blk(qoff + 2 * n_pairs), tab, tab, tab],
        out_specs=pl.BlockSpec((None, seq, LANES), lambda i, j: (i, 0, j)),
        out_shape=jax.ShapeDtypeStruct((b, seq, attn_w_cols), F32),
        scratch_shapes=([big] * 16 + [pltpu.VMEM((seq, LANES), BF16)] * 9
                        + [pltpu.VMEM((2, ATTN_BLOCK, 2 * ATTN_BLOCK), F32)]),
        compiler_params=pltpu.CompilerParams(
            dimension_semantics=("arbitrary", "arbitrary"),
            vmem_limit_bytes=48 * 1024 * 1024),
        name="attention",
    )(proj3, proj3, proj3, cos_t, sa, sb)


def _out_proj_kernel(cx_ref, cb_ref, cc_ref, hx_ref, hc_ref, ya_ref, x_ref,
                     cw_ref, cg_ref, ag_ref, wo_ref, fg_ref, wrh_ref, wrl_ref, br_ref,
                     x1_ref, h2_ref, idx_ref, gate_ref, rank_ref, cnt_ref, carry,
                     *, tile, n_experts):
    first_tile_of_seq = pl.program_id(1) == 0
    first_step = (pl.program_id(0) == 0) & first_tile_of_seq

    @pl.when(first_step)
    def _():
        carry[...] = jnp.zeros_like(carry)

    u = cc_ref[...] * cx_ref[...]
    uh = jnp.where(first_tile_of_seq, 0.0, hc_ref[...] * hx_ref[...])
    row8 = lax.broadcasted_iota(jnp.int32, (SUBLANES, 1), 0)

    def shifted(k):
        r = pltpu.roll(u, k, 0)
        top = jnp.where(row8 < k, pltpu.roll(uh, k, 0), r[:SUBLANES])
        return jnp.concatenate([top, r[SUBLANES:]], axis=0)

    conv = cw_ref[2:3, :] * u + cw_ref[1:2, :] * shifted(1) + cw_ref[0:1, :] * shifted(2)
    y_conv = cb_ref[...] * conv

    mixed = jnp.concatenate(
        [_rms(y_conv, cg_ref[...]), _rms(ya_ref[...], ag_ref[...])], axis=-1).astype(BF16)
    x1 = x_ref[...] + jnp.dot(mixed, wo_ref[...], preferred_element_type=F32)
    x1_ref[...] = x1
    h2 = _rms(x1, fg_ref[...])
    d = h2.shape[-1]
    chunks = d // LANES
    for c in range(chunks):
        h2_ref[pl.ds(c, tile, stride=chunks), :] = h2[:, c * LANES:(c + 1) * LANES]

    h2_hi = h2.astype(BF16)
    h2_lo = (h2 - h2_hi.astype(F32)).astype(BF16)
    logits = (jnp.dot(h2_hi, wrh_ref[...], preferred_element_type=F32)
              + jnp.dot(h2_hi, wrl_ref[...], preferred_element_type=F32)
              + jnp.dot(h2_lo, wrh_ref[...], preferred_element_type=F32)) + br_ref[...]
    eio = lax.broadcasted_iota(jnp.int32, (tile, n_experts), 1).astype(F32)
    work = logits
    vals, idxs = [], []
    for _ in range(TOP_K):
        m = jnp.max(work, axis=1, keepdims=True)
        ik = jnp.min(jnp.where(work == m, eio, float(n_experts)), axis=1, keepdims=True)
        vals.append(m)
        idxs.append(ik)
        work = jnp.where(eio == ik, -jnp.inf, work)
    exps = [jnp.exp(v - vals[0]) for v in vals]
    tot = exps[0] + exps[1] + exps[2] + exps[3]

    onehot = jnp.zeros((tile, n_experts), F32)
    for ik in idxs:
        onehot = onehot + (eio == ik).astype(F32)
    ri = lax.broadcasted_iota(jnp.int32, (tile, tile), 0)
    ci = lax.broadcasted_iota(jnp.int32, (tile, tile), 1)
    tri = (ci < ri).astype(BF16)
    before = jnp.dot(tri, onehot.astype(BF16), preferred_element_type=F32) + carry[...]
    carry[...] = carry[...] + jnp.sum(onehot, axis=0, keepdims=True)
    cnt_ref[...] = carry[...].astype(jnp.int32)

    lio = lax.broadcasted_iota(jnp.int32, (tile, LANES), 1)
    idx_out = jnp.zeros((tile, LANES), jnp.int32)
    gate_out = jnp.zeros((tile, LANES), F32)
    rank_out = jnp.zeros((tile, LANES), jnp.int32)
    for k in range(TOP_K):
        rk = jnp.sum(jnp.where(eio == idxs[k], before, 0.0), axis=1, keepdims=True)
        idx_out = jnp.where(lio == k, idxs[k].astype(jnp.int32), idx_out)
        gate_out = jnp.where(lio == k, exps[k] / tot, gate_out)
        rank_out = jnp.where(lio == k, rk.astype(jnp.int32), rank_out)
    idx_ref[...] = idx_out
    gate_ref[...] = gate_out
    rank_ref[...] = rank_out


def _out_proj(proj3, y_attn, x3, conv_w, conv_g, attn_g, w_out_bf16, ffn_g, w_router, b_router,
              tile=256):
    b, seq, d = x3.shape
    cw_cols = conv_w.shape[1]
    aw_cols = y_attn.shape[2]
    n_experts = w_router.shape[1]
    w_router_hi = w_router.astype(BF16)
    w_router_lo = (w_router - w_router_hi.astype(F32)).astype(BF16)
    chunks = d // LANES
    tiles = seq // tile
    n = b * seq
    halo_blocks = tile // SUBLANES
    row_blk = lambda width, col: pl.BlockSpec((None, tile, width), lambda i, j: (i, j, col))
    halo = lambda col: pl.BlockSpec(
        (None, SUBLANES, cw_cols), lambda i, j: (i, jnp.maximum(j * halo_blocks - 1, 0), col))
    const = lambda shape: pl.BlockSpec(shape, lambda i, j: (0,) * len(shape))
    flat = lambda width: pl.BlockSpec((tile, width), lambda i, j: (i * tiles + j, 0))
    outs = pl.pallas_call(
        functools.partial(_out_proj_kernel, tile=tile, n_experts=n_experts),
        grid=(b, tiles),
        in_specs=[
            row_blk(cw_cols, 0), row_blk(cw_cols, 1), row_blk(cw_cols, 2), halo(0), halo(2),
            row_blk(aw_cols, 0), row_blk(d, 0),
            const((CONV_K, cw_cols)), const((1, cw_cols)), const((1, aw_cols)),
            const((d, d)), const((1, d)), const((d, n_experts)), const((d, n_experts)),
            const((1, n_experts)),
        ],
        out_specs=[
            row_blk(d, 0),
            pl.BlockSpec((tile * chunks, LANES), lambda i, j: (i * tiles + j, 0)),
            flat(LANES), flat(LANES), flat(LANES),
            const((1, n_experts)),
        ],
        out_shape=[
            jax.ShapeDtypeStruct((b, seq, d), F32),
            jax.ShapeDtypeStruct((n * chunks, LANES), F32),
            jax.ShapeDtypeStruct((n, LANES), jnp.int32),
            jax.ShapeDtypeStruct((n, LANES), F32),
            jax.ShapeDtypeStruct((n, LANES), jnp.int32),
            jax.ShapeDtypeStruct((1, n_experts), jnp.int32),
        ],
        scratch_shapes=[pltpu.VMEM((1, n_experts), F32)],
        compiler_params=pltpu.CompilerParams(
            dimension_semantics=("arbitrary", "arbitrary"),
            vmem_limit_bytes=48 * 1024 * 1024),
        name="out_proj",
    )(proj3, proj3, proj3, proj3, proj3, y_attn, x3, conv_w, conv_g, attn_g, w_out_bf16,
      ffn_g, w_router_hi, w_router_lo, b_router)
    return outs


DMA_ISSUE_UNROLL = 4


def _dispatch_kernel(idx_ref, rank_ref, offs_ref, pad_start, pad_rows, used_blocks, h2_ref, xs_hbm,
                     dest_ref, zbuf, sem, zsem, *, tile, chunks, n_experts, n_row_blocks):
    gran = MOE_ROW_GRAN

    @pl.when(pl.program_id(0) == 0)
    def _():
        zbuf[...] = jnp.zeros_like(zbuf)

        def zero_copy(first_row, n_rows):
            return pltpu.make_async_copy(_token_rows(zbuf, 0, n_rows, chunks),
                                         _token_rows(xs_hbm, first_row, n_rows, chunks), zsem)

        def for_each_zero_run(act):
            def per_expert(e, carry):
                row = pad_start[e]
                left = pad_rows[e]
                for bit in reversed(range(gran.bit_length() - 1)):
                    take = (left >> bit) & 1

                    @pl.when(take == 1)
                    def _():
                        act(zero_copy(row, 1 << bit))

                    row = row + (take << bit)
                return carry

            lax.fori_loop(0, n_experts, per_expert, 0)

            def tail(blk, carry):
                act(zero_copy(blk * gran, gran))
                return carry

            lax.fori_loop(used_blocks[0], n_row_blocks, tail, 0)

        for_each_zero_run(lambda cp: cp.start())
        for_each_zero_run(lambda cp: cp.wait())

    def issue(g, carry):
        for u in range(DMA_ISSUE_UNROLL):
            t = g * DMA_ISSUE_UNROLL + u
            for k in range(TOP_K):
                a = t * TOP_K + k
                dest = offs_ref[idx_ref[0, 0, a]] + rank_ref[0, 0, a]
                dest_ref[0, 0, a] = dest
                pltpu.make_async_copy(_token_rows(h2_ref, t, 1, chunks),
                                      _token_rows(xs_hbm, dest, 1, chunks), sem).start()
        return carry

    lax.fori_loop(0, tile // DMA_ISSUE_UNROLL, issue, 0)
    counted = _token_rows(xs_hbm, 0, tile * TOP_K, chunks)
    pltpu.make_async_copy(counted, counted, sem).wait()


def _dispatch(idx_flat, rank_flat, offs, pad_start, pad_rows, used_blocks, h2_tm, n_rows, tile=256):
    steps = idx_flat.shape[0] // (tile * TOP_K)
    per_tile = lambda a: a.reshape(steps, 1, tile * TOP_K)
    chunks = h2_tm.shape[0] // (idx_flat.shape[0] // TOP_K)
    gran = MOE_ROW_GRAN
    smem = pl.BlockSpec(memory_space=pltpu.SMEM)
    tile_smem = pl.BlockSpec((1, 1, tile * TOP_K), lambda i: (i, 0, 0), memory_space=pltpu.SMEM)
    xs, dest = pl.pallas_call(
        functools.partial(_dispatch_kernel, tile=tile, chunks=chunks,
                          n_experts=pad_start.shape[0], n_row_blocks=n_rows // gran),
        grid=(steps,),
        in_specs=[
            tile_smem, tile_smem, smem, smem, smem, smem,
            pl.BlockSpec((tile * chunks, LANES), lambda i: (i, 0)),
        ],
        out_specs=[pl.BlockSpec(memory_space=pl.ANY), tile_smem],
        out_shape=[jax.ShapeDtypeStruct((n_rows * chunks, LANES), F32),
                   jax.ShapeDtypeStruct((steps, 1, tile * TOP_K), jnp.int32)],
        scratch_shapes=[pltpu.VMEM((gran * chunks, LANES), F32),
                        pltpu.SemaphoreType.DMA(()), pltpu.SemaphoreType.DMA(())],
        compiler_params=pltpu.CompilerParams(dimension_semantics=("arbitrary",)),
        name="dispatch",
    )(per_tile(idx_flat), per_tile(rank_flat), offs, pad_start, pad_rows, used_blocks, h2_tm)
    return xs, dest.reshape(-1)


MOE_WEIGHT_SLOTS = 4


def _moe_kernel(item_e, item_row0, item_nblk, used_blocks, live_items, xs_hbm, wgu_hbm, wdn_hbm,
                bgu_ref, bd_ref, ys_hbm, iobuf, xb, yacc, wgf, wuf, wdf, wgb, wub, wdb,
                sem_io, sem_w, *, chunks, n_col_steps, n_row_blocks):
    s = pl.program_id(0)
    nb = item_nblk[s]
    row0 = item_row0[s]
    gran = MOE_ROW_GRAN
    tn = MOE_COL_TILE
    de = n_col_steps * tn
    total_chunks = live_items[0] * n_col_steps

    def staged(slot):
        return iobuf.at[slot]

    def ys_block(blk):
        return _token_rows(ys_hbm, blk * gran, gran, chunks)

    def weight_copies(g):
        e = item_e[g // n_col_steps]
        col = pl.multiple_of((g % n_col_steps) * tn, tn)
        slot = g % MOE_WEIGHT_SLOTS
        return (
            pltpu.make_async_copy(wgu_hbm.at[e, :, pl.ds(col, tn)], wgf.at[slot], sem_w.at[slot]),
            pltpu.make_async_copy(wgu_hbm.at[e, :, pl.ds(pl.multiple_of(de + col, tn), tn)],
                                  wuf.at[slot], sem_w.at[slot]),
            pltpu.make_async_copy(wdn_hbm.at[e, pl.ds(col, tn), :], wdf.at[slot], sem_w.at[slot]),
        )

    def fetch(g):
        @pl.when(g < total_chunks)
        def _():
            for cp in weight_copies(g):
                cp.start()

    @pl.when(s == 0)
    def _():
        for g in range(MOE_WEIGHT_SLOTS - 1):
            fetch(g)

    @pl.when(nb > 0)
    def _():
        def io_slot(sb):
            return sb % MOE_IO_SLOTS

        def x_copy(sb):
            return pltpu.make_async_copy(
                _token_rows(xs_hbm, row0 + sb * gran, gran, chunks), staged(io_slot(sb)),
                sem_io.at[io_slot(sb)])

        def y_copy(sb):
            return pltpu.make_async_copy(staged(io_slot(sb)), ys_block(row0 // gran + sb),
                                         sem_io.at[io_slot(sb)])

        for sb in range(MOE_IO_SLOTS):
            @pl.when(sb < nb)
            def _():
                x_copy(sb).start()

        def block_rows(sb):
            return pl.ds(pl.multiple_of(sb * gran, gran), gran)

        def column_step(j, phase):
            g = s * n_col_steps + j
            slot = g % MOE_WEIGHT_SLOTS
            for cp in weight_copies(g):
                cp.wait()
            fetch(g + MOE_WEIGHT_SLOTS - 1)
            col = pl.multiple_of(j * tn, tn)
            bg = bgu_ref[:, pl.ds(col, tn)]
            bu = bgu_ref[:, pl.ds(pl.multiple_of(de + col, tn), tn)]

            def cast_weights():
                wgb[...] = wgf[slot].astype(BF16)
                wub[...] = wuf[slot].astype(BF16)
                wdb[...] = wdf[slot].astype(BF16)

            def partial_out(sb):
                x = xb[block_rows(sb), :]
                gate = jnp.dot(x, wgb[...], preferred_element_type=F32) + bg
                up = jnp.dot(x, wub[...], preferred_element_type=F32) + bu
                gate = jnp.minimum(gate, SWIGLU_LIMIT)
                up = jnp.clip(up, -SWIGLU_LIMIT, SWIGLU_LIMIT)
                act = (up + 1.0) * (gate * jax.nn.sigmoid(SWIGLU_ALPHA * gate))
                return jnp.dot(act.astype(BF16), wdb[...], preferred_element_type=F32)

            def trip(first, count):
                blocks = [first + u for u in range(count)]
                if phase == "first":
                    for sb in blocks:
                        x_copy(sb).wait()
                if phase == "last":
                    for sb in blocks:
                        @pl.when(sb >= MOE_IO_SLOTS)
                        def _():
                            y_copy(sb - MOE_IO_SLOTS).wait()
                parts = []
                for sb in blocks:
                    if phase == "first":
                        for c in range(chunks):
                            xb[block_rows(sb), c * LANES:(c + 1) * LANES] = (
                                iobuf[io_slot(sb), pl.ds(c, gran, stride=chunks), :].astype(BF16))
                    parts.append(partial_out(sb))
                for sb, part in zip(blocks, parts):
                    if phase == "first":
                        yacc[block_rows(sb), :] = part + bd_ref[...]
                    elif phase == "middle":
                        yacc[block_rows(sb), :] += part
                    else:
                        y = yacc[block_rows(sb), :] + part
                        for c in range(chunks):
                            iobuf[io_slot(sb), pl.ds(c, gran, stride=chunks), :] = (
                                y[:, c * LANES:(c + 1) * LANES])
                for sb in blocks:
                    if phase == "first":
                        @pl.when(sb + MOE_IO_SLOTS < nb)
                        def _():
                            x_copy(sb + MOE_IO_SLOTS).start()
                    if phase == "last":
                        y_copy(sb).start()

            full = MOE_TRIP_BLOCKS

            def tail(left):
                count = full // 2
                while count >= 1:
                    @pl.when(left % (2 * count) >= count)
                    def _():
                        trip(nb - left % (2 * count), count)

                    count //= 2

            @pl.when(nb >= full)
            def _():
                cast_weights()
                trip(0, full)

                def body(i, c):
                    trip(i * full, full)
                    return c

                lax.fori_loop(1, nb // full, body, 0)
                tail(nb % full)

            @pl.when(nb < full)
            def _():
                cast_weights()
                tail(nb)

        column_step(0, "first")

        def middle(j, carry):
            column_step(j, "middle")
            return carry

        lax.fori_loop(1, n_col_steps - 1, middle, 0)
        column_step(n_col_steps - 1, "last")

        for k in range(MOE_IO_SLOTS):
            @pl.when(nb > k)
            def _():
                y_copy(nb - 1 - k).wait()

    @pl.when(s == pl.num_programs(0) - 1)
    def _():
        iobuf[0] = jnp.zeros(iobuf.shape[1:], iobuf.dtype)

        def fill(blk, carry):
            cp = pltpu.make_async_copy(staged(0), ys_block(blk), sem_io.at[0])
            cp.start()
            cp.wait()
            return carry

        lax.fori_loop(used_blocks[0], n_row_blocks, fill, 0)


def _moe(item_e, item_row0, item_nblk, used_blocks, live_items, xs, w_gate_up, b_gate_up, w_down,
         b_down, n_rows):
    n_experts, d, two_de = w_gate_up.shape
    de = two_de // 2
    chunks = d // LANES
    tn = MOE_COL_TILE
    n_col_steps = de // tn
    assert n_col_steps >= 2
    n_items = item_e.shape[0]
    gran = MOE_ROW_GRAN
    slots = MOE_WEIGHT_SLOTS

    bgu3 = b_gate_up.reshape(n_experts, 1, two_de)
    bd3 = b_down.reshape(n_experts, 1, d)
    per_expert = lambda width: pl.BlockSpec((None, 1, width), lambda s, e, *_: (e[s], 0, 0))
    grid_spec = pltpu.PrefetchScalarGridSpec(
        num_scalar_prefetch=5,
        grid=(n_items,),
        in_specs=[
            pl.BlockSpec(memory_space=pl.ANY),
            pl.BlockSpec(memory_space=pl.ANY),
            pl.BlockSpec(memory_space=pl.ANY),
            per_expert(two_de),
            per_expert(d),
        ],
        out_specs=pl.BlockSpec(memory_space=pl.ANY),
        scratch_shapes=[
            pltpu.VMEM((MOE_IO_SLOTS, gran * chunks, LANES), F32),
            pltpu.VMEM((MOE_ITEM_ROWS, d), BF16),
            pltpu.VMEM((MOE_ITEM_ROWS, d), F32),
            pltpu.VMEM((slots, d, tn), F32),
            pltpu.VMEM((slots, d, tn), F32),
            pltpu.VMEM((slots, tn, d), F32),
            pltpu.VMEM((d, tn), BF16),
            pltpu.VMEM((d, tn), BF16),
            pltpu.VMEM((tn, d), BF16),
            pltpu.SemaphoreType.DMA((MOE_IO_SLOTS,)),
            pltpu.SemaphoreType.DMA((slots,)),
        ],
    )
    return pl.pallas_call(
        functools.partial(_moe_kernel, chunks=chunks, n_col_steps=n_col_steps,
                          n_row_blocks=n_rows // gran),
        grid_spec=grid_spec,
        out_shape=jax.ShapeDtypeStruct((n_rows * chunks, LANES), F32),
        compiler_params=pltpu.CompilerParams(
            dimension_semantics=("arbitrary",),
            vmem_limit_bytes=56 * 1024 * 1024),
        name="moe",
    )(item_e, item_row0, item_nblk, used_blocks, live_items, xs, w_gate_up, w_down, bgu3, bd3)


def _combine_kernel(dest_ref, ys_hbm, x1_ref, gate_ref, fg_ref, o_ref, gbuf, sem,
                    *, tile, chunks, final_norm):
    i = pl.program_id(0)
    slot = i % 2
    per_tile = tile * TOP_K
    stride = _token_row_stride(chunks)

    def gather(step, dst_slot):
        base = step * per_tile

        def issue(g, carry):
            for u in range(DMA_ISSUE_UNROLL):
                t = g * DMA_ISSUE_UNROLL + u
                for k in range(TOP_K):
                    pltpu.make_async_copy(
                        _token_rows(ys_hbm, dest_ref[base + t * TOP_K + k], 1, chunks),
                        gbuf.at[dst_slot, pl.ds(pl.multiple_of((k * tile + t) * stride, SUBLANES),
                                                chunks), :],
                        sem.at[dst_slot]).start()
            return carry

        lax.fori_loop(0, tile // DMA_ISSUE_UNROLL, issue, 0)

    @pl.when(i == 0)
    def _():
        gather(0, 0)

    @pl.when(i + 1 < pl.num_programs(0))
    def _():
        gather(i + 1, 1 - slot)

    counted = gbuf.at[slot, pl.ds(0, per_tile * chunks), :]
    pltpu.make_async_copy(counted, counted, sem.at[slot]).wait()

    gates = gate_ref[...]
    gk = [gates[:, k:k + 1] for k in range(TOP_K)]
    cols = []
    for c in range(chunks):
        acc = x1_ref[:, c * LANES:(c + 1) * LANES]
        for k in range(TOP_K):
            acc = acc + gk[k] * gbuf[slot, pl.ds(k * tile * stride + c, tile, stride=stride), :]
        cols.append(acc)
    out = jnp.concatenate(cols, axis=-1)
    if final_norm:
        out = _rms(out, fg_ref[...])
    o_ref[...] = out


def _combine(dest, ys, x1, gates, final_g, final_norm, tile=128):
    n, d = x1.shape
    chunks = d // LANES
    steps = n // tile
    grid_spec = pltpu.PrefetchScalarGridSpec(
        num_scalar_prefetch=1,
        grid=(steps,),
        in_specs=[
            pl.BlockSpec(memory_space=pl.ANY),
            pl.BlockSpec((tile, d), lambda i, *_: (i, 0)),
            pl.BlockSpec((tile, LANES), lambda i, *_: (i, 0)),
            pl.BlockSpec((1, d), lambda i, *_: (0, 0)),
        ],
        out_specs=pl.BlockSpec((tile, d), lambda i, *_: (i, 0)),
        scratch_shapes=[pltpu.VMEM((2, TOP_K * tile * _token_row_stride(chunks), LANES), F32),
                        pltpu.SemaphoreType.DMA((2,))],
    )
    return pl.pallas_call(
        functools.partial(_combine_kernel, tile=tile, chunks=chunks, final_norm=final_norm),
        grid_spec=grid_spec,
        out_shape=jax.ShapeDtypeStruct((n, d), F32),
        compiler_params=pltpu.CompilerParams(dimension_semantics=("arbitrary",),
                                             vmem_limit_bytes=40 * 1024 * 1024),
        name="combine",
    )(dest, ys, x1, gates, final_g)


def _routing_tables(counts, n_rows_cap):
    n_experts = counts.shape[0]
    gran, item_rows = MOE_ROW_GRAN, MOE_ITEM_ROWS
    padded = ((counts + gran - 1) // gran) * gran
    ends = jnp.cumsum(padded)
    offs = ends - padded
    items_per = (padded + item_rows - 1) // item_rows
    item_ends = jnp.cumsum(items_per)
    item_starts = item_ends - items_per
    n_items = n_experts + n_rows_cap // item_rows
    slot = jnp.arange(n_items, dtype=jnp.int32)
    total = item_ends[-1]
    live = slot < total
    live_slot = jnp.minimum(slot, total - 1)
    e_of = jnp.minimum(jnp.sum((live_slot[:, None] >= item_ends[None, :]).astype(jnp.int32), axis=1),
                       n_experts - 1).astype(jnp.int32)
    local = slot - item_starts[e_of]
    row0 = offs[e_of] + local * item_rows
    rows = jnp.clip(padded[e_of] - local * item_rows, 0, item_rows)
    nblk = jnp.where(live, rows // gran, 0).astype(jnp.int32)
    row0 = jnp.where(live, row0, 0).astype(jnp.int32)
    used_blocks = (ends[-1:] // gran).astype(jnp.int32)
    pad_start = (offs + counts).astype(jnp.int32)
    pad_rows = (padded - counts).astype(jnp.int32)
    live_items = total.reshape(1).astype(jnp.int32)
    return (offs.astype(jnp.int32), e_of, row0, nblk, used_blocks, live_items, pad_start,
            pad_rows)


def _layer(x3, mix_g, w_in, conv_w, conv_g, attn_g, w_out, ffn_g, w_router, b_router,
           w_gate_up, b_gate_up, w_down, b_down, final_g, final_norm):
    b, seq, d = x3.shape
    n = b * seq
    cw_cols = conv_w.shape[1]
    aw_cols = attn_g.shape[0]
    n_experts = w_router.shape[1]
    chunks = d // LANES

    proj = _in_proj(x3.reshape(n, d), mix_g.reshape(1, d), w_in.astype(BF16))
    proj3 = proj.reshape(b, seq, -1)
    y_attn = _attention(proj3, cw_cols, aw_cols)
    x1, h2_tm, idx, gates, rank, counts = _out_proj(
        proj3, y_attn, x3, conv_w, conv_g.reshape(1, -1), attn_g.reshape(1, -1),
        w_out.astype(BF16), ffn_g.reshape(1, d), w_router, b_router.reshape(1, -1))

    n_rows = n * TOP_K + n_experts * MOE_ROW_GRAN
    (offs, item_e, item_row0, item_nblk, used_blocks, live_items, pad_start,
     pad_rows) = _routing_tables(counts[0], n_rows)
    idx_flat = idx[:, :TOP_K].reshape(n * TOP_K)
    rank_flat = rank[:, :TOP_K].reshape(n * TOP_K)
    xs, dest = _dispatch(idx_flat, rank_flat, offs, pad_start, pad_rows, used_blocks, h2_tm,
                         n_rows)
    ys = _moe(item_e, item_row0, item_nblk, used_blocks, live_items, xs, w_gate_up, b_gate_up,
              w_down, b_down, n_rows)
    out = _combine(dest, ys, x1.reshape(n, d), gates, final_g.reshape(1, d), final_norm)
    return out.reshape(b, seq, d)


def kernel(x, mix_norm_g, w_in, conv_w, conv_norm_g, attn_norm_g, w_out, ffn_norm_g, w_router,
           b_router, w_gate_up, b_gate_up, w_down, b_down, final_norm_g):
    depth = w_in.shape[0]
    for layer in range(depth):
        x = _layer(x, mix_norm_g[layer], w_in[layer], conv_w[layer], conv_norm_g[layer],
                   attn_norm_g[layer], w_out[layer], ffn_norm_g[layer], w_router[layer],
                   b_router[layer], w_gate_up[layer], b_gate_up[layer], w_down[layer],
                   b_down[layer], final_norm_g, layer == depth - 1)
    return x
```

```python
import functools

import jax
import jax.numpy as jnp
from jax import lax
from jax.experimental import pallas as pl
from jax.experimental.pallas import tpu as pltpu

HEAD_DIM = 64
CONV_K = 3
ROT_DIM = HEAD_DIM // 4
ROPE_THETA = 500000.0
DILATIONS = (1, 4, 16)
ATTN_BLOCK = 128
TOP_K = 4
SWIGLU_LIMIT = 7.0
SWIGLU_ALPHA = 1.702
EPS = 1e-5

LANES = 128
SUBLANES = 8

MOE_ROW_GRAN = 256
MOE_ITEM_ROWS = 1536
MOE_COL_TILE = 256
MOE_TRIP_BLOCKS = 2
MOE_IO_SLOTS = 2 * MOE_TRIP_BLOCKS
BF16 = jnp.bfloat16
F32 = jnp.float32


def _token_row_stride(chunks):
    return chunks + SUBLANES


def _token_rows(ref, first_token, n_tokens, chunks):
    return ref.at[pl.ds(pl.multiple_of(first_token * chunks, chunks), n_tokens * chunks), :]


def _rms(x, g):
    return x * lax.rsqrt(jnp.mean(x * x, axis=-1, keepdims=True) + EPS) * g


def _in_proj_kernel(x_ref, g_ref, w_ref, o_ref, xn_ref):
    @pl.when(pl.program_id(1) == 0)
    def _():
        xn_ref[...] = _rms(x_ref[...], g_ref[...]).astype(BF16)

    o_ref[...] = jnp.dot(xn_ref[...], w_ref[...], preferred_element_type=F32)


def _in_proj(x2, g, w_bf16, tm=1024):
    n, d = x2.shape
    cols = w_bf16.shape[1]
    tn = 1024 if cols % 1024 == 0 else 512
    return pl.pallas_call(
        _in_proj_kernel,
        grid=(n // tm, cols // tn),
        in_specs=[
            pl.BlockSpec((tm, d), lambda i, j: (i, 0)),
            pl.BlockSpec((1, d), lambda i, j: (0, 0)),
            pl.BlockSpec((d, tn), lambda i, j: (0, j)),
        ],
        out_specs=pl.BlockSpec((tm, tn), lambda i, j: (i, j)),
        out_shape=jax.ShapeDtypeStruct((n, cols), F32),
        scratch_shapes=[pltpu.VMEM((tm, d), BF16)],
        compiler_params=pltpu.CompilerParams(
            dimension_semantics=("arbitrary", "arbitrary"),
            vmem_limit_bytes=48 * 1024 * 1024),
        name="in_proj",
    )(x2, g, w_bf16)


def _attn_kernel(q_ref, k_ref, v_ref, cos_ref, sa_ref, sb_ref, o_ref,
                 qn, kn, q4, q16, a1, d1, m1, a4, d4, m4, a16, d16, m16, acm, dcm, mcm,
                 k1, va1, vb1, k4, va4, vb4, k16, va16, vb16, bias_s, *, seq):
    nblk_total = seq // ATTN_BLOCK
    lane = lax.broadcasted_iota(jnp.int32, (1, LANES), 1)
    head0 = lane < HEAD_DIM

    qi = lax.broadcasted_iota(jnp.int32, (ATTN_BLOCK, 2 * ATTN_BLOCK), 0)
    kj = lax.broadcasted_iota(jnp.int32, (ATTN_BLOCK, 2 * ATTN_BLOCK), 1)
    band = (kj >= qi) & (kj <= qi + ATTN_BLOCK)
    neg = jnp.float32(-jnp.inf)
    bias_s[0] = jnp.where(band, 0.0, neg)
    bias_s[1] = jnp.where(band & (kj >= ATTN_BLOCK), 0.0, neg)

    def rot(x):
        return (x * cos_ref[...] + pltpu.roll(x, LANES - ROT_DIM // 2, 1) * sa_ref[...]
                + pltpu.roll(x, ROT_DIM // 2, 1) * sb_ref[...])

    qn[...] = rot(q_ref[...]) * (HEAD_DIM ** -0.5)
    kn[...] = rot(k_ref[...])

    def put_kv(dst_rows, kc, vac, vbc, k, v):
        kc[dst_rows, :] = k.astype(BF16)
        vac[dst_rows, :] = jnp.where(head0, v, 1.0).astype(BF16)
        vbc[dst_rows, :] = jnp.where(head0, 1.0, v).astype(BF16)

    put_kv(slice(None), k1, va1, vb1, kn[...], v_ref[...])
    for d, qc, kv in ((4, q4, (k4, va4, vb4)), (16, q16, (k16, va16, vb16))):
        ln = seq // d
        for r in range(d):
            rows = slice(r * ln, (r + 1) * ln)
            qc[rows, :] = qn[pl.ds(r, ln, stride=d), :]
            put_kv(rows, *kv, kn[pl.ds(r, ln, stride=d), :], v_ref[pl.ds(r, ln, stride=d), :])

    unroll = 16

    def run_branch(qsrc, kv, dsts, blocks_per_class):
        ksrc, vsrcs = kv[0], kv[1:]
        a_dst, d_dst, m_dst = dsts

        def attend(b, first):
            cur = pl.ds(pl.multiple_of(b * ATTN_BLOCK, ATTN_BLOCK), ATTN_BLOCK)
            qb = qsrc[cur, :]
            prv = pl.ds(pl.multiple_of(jnp.maximum(b - 1, 0) * ATTN_BLOCK, ATTN_BLOCK),
                        ATTN_BLOCK)
            bias = bias_s[int(first)] if isinstance(first, bool) else bias_s[first]
            kk = jnp.concatenate([ksrc[prv, :], ksrc[cur, :]], axis=0)
            res, mx = [], []
            for h in range(2):
                mine = head0 if h == 0 else jnp.logical_not(head0)
                qh = jnp.where(mine, qb, 0.0).astype(BF16)
                s = lax.dot_general(qh, kk, (((1,), (1,)), ((), ())), preferred_element_type=F32)
                s = s + bias
                m = jnp.max(s, axis=-1, keepdims=True)
                p = jnp.exp(s - m).astype(BF16)
                vh = jnp.concatenate([vsrcs[h][prv, :], vsrcs[h][cur, :]], axis=0)
                res.append(jnp.dot(p, vh, preferred_element_type=F32))
                mx.append(m)
            a_dst[cur, :] = jnp.where(head0, res[0], res[1])
            d_dst[cur, :] = pltpu.roll(jnp.where(head0, res[1], res[0]), HEAD_DIM, 1)
            m_dst[cur, :] = jnp.where(head0, mx[0], mx[1])

        def body(i, carry):
            for u in range(unroll):
                b = i * unroll + u
                if unroll % blocks_per_class == 0:
                    first = u % blocks_per_class == 0
                elif blocks_per_class % unroll == 0 and u != 0:
                    first = False
                else:
                    first = jnp.where(jnp.asarray(b % blocks_per_class == 0), 1, 0)
                attend(b, first)
            return carry

        lax.fori_loop(0, nblk_total // unroll, body, 0)

    res1, res4, res16, res_cm = (a1, d1, m1), (a4, d4, m4), (a16, d16, m16), (acm, dcm, mcm)
    run_branch(qn, (k1, va1, vb1), res1, nblk_total)
    for d, qc, kv, res in ((4, q4, (k4, va4, vb4), res4), (16, q16, (k16, va16, vb16), res16)):
        run_branch(qc, kv, res_cm, nblk_total // d)
        ln = seq // d
        for cm, nat in zip(res_cm, res):
            for r in range(d):
                nat[pl.ds(r, ln, stride=d), :] = cm[r * ln:(r + 1) * ln, :]

    ma, mb, mc = m1[...], m4[...], m16[...]
    mx = jnp.maximum(jnp.maximum(ma, mb), mc)
    ea, eb, ec = jnp.exp(ma - mx), jnp.exp(mb - mx), jnp.exp(mc - mx)
    o_ref[...] = ((ea * a1[...] + eb * a4[...] + ec * a16[...])
                  / (ea * d1[...] + eb * d4[...] + ec * d16[...]))


def _rotary_tables(seq):
    half = ROT_DIM // 2
    inv_freq = ROPE_THETA ** (-jnp.arange(0, ROT_DIM, 2, dtype=F32) / ROT_DIM)
    ang = jnp.arange(seq, dtype=F32)[:, None] * inv_freq[None, :]
    cos, sin = jnp.cos(ang), jnp.sin(ang)
    pos = jnp.arange(LANES) % HEAD_DIM
    fidx = pos % half
    in_lo = pos < half
    in_hi = (pos >= half) & (pos < ROT_DIM)
    cos_t = jnp.where((in_lo | in_hi)[None, :], cos[:, fidx], 1.0)
    sa = jnp.where(in_lo[None, :], -sin[:, fidx], 0.0)
    sb = jnp.where(in_hi[None, :], sin[:, fidx], 0.0)
    return cos_t.astype(F32), sa.astype(F32), sb.astype(F32)


def _attention(proj3, conv_w_cols, attn_w_cols):
    b, seq, _ = proj3.shape
    assert seq % (ATTN_BLOCK * DILATIONS[-1]) == 0
    n_pairs = attn_w_cols // LANES
    qoff = 3 * conv_w_cols // LANES
    cos_t, sa, sb = _rotary_tables(seq)
    blk = lambda off: pl.BlockSpec((None, seq, LANES), lambda i, j: (i, 0, off + j))
    tab = pl.BlockSpec((seq, LANES), lambda i, j: (0, 0))
    big = pltpu.VMEM((seq, LANES), F32)
    return pl.pallas_call(
        functools.partial(_attn_kernel, seq=seq),
        grid=(b, n_pairs),
        in_specs=[blk(qoff), blk(qoff + n_pairs), blk(qoff + 2 * n_pairs), tab, tab, tab],
        out_specs=pl.BlockSpec((None, seq, LANES), lambda i, j: (i, 0, j)),
        out_shape=jax.ShapeDtypeStruct((b, seq, attn_w_cols), F32),
        scratch_shapes=([big] * 16 + [pltpu.VMEM((seq, LANES), BF16)] * 9
                        + [pltpu.VMEM((2, ATTN_BLOCK, 2 * ATTN_BLOCK), F32)]),
        compiler_params=pltpu.CompilerParams(
            dimension_semantics=("arbitrary", "arbitrary"),
            vmem_limit_bytes=48 * 1024 * 1024),
        name="attention",
    )(proj3, proj3, proj3, cos_t, sa, sb)


def _out_proj_kernel(cx_ref, cb_ref, cc_ref, hx_ref, hc_ref, ya_ref, x_ref,
                     cw_ref, cg_ref, ag_ref, wo_ref, fg_ref, wrh_ref, wrl_ref, br_ref,
                     x1_ref, h2_ref, idx_ref, gate_ref, rank_ref, cnt_ref, carry,
                     *, tile, n_experts):
    first_tile_of_seq = pl.program_id(1) == 0
    first_step = (pl.program_id(0) == 0) & first_tile_of_seq

    @pl.when(first_step)
    def _():
        carry[...] = jnp.zeros_like(carry)

    u = cc_ref[...] * cx_ref[...]
    uh = jnp.where(first_tile_of_seq, 0.0, hc_ref[...] * hx_ref[...])
    row8 = lax.broadcasted_iota(jnp.int32, (SUBLANES, 1), 0)

    def shifted(k):
        r = pltpu.roll(u, k, 0)
        top = jnp.where(row8 < k, pltpu.roll(uh, k, 0), r[:SUBLANES])
        return jnp.concatenate([top, r[SUBLANES:]], axis=0)

    conv = cw_ref[2:3, :] * u + cw_ref[1:2, :] * shifted(1) + cw_ref[0:1, :] * shifted(2)
    y_conv = cb_ref[...] * conv

    mixed = jnp.concatenate(
        [_rms(y_conv, cg_ref[...]), _rms(ya_ref[...], ag_ref[...])], axis=-1).astype(BF16)
    x1 = x_ref[...] + jnp.dot(mixed, wo_ref[...], preferred_element_type=F32)
    x1_ref[...] = x1
    h2 = _rms(x1, fg_ref[...])
    d = h2.shape[-1]
    chunks = d // LANES
    for c in range(chunks):
        h2_ref[pl.ds(c, tile, stride=chunks), :] = h2[:, c * LANES:(c + 1) * LANES]

    h2_hi = h2.astype(BF16)
    h2_lo = (h2 - h2_hi.astype(F32)).astype(BF16)
    logits = (jnp.dot(h2_hi, wrh_ref[...], preferred_element_type=F32)
              + jnp.dot(h2_hi, wrl_ref[...], preferred_element_type=F32)
              + jnp.dot(h2_lo, wrh_ref[...], preferred_element_type=F32)) + br_ref[...]
    eio = lax.broadcasted_iota(jnp.int32, (tile, n_experts), 1).astype(F32)
    work = logits
    vals, idxs = [], []
    for _ in range(TOP_K):
        m = jnp.max(work, axis=1, keepdims=True)
        ik = jnp.min(jnp.where(work == m, eio, float(n_experts)), axis=1, keepdims=True)
        vals.append(m)
        idxs.append(ik)
        work = jnp.where(eio == ik, -jnp.inf, work)
    exps = [jnp.exp(v - vals[0]) for v in vals]
    tot = exps[0] + exps[1] + exps[2] + exps[3]

    onehot = jnp.zeros((tile, n_experts), F32)
    for ik in idxs:
        onehot = onehot + (eio == ik).astype(F32)
    ri = lax.broadcasted_iota(jnp.int32, (tile, tile), 0)
    ci = lax.broadcasted_iota(jnp.int32, (tile, tile), 1)
    tri = (ci < ri).astype(BF16)
    before = jnp.dot(tri, onehot.astype(BF16), preferred_element_type=F32) + carry[...]
    carry[...] = carry[...] + jnp.sum(onehot, axis=0, keepdims=True)
    cnt_ref[...] = carry[...].astype(jnp.int32)

    lio = lax.broadcasted_iota(jnp.int32, (tile, LANES), 1)
    idx_out = jnp.zeros((tile, LANES), jnp.int32)
    gate_out = jnp.zeros((tile, LANES), F32)
    rank_out = jnp.zeros((tile, LANES), jnp.int32)
    for k in range(TOP_K):
        rk = jnp.sum(jnp.where(eio == idxs[k], before, 0.0), axis=1, keepdims=True)
        idx_out = jnp.where(lio == k, idxs[k].astype(jnp.int32), idx_out)
        gate_out = jnp.where(lio == k, exps[k] / tot, gate_out)
        rank_out = jnp.where(lio == k, rk.astype(jnp.int32), rank_out)
    idx_ref[...] = idx_out
    gate_ref[...] = gate_out
    rank_ref[...] = rank_out


def _out_proj(proj3, y_attn, x3, conv_w, conv_g, attn_g, w_out_bf16, ffn_g, w_router, b_router,
              tile=256):
    b, seq, d = x3.shape
    cw_cols = conv_w.shape[1]
    aw_cols = y_attn.shape[2]
    n_experts = w_router.shape[1]
    w_router_hi = w_router.astype(BF16)
    w_router_lo = (w_router - w_router_hi.astype(F32)).astype(BF16)
    chunks = d // LANES
    tiles = seq // tile
    n = b * seq
    halo_blocks = tile // SUBLANES
    row_blk = lambda width, col: pl.BlockSpec((None, tile, width), lambda i, j: (i, j, col))
    halo = lambda col: pl.BlockSpec(
        (None, SUBLANES, cw_cols), lambda i, j: (i, jnp.maximum(j * halo_blocks - 1, 0), col))
    const = lambda shape: pl.BlockSpec(shape, lambda i, j: (0,) * len(shape))
    flat = lambda width: pl.BlockSpec((tile, width), lambda i, j: (i * tiles + j, 0))
    outs = pl.pallas_call(
        functools.partial(_out_proj_kernel, tile=tile, n_experts=n_experts),
        grid=(b, tiles),
        in_specs=[
            row_blk(cw_cols, 0), row_blk(cw_cols, 1), row_blk(cw_cols, 2), halo(0), halo(2),
            row_blk(aw_cols, 0), row_blk(d, 0),
            const((CONV_K, cw_cols)), const((1, cw_cols)), const((1, aw_cols)),
            const((d, d)), const((1, d)), const((d, n_experts)), const((d, n_experts)),
            const((1, n_experts)),
        ],
        out_specs=[
            row_blk(d, 0),
            pl.BlockSpec((tile * chunks, LANES), lambda i, j: (i * tiles + j, 0)),
            flat(LANES), flat(LANES), flat(LANES),
            const((1, n_experts)),
        ],
        out_shape=[
            jax.ShapeDtypeStruct((b, seq, d), F32),
            jax.ShapeDtypeStruct((n * chunks, LANES), F32),
            jax.ShapeDtypeStruct((n, LANES), jnp.int32),
            jax.ShapeDtypeStruct((n, LANES), F32),
            jax.ShapeDtypeStruct((n, LANES), jnp.int32),
            jax.ShapeDtypeStruct((1, n_experts), jnp.int32),
        ],
        scratch_shapes=[pltpu.VMEM((1, n_experts), F32)],
        compiler_params=pltpu.CompilerParams(
            dimension_semantics=("arbitrary", "arbitrary"),
            vmem_limit_bytes=48 * 1024 * 1024),
        name="out_proj",
    )(proj3, proj3, proj3, proj3, proj3, y_attn, x3, conv_w, conv_g, attn_g, w_out_bf16,
      ffn_g, w_router_hi, w_router_lo, b_router)
    return outs


DMA_ISSUE_UNROLL = 4


def _dispatch_kernel(idx_ref, rank_ref, offs_ref, pad_start, pad_rows, used_blocks, h2_ref, xs_hbm,
                     dest_ref, zbuf, sem, zsem, *, tile, chunks, n_experts, n_row_blocks):
    gran = MOE_ROW_GRAN

    @pl.when(pl.program_id(0) == 0)
    def _():
        zbuf[...] = jnp.zeros_like(zbuf)

        def zero_copy(first_row, n_rows):
            return pltpu.make_async_copy(_token_rows(zbuf, 0, n_rows, chunks),
                                         _token_rows(xs_hbm, first_row, n_rows, chunks), zsem)

        def for_each_zero_run(act):
            def per_expert(e, carry):
                row = pad_start[e]
                left = pad_rows[e]
                for bit in reversed(range(gran.bit_length() - 1)):
                    take = (left >> bit) & 1

                    @pl.when(take == 1)
                    def _():
                        act(zero_copy(row, 1 << bit))

                    row = row + (take << bit)
                return carry

            lax.fori_loop(0, n_experts, per_expert, 0)

            def tail(blk, carry):
                act(zero_copy(blk * gran, gran))
                return carry

            lax.fori_loop(used_blocks[0], n_row_blocks, tail, 0)

        for_each_zero_run(lambda cp: cp.start())
        for_each_zero_run(lambda cp: cp.wait())

    def issue(g, carry):
        for u in range(DMA_ISSUE_UNROLL):
            t = g * DMA_ISSUE_UNROLL + u
            for k in range(TOP_K):
                a = t * TOP_K + k
                dest = offs_ref[idx_ref[0, 0, a]] + rank_ref[0, 0, a]
                dest_ref[0, 0, a] = dest
                pltpu.make_async_copy(_token_rows(h2_ref, t, 1, chunks),
                                      _token_rows(xs_hbm, dest, 1, chunks), sem).start()
        return carry

    lax.fori_loop(0, tile // DMA_ISSUE_UNROLL, issue, 0)
    counted = _token_rows(xs_hbm, 0, tile * TOP_K, chunks)
    pltpu.make_async_copy(counted, counted, sem).wait()


def _dispatch(idx_flat, rank_flat, offs, pad_start, pad_rows, used_blocks, h2_tm, n_rows, tile=512):
    steps = idx_flat.shape[0] // (tile * TOP_K)
    per_tile = lambda a: a.reshape(steps, 1, tile * TOP_K)
    chunks = h2_tm.shape[0] // (idx_flat.shape[0] // TOP_K)
    gran = MOE_ROW_GRAN
    smem = pl.BlockSpec(memory_space=pltpu.SMEM)
    tile_smem = pl.BlockSpec((1, 1, tile * TOP_K), lambda i: (i, 0, 0), memory_space=pltpu.SMEM)
    xs, dest = pl.pallas_call(
        functools.partial(_dispatch_kernel, tile=tile, chunks=chunks,
                          n_experts=pad_start.shape[0], n_row_blocks=n_rows // gran),
        grid=(steps,),
        in_specs=[
            tile_smem, tile_smem, smem, smem, smem, smem,
            pl.BlockSpec((tile * chunks, LANES), lambda i: (i, 0)),
        ],
        out_specs=[pl.BlockSpec(memory_space=pl.ANY), tile_smem],
        out_shape=[jax.ShapeDtypeStruct((n_rows * chunks, LANES), F32),
                   jax.ShapeDtypeStruct((steps, 1, tile * TOP_K), jnp.int32)],
        scratch_shapes=[pltpu.VMEM((gran * chunks, LANES), F32),
                        pltpu.SemaphoreType.DMA(()), pltpu.SemaphoreType.DMA(())],
        compiler_params=pltpu.CompilerParams(dimension_semantics=("arbitrary",)),
        name="dispatch",
    )(per_tile(idx_flat), per_tile(rank_flat), offs, pad_start, pad_rows, used_blocks, h2_tm)
    return xs, dest.reshape(-1)


MOE_WEIGHT_SLOTS = 3


def _moe_kernel(item_e, item_row0, item_nblk, used_blocks, live_items, xs_hbm, wgu_hbm, wdn_hbm,
                bgu_ref, bd_ref, ys_hbm, iobuf, xb, yacc, wgf, wuf, wdf, wgb, wub, wdb,
                sem_io, sem_w, *, chunks, n_col_steps, n_row_blocks):
    s = pl.program_id(0)
    nb = item_nblk[s]
    row0 = item_row0[s]
    gran = MOE_ROW_GRAN
    tn = MOE_COL_TILE
    de = n_col_steps * tn
    total_chunks = live_items[0] * n_col_steps

    def staged(slot):
        return iobuf.at[slot]

    def ys_block(blk):
        return _token_rows(ys_hbm, blk * gran, gran, chunks)

    def weight_copies(g):
        e = item_e[g // n_col_steps]
        col = pl.multiple_of((g % n_col_steps) * tn, tn)
        slot = g % MOE_WEIGHT_SLOTS
        return (
            pltpu.make_async_copy(wgu_hbm.at[e, :, pl.ds(col, tn)], wgf.at[slot], sem_w.at[slot]),
            pltpu.make_async_copy(wgu_hbm.at[e, :, pl.ds(pl.multiple_of(de + col, tn), tn)],
                                  wuf.at[slot], sem_w.at[slot]),
            pltpu.make_async_copy(wdn_hbm.at[e, pl.ds(col, tn), :], wdf.at[slot], sem_w.at[slot]),
        )

    def fetch(g):
        @pl.when(g < total_chunks)
        def _():
            for cp in weight_copies(g):
                cp.start()

    @pl.when(s == 0)
    def _():
        for g in range(MOE_WEIGHT_SLOTS - 1):
            fetch(g)

    @pl.when(nb > 0)
    def _():
        def io_slot(sb):
            return sb % MOE_IO_SLOTS

        def x_copy(sb):
            return pltpu.make_async_copy(
                _token_rows(xs_hbm, row0 + sb * gran, gran, chunks), staged(io_slot(sb)),
                sem_io.at[io_slot(sb)])

        def y_copy(sb):
            return pltpu.make_async_copy(staged(io_slot(sb)), ys_block(row0 // gran + sb),
                                         sem_io.at[io_slot(sb)])

        for sb in range(MOE_IO_SLOTS):
            @pl.when(sb < nb)
            def _():
                x_copy(sb).start()

        def block_rows(sb):
            return pl.ds(pl.multiple_of(sb * gran, gran), gran)

        def column_step(j, phase):
            g = s * n_col_steps + j
            slot = g % MOE_WEIGHT_SLOTS
            for cp in weight_copies(g):
                cp.wait()
            fetch(g + MOE_WEIGHT_SLOTS - 1)
            col = pl.multiple_of(j * tn, tn)
            bg = bgu_ref[:, pl.ds(col, tn)]
            bu = bgu_ref[:, pl.ds(pl.multiple_of(de + col, tn), tn)]

            def cast_weights():
                wgb[...] = wgf[slot].astype(BF16)
                wub[...] = wuf[slot].astype(BF16)
                wdb[...] = wdf[slot].astype(BF16)

            def partial_out(sb):
                x = xb[block_rows(sb), :]
                gate = jnp.dot(x, wgb[...], preferred_element_type=F32) + bg
                up = jnp.dot(x, wub[...], preferred_element_type=F32) + bu
                gate = jnp.minimum(gate, SWIGLU_LIMIT)
                up = jnp.clip(up, -SWIGLU_LIMIT, SWIGLU_LIMIT)
                act = (up + 1.0) * (gate * jax.nn.sigmoid(SWIGLU_ALPHA * gate))
                return jnp.dot(act.astype(BF16), wdb[...], preferred_element_type=F32)

            def trip(first, count):
                blocks = [first + u for u in range(count)]
                if phase == "first":
                    for sb in blocks:
                        x_copy(sb).wait()
                if phase == "last":
                    for sb in blocks:
                        @pl.when(sb >= MOE_IO_SLOTS)
                        def _():
                            y_copy(sb - MOE_IO_SLOTS).wait()
                parts = []
                for sb in blocks:
                    if phase == "first":
                        for c in range(chunks):
                            xb[block_rows(sb), c * LANES:(c + 1) * LANES] = (
                                iobuf[io_slot(sb), pl.ds(c, gran, stride=chunks), :].astype(BF16))
                    parts.append(partial_out(sb))
                for sb, part in zip(blocks, parts):
                    if phase == "first":
                        yacc[block_rows(sb), :] = part + bd_ref[...]
                    elif phase == "middle":
                        yacc[block_rows(sb), :] += part
                    else:
                        y = yacc[block_rows(sb), :] + part
                        for c in range(chunks):
                            iobuf[io_slot(sb), pl.ds(c, gran, stride=chunks), :] = (
                                y[:, c * LANES:(c + 1) * LANES])
                for sb in blocks:
                    if phase == "first":
                        @pl.when(sb + MOE_IO_SLOTS < nb)
                        def _():
                            x_copy(sb + MOE_IO_SLOTS).start()
                    if phase == "last":
                        y_copy(sb).start()

            full = MOE_TRIP_BLOCKS

            def tail(left):
                count = full // 2
                while count >= 1:
                    @pl.when(left % (2 * count) >= count)
                    def _():
                        trip(nb - left % (2 * count), count)

                    count //= 2

            @pl.when(nb >= full)
            def _():
                cast_weights()
                trip(0, full)

                def body(i, c):
                    trip(i * full, full)
                    return c

                lax.fori_loop(1, nb // full, body, 0)
                tail(nb % full)

            @pl.when(nb < full)
            def _():
                cast_weights()
                tail(nb)

        column_step(0, "first")

        def middle(j, carry):
            column_step(j, "middle")
            return carry

        lax.fori_loop(1, n_col_steps - 1, middle, 0)
        column_step(n_col_steps - 1, "last")

        for k in range(MOE_IO_SLOTS):
            @pl.when(nb > k)
            def _():
                y_copy(nb - 1 - k).wait()

    @pl.when(s == pl.num_programs(0) - 1)
    def _():
        iobuf[0] = jnp.zeros(iobuf.shape[1:], iobuf.dtype)

        def fill(blk, carry):
            cp = pltpu.make_async_copy(staged(0), ys_block(blk), sem_io.at[0])
            cp.start()
            cp.wait()
            return carry

        lax.fori_loop(used_blocks[0], n_row_blocks, fill, 0)


def _moe(item_e, item_row0, item_nblk, used_blocks, live_items, xs, w_gate_up, b_gate_up, w_down,
         b_down, n_rows):
    n_experts, d, two_de = w_gate_up.shape
    de = two_de // 2
    chunks = d // LANES
    tn = MOE_COL_TILE
    n_col_steps = de // tn
    assert n_col_steps >= 2
    n_items = item_e.shape[0]
    gran = MOE_ROW_GRAN
    slots = MOE_WEIGHT_SLOTS

    bgu3 = b_gate_up.reshape(n_experts, 1, two_de)
    bd3 = b_down.reshape(n_experts, 1, d)
    per_expert = lambda width: pl.BlockSpec((None, 1, width), lambda s, e, *_: (e[s], 0, 0))
    grid_spec = pltpu.PrefetchScalarGridSpec(
        num_scalar_prefetch=5,
        grid=(n_items,),
        in_specs=[
            pl.BlockSpec(memory_space=pl.ANY),
            pl.BlockSpec(memory_space=pl.ANY),
            pl.BlockSpec(memory_space=pl.ANY),
            per_expert(two_de),
            per_expert(d),
        ],
        out_specs=pl.BlockSpec(memory_space=pl.ANY),
        scratch_shapes=[
            pltpu.VMEM((MOE_IO_SLOTS, gran * chunks, LANES), F32),
            pltpu.VMEM((MOE_ITEM_ROWS, d), BF16),
            pltpu.VMEM((MOE_ITEM_ROWS, d), F32),
            pltpu.VMEM((slots, d, tn), F32),
            pltpu.VMEM((slots, d, tn), F32),
            pltpu.VMEM((slots, tn, d), F32),
            pltpu.VMEM((d, tn), BF16),
            pltpu.VMEM((d, tn), BF16),
            pltpu.VMEM((tn, d), BF16),
            pltpu.SemaphoreType.DMA((MOE_IO_SLOTS,)),
            pltpu.SemaphoreType.DMA((slots,)),
        ],
    )
    return pl.pallas_call(
        functools.partial(_moe_kernel, chunks=chunks, n_col_steps=n_col_steps,
                          n_row_blocks=n_rows // gran),
        grid_spec=grid_spec,
        out_shape=jax.ShapeDtypeStruct((n_rows * chunks, LANES), F32),
        compiler_params=pltpu.CompilerParams(
            dimension_semantics=("arbitrary",),
            vmem_limit_bytes=56 * 1024 * 1024),
        name="moe",
    )(item_e, item_row0, item_nblk, used_blocks, live_items, xs, w_gate_up, w_down, bgu3, bd3)


def _combine_kernel(dest_ref, ys_hbm, x1_ref, gate_ref, fg_ref, o_ref, gbuf, sem,
                    *, tile, chunks, final_norm):
    i = pl.program_id(0)
    slot = i % 2
    per_tile = tile * TOP_K
    stride = _token_row_stride(chunks)

    def gather(step, dst_slot):
        base = step * per_tile

        def issue(g, carry):
            for u in range(DMA_ISSUE_UNROLL):
                t = g * DMA_ISSUE_UNROLL + u
                for k in range(TOP_K):
                    pltpu.make_async_copy(
                        _token_rows(ys_hbm, dest_ref[base + t * TOP_K + k], 1, chunks),
                        gbuf.at[dst_slot, pl.ds(pl.multiple_of((k * tile + t) * stride, SUBLANES),
                                                chunks), :],
                        sem.at[dst_slot]).start()
            return carry

        lax.fori_loop(0, tile // DMA_ISSUE_UNROLL, issue, 0)

    @pl.when(i == 0)
    def _():
        gather(0, 0)

    @pl.when(i + 1 < pl.num_programs(0))
    def _():
        gather(i + 1, 1 - slot)

    counted = gbuf.at[slot, pl.ds(0, per_tile * chunks), :]
    pltpu.make_async_copy(counted, counted, sem.at[slot]).wait()

    gates = gate_ref[...]
    gk = [gates[:, k:k + 1] for k in range(TOP_K)]
    cols = []
    for c in range(chunks):
        acc = x1_ref[:, c * LANES:(c + 1) * LANES]
        for k in range(TOP_K):
            acc = acc + gk[k] * gbuf[slot, pl.ds(k * tile * stride + c, tile, stride=stride), :]
        cols.append(acc)
    out = jnp.concatenate(cols, axis=-1)
    if final_norm:
        out = _rms(out, fg_ref[...])
    o_ref[...] = out


def _combine(dest, ys, x1, gates, final_g, final_norm, tile=256):
    n, d = x1.shape
    chunks = d // LANES
    steps = n // tile
    grid_spec = pltpu.PrefetchScalarGridSpec(
        num_scalar_prefetch=1,
        grid=(steps,),
        in_specs=[
            pl.BlockSpec(memory_space=pl.ANY),
            pl.BlockSpec((tile, d), lambda i, *_: (i, 0)),
            pl.BlockSpec((tile, LANES), lambda i, *_: (i, 0)),
            pl.BlockSpec((1, d), lambda i, *_: (0, 0)),
        ],
        out_specs=pl.BlockSpec((tile, d), lambda i, *_: (i, 0)),
        scratch_shapes=[pltpu.VMEM((2, TOP_K * tile * _token_row_stride(chunks), LANES), F32),
                        pltpu.SemaphoreType.DMA((2,))],
    )
    return pl.pallas_call(
        functools.partial(_combine_kernel, tile=tile, chunks=chunks, final_norm=final_norm),
        grid_spec=grid_spec,
        out_shape=jax.ShapeDtypeStruct((n, d), F32),
        compiler_params=pltpu.CompilerParams(dimension_semantics=("arbitrary",),
                                             vmem_limit_bytes=40 * 1024 * 1024),
        name="combine",
    )(dest, ys, x1, gates, final_g)


def _routing_tables(counts, n_rows_cap):
    n_experts = counts.shape[0]
    gran, item_rows = MOE_ROW_GRAN, MOE_ITEM_ROWS
    padded = ((counts + gran - 1) // gran) * gran
    ends = jnp.cumsum(padded)
    offs = ends - padded
    items_per = (padded + item_rows - 1) // item_rows
    item_ends = jnp.cumsum(items_per)
    item_starts = item_ends - items_per
    n_items = n_experts + n_rows_cap // item_rows
    slot = jnp.arange(n_items, dtype=jnp.int32)
    total = item_ends[-1]
    live = slot < total
    live_slot = jnp.minimum(slot, total - 1)
    e_of = jnp.minimum(jnp.sum((live_slot[:, None] >= item_ends[None, :]).astype(jnp.int32), axis=1),
                       n_experts - 1).astype(jnp.int32)
    local = slot - item_starts[e_of]
    row0 = offs[e_of] + local * item_rows
    rows = jnp.clip(padded[e_of] - local * item_rows, 0, item_rows)
    nblk = jnp.where(live, rows // gran, 0).astype(jnp.int32)
    row0 = jnp.where(live, row0, 0).astype(jnp.int32)
    used_blocks = (ends[-1:] // gran).astype(jnp.int32)
    pad_start = (offs + counts).astype(jnp.int32)
    pad_rows = (padded - counts).astype(jnp.int32)
    live_items = total.reshape(1).astype(jnp.int32)
    return (offs.astype(jnp.int32), e_of, row0, nblk, used_blocks, live_items, pad_start,
            pad_rows)


def _layer(x3, mix_g, w_in, conv_w, conv_g, attn_g, w_out, ffn_g, w_router, b_router,
           w_gate_up, b_gate_up, w_down, b_down, final_g, final_norm):
    b, seq, d = x3.shape
    n = b * seq
    cw_cols = conv_w.shape[1]
    aw_cols = attn_g.shape[0]
    n_experts = w_router.shape[1]
    chunks = d // LANES

    proj = _in_proj(x3.reshape(n, d), mix_g.reshape(1, d), w_in.astype(BF16))
    proj3 = proj.reshape(b, seq, -1)
    y_attn = _attention(proj3, cw_cols, aw_cols)
    x1, h2_tm, idx, gates, rank, counts = _out_proj(
        proj3, y_attn, x3, conv_w, conv_g.reshape(1, -1), attn_g.reshape(1, -1),
        w_out.astype(BF16), ffn_g.reshape(1, d), w_router, b_router.reshape(1, -1))

    n_rows = n * TOP_K + n_experts * MOE_ROW_GRAN
    (offs, item_e, item_row0, item_nblk, used_blocks, live_items, pad_start,
     pad_rows) = _routing_tables(counts[0], n_rows)
    idx_flat = idx[:, :TOP_K].reshape(n * TOP_K)
    rank_flat = rank[:, :TOP_K].reshape(n * TOP_K)
    xs, dest = _dispatch(idx_flat, rank_flat, offs, pad_start, pad_rows, used_blocks, h2_tm,
                         n_rows)
    ys = _moe(item_e, item_row0, item_nblk, used_blocks, live_items, xs, w_gate_up, b_gate_up,
              w_down, b_down, n_rows)
    out = _combine(dest, ys, x1.reshape(n, d), gates, final_g.reshape(1, d), final_norm)
    return out.reshape(b, seq, d)


def kernel(x, mix_norm_g, w_in, conv_w, conv_norm_g, attn_norm_g, w_out, ffn_norm_g, w_router,
           b_router, w_gate_up, b_gate_up, w_down, b_down, final_norm_g):
    depth = w_in.shape[0]
    for layer in range(depth):
        x = _layer(x, mix_norm_g[layer], w_in[layer], conv_w[layer], conv_norm_g[layer],
                   attn_norm_g[layer], w_out[layer], ffn_norm_g[layer], w_router[layer],
                   b_router[layer], w_gate_up[layer], b_gate_up[layer], w_down[layer],
                   b_down[layer], final_norm_g, layer == depth - 1)
    return x
```

```python
import functools

import jax
import jax.numpy as jnp
from jax import lax
from jax.experimental import pallas as pl
from jax.experimental.pallas import tpu as pltpu

HEAD_DIM = 64
CONV_K = 3
ROT_DIM = HEAD_DIM // 4
ROPE_THETA = 500000.0
DILATIONS = (1, 4, 16)
ATTN_BLOCK = 128
TOP_K = 4
SWIGLU_LIMIT = 7.0
SWIGLU_ALPHA = 1.702
EPS = 1e-5

LANES = 128
SUBLANES = 8

MOE_ROW_GRAN = 256
MOE_ITEM_ROWS = 1536
MOE_COL_TILE = 256
MOE_TRIP_BLOCKS = 2
MOE_IO_SLOTS = 2 * MOE_TRIP_BLOCKS
BF16 = jnp.bfloat16
F32 = jnp.float32


def _token_row_stride(chunks):
    return chunks + SUBLANES


def _token_rows(ref, first_token, n_tokens, chunks):
    return ref.at[pl.ds(pl.multiple_of(first_token * chunks, chunks), n_tokens * chunks), :]


def _rms(x, g):
    return x * lax.rsqrt(jnp.mean(x * x, axis=-1, keepdims=True) + EPS) * g


def _in_proj_kernel(x_ref, g_ref, w_ref, o_ref, xn_ref):
    @pl.when(pl.program_id(1) == 0)
    def _():
        xn_ref[...] = _rms(x_ref[...], g_ref[...]).astype(BF16)

    o_ref[...] = jnp.dot(xn_ref[...], w_ref[...], preferred_element_type=F32)


def _in_proj(x2, g, w_bf16, tm=1024):
    n, d = x2.shape
    cols = w_bf16.shape[1]
    tn = 1024 if cols % 1024 == 0 else 512
    return pl.pallas_call(
        _in_proj_kernel,
        grid=(n // tm, cols // tn),
        in_specs=[
            pl.BlockSpec((tm, d), lambda i, j: (i, 0)),
            pl.BlockSpec((1, d), lambda i, j: (0, 0)),
            pl.BlockSpec((d, tn), lambda i, j: (0, j)),
        ],
        out_specs=pl.BlockSpec((tm, tn), lambda i, j: (i, j)),
        out_shape=jax.ShapeDtypeStruct((n, cols), F32),
        scratch_shapes=[pltpu.VMEM((tm, d), BF16)],
        compiler_params=pltpu.CompilerParams(
            dimension_semantics=("arbitrary", "arbitrary"),
            vmem_limit_bytes=48 * 1024 * 1024),
        name="in_proj",
    )(x2, g, w_bf16)


def _attn_kernel(q_ref, k_ref, v_ref, cos_ref, sa_ref, sb_ref, o_ref,
                 qn, kn, q4, q16, a1, d1, m1, a4, d4, m4, a16, d16, m16, acm, dcm, mcm,
                 k1, va1, vb1, k4, va4, vb4, k16, va16, vb16, bias_s, *, seq):
    nblk_total = seq // ATTN_BLOCK
    lane = lax.broadcasted_iota(jnp.int32, (1, LANES), 1)
    head0 = lane < HEAD_DIM

    qi = lax.broadcasted_iota(jnp.int32, (ATTN_BLOCK, 2 * ATTN_BLOCK), 0)
    kj = lax.broadcasted_iota(jnp.int32, (ATTN_BLOCK, 2 * ATTN_BLOCK), 1)
    band = (kj >= qi) & (kj <= qi + ATTN_BLOCK)
    neg = jnp.float32(-jnp.inf)
    bias_s[0] = jnp.where(band, 0.0, neg)
    bias_s[1] = jnp.where(band & (kj >= ATTN_BLOCK), 0.0, neg)

    def rot(x):
        return (x * cos_ref[...] + pltpu.roll(x, LANES - ROT_DIM // 2, 1) * sa_ref[...]
                + pltpu.roll(x, ROT_DIM // 2, 1) * sb_ref[...])

    qn[...] = rot(q_ref[...]) * (HEAD_DIM ** -0.5)
    kn[...] = rot(k_ref[...])

    def put_kv(dst_rows, kc, vac, vbc, k, v):
        kc[dst_rows, :] = k.astype(BF16)
        vac[dst_rows, :] = jnp.where(head0, v, 1.0).astype(BF16)
        vbc[dst_rows, :] = jnp.where(head0, 1.0, v).astype(BF16)

    put_kv(slice(None), k1, va1, vb1, kn[...], v_ref[...])
    for d, qc, kv in ((4, q4, (k4, va4, vb4)), (16, q16, (k16, va16, vb16))):
        ln = seq // d
        for r in range(d):
            rows = slice(r * ln, (r + 1) * ln)
            qc[rows, :] = qn[pl.ds(r, ln, stride=d), :]
            put_kv(rows, *kv, kn[pl.ds(r, ln, stride=d), :], v_ref[pl.ds(r, ln, stride=d), :])

    unroll = 16

    def run_branch(qsrc, kv, dsts, blocks_per_class):
        ksrc, vsrcs = kv[0], kv[1:]
        a_dst, d_dst, m_dst = dsts

        def attend(b, first):
            cur = pl.ds(pl.multiple_of(b * ATTN_BLOCK, ATTN_BLOCK), ATTN_BLOCK)
            qb = qsrc[cur, :]
            prv = pl.ds(pl.multiple_of(jnp.maximum(b - 1, 0) * ATTN_BLOCK, ATTN_BLOCK),
                        ATTN_BLOCK)
            bias = bias_s[int(first)] if isinstance(first, bool) else bias_s[first]
            kk = jnp.concatenate([ksrc[prv, :], ksrc[cur, :]], axis=0)
            res, mx = [], []
            for h in range(2):
                mine = head0 if h == 0 else jnp.logical_not(head0)
                qh = jnp.where(mine, qb, 0.0).astype(BF16)
                s = lax.dot_general(qh, kk, (((1,), (1,)), ((), ())), preferred_element_type=F32)
                s = s + bias
                m = jnp.max(s, axis=-1, keepdims=True)
                p = jnp.exp(s - m).astype(BF16)
                vh = jnp.concatenate([vsrcs[h][prv, :], vsrcs[h][cur, :]], axis=0)
                res.append(jnp.dot(p, vh, preferred_element_type=F32))
                mx.append(m)
            a_dst[cur, :] = jnp.where(head0, res[0], res[1])
            d_dst[cur, :] = pltpu.roll(jnp.where(head0, res[1], res[0]), HEAD_DIM, 1)
            m_dst[cur, :] = jnp.where(head0, mx[0], mx[1])

        def body(i, carry):
            for u in range(unroll):
                b = i * unroll + u
                if unroll % blocks_per_class == 0:
                    first = u % blocks_per_class == 0
                elif blocks_per_class % unroll == 0 and u != 0:
                    first = False
                else:
                    first = jnp.where(jnp.asarray(b % blocks_per_class == 0), 1, 0)
                attend(b, first)
            return carry

        lax.fori_loop(0, nblk_total // unroll, body, 0)

    res1, res4, res16, res_cm = (a1, d1, m1), (a4, d4, m4), (a16, d16, m16), (acm, dcm, mcm)
    run_branch(qn, (k1, va1, vb1), res1, nblk_total)
    for d, qc, kv, res in ((4, q4, (k4, va4, vb4), res4), (16, q16, (k16, va16, vb16), res16)):
        run_branch(qc, kv, res_cm, nblk_total // d)
        ln = seq // d
        for cm, nat in zip(res_cm, res):
            for r in range(d):
                nat[pl.ds(r, ln, stride=d), :] = cm[r * ln:(r + 1) * ln, :]

    ma, mb, mc = m1[...], m4[...], m16[...]
    mx = jnp.maximum(jnp.maximum(ma, mb), mc)
    ea, eb, ec = jnp.exp(ma - mx), jnp.exp(mb - mx), jnp.exp(mc - mx)
    o_ref[...] = ((ea * a1[...] + eb * a4[...] + ec * a16[...])
                  / (ea * d1[...] + eb * d4[...] + ec * d16[...]))


def _rotary_tables(seq):
    half = ROT_DIM // 2
    inv_freq = ROPE_THETA ** (-jnp.arange(0, ROT_DIM, 2, dtype=F32) / ROT_DIM)
    ang = jnp.arange(seq, dtype=F32)[:, None] * inv_freq[None, :]
    cos, sin = jnp.cos(ang), jnp.sin(ang)
    pos = jnp.arange(LANES) % HEAD_DIM
    fidx = pos % half
    in_lo = pos < half
    in_hi = (pos >= half) & (pos < ROT_DIM)
    cos_t = jnp.where((in_lo | in_hi)[None, :], cos[:, fidx], 1.0)
    sa = jnp.where(in_lo[None, :], -sin[:, fidx], 0.0)
    sb = jnp.where(in_hi[None, :], sin[:, fidx], 0.0)
    return cos_t.astype(F32), sa.astype(F32), sb.astype(F32)


def _attention(proj3, conv_w_cols, attn_w_cols):
    b, seq, _ = proj3.shape
    assert seq % (ATTN_BLOCK * DILATIONS[-1]) == 0
    n_pairs = attn_w_cols // LANES
    qoff = 3 * conv_w_cols // LANES
    cos_t, sa, sb = _rotary_tables(seq)
    blk = lambda off: pl.BlockSpec((None, seq, LANES), lambda i, j: (i, 0, off + j))
    tab = pl.BlockSpec((seq, LANES), lambda i, j: (0, 0))
    big = pltpu.VMEM((seq, LANES), F32)
    return pl.pallas_call(
        functools.partial(_attn_kernel, seq=seq),
        grid=(b, n_pairs),
        in_specs=[blk(qoff), blk(qoff + n_pairs), blk(qoff + 2 * n_pairs), tab, tab, tab],
        out_specs=pl.BlockSpec((None, seq, LANES), lambda i, j: (i, 0, j)),
        out_shape=jax.ShapeDtypeStruct((b, seq, attn_w_cols), F32),
        scratch_shapes=([big] * 16 + [pltpu.VMEM((seq, LANES), BF16)] * 9
                        + [pltpu.VMEM((2, ATTN_BLOCK, 2 * ATTN_BLOCK), F32)]),
        compiler_params=pltpu.CompilerParams(
            dimension_semantics=("arbitrary", "arbitrary"),
            vmem_limit_bytes=48 * 1024 * 1024),
        name="attention",
    )(proj3, proj3, proj3, cos_t, sa, sb)


def _out_proj_kernel(cx_ref, cb_ref, cc_ref, hx_ref, hc_ref, ya_ref, x_ref,
                     cw_ref, cg_ref, ag_ref, wo_ref, fg_ref, wrh_ref, wrl_ref, br_ref,
                     x1_ref, h2_ref, idx_ref, gate_ref, rank_ref, cnt_ref, carry,
                     *, tile, n_experts):
    first_tile_of_seq = pl.program_id(1) == 0
    first_step = (pl.program_id(0) == 0) & first_tile_of_seq

    @pl.when(first_step)
    def _():
        carry[...] = jnp.zeros_like(carry)

    u = cc_ref[...] * cx_ref[...]
    uh = jnp.where(first_tile_of_seq, 0.0, hc_ref[...] * hx_ref[...])
    row8 = lax.broadcasted_iota(jnp.int32, (SUBLANES, 1), 0)

    def shifted(k):
        r = pltpu.roll(u, k, 0)
        top = jnp.where(row8 < k, pltpu.roll(uh, k, 0), r[:SUBLANES])
        return jnp.concatenate([top, r[SUBLANES:]], axis=0)

    conv = cw_ref[2:3, :] * u + cw_ref[1:2, :] * shifted(1) + cw_ref[0:1, :] * shifted(2)
    y_conv = cb_ref[...] * conv

    mixed = jnp.concatenate(
        [_rms(y_conv, cg_ref[...]), _rms(ya_ref[...], ag_ref[...])], axis=-1).astype(BF16)
    x1 = x_ref[...] + jnp.dot(mixed, wo_ref[...], preferred_element_type=F32)
    x1_ref[...] = x1
    h2 = _rms(x1, fg_ref[...])
    d = h2.shape[-1]
    chunks = d // LANES
    for c in range(chunks):
        h2_ref[pl.ds(c, tile, stride=chunks), :] = h2[:, c * LANES:(c + 1) * LANES]

    h2_hi = h2.astype(BF16)
    h2_lo = (h2 - h2_hi.astype(F32)).astype(BF16)
    logits = (jnp.dot(h2_hi, wrh_ref[...], preferred_element_type=F32)
              + jnp.dot(h2_hi, wrl_ref[...], preferred_element_type=F32)
              + jnp.dot(h2_lo, wrh_ref[...], preferred_element_type=F32)) + br_ref[...]
    eio = lax.broadcasted_iota(jnp.int32, (tile, n_experts), 1).astype(F32)
    work = logits
    vals, idxs = [], []
    for _ in range(TOP_K):
        m = jnp.max(work, axis=1, keepdims=True)
        ik = jnp.min(jnp.where(work == m, eio, float(n_experts)), axis=1, keepdims=True)
        vals.append(m)
        idxs.append(ik)
        work = jnp.where(eio == ik, -jnp.inf, work)
    exps = [jnp.exp(v - vals[0]) for v in vals]
    tot = exps[0] + exps[1] + exps[2] + exps[3]

    onehot = jnp.zeros((tile, n_experts), F32)
    for ik in idxs:
        onehot = onehot + (eio == ik).astype(F32)
    ri = lax.broadcasted_iota(jnp.int32, (tile, tile), 0)
    ci = lax.broadcasted_iota(jnp.int32, (tile, tile), 1)
    tri = (ci < ri).astype(BF16)
    before = jnp.dot(tri, onehot.astype(BF16), preferred_element_type=F32) + carry[...]
    carry[...] = carry[...] + jnp.sum(onehot, axis=0, keepdims=True)
    cnt_ref[...] = carry[...].astype(jnp.int32)

    lio = lax.broadcasted_iota(jnp.int32, (tile, LANES), 1)
    idx_out = jnp.zeros((tile, LANES), jnp.int32)
    gate_out = jnp.zeros((tile, LANES), F32)
    rank_out = jnp.zeros((tile, LANES), jnp.int32)
    for k in range(TOP_K):
        rk = jnp.sum(jnp.where(eio == idxs[k], before, 0.0), axis=1, keepdims=True)
        idx_out = jnp.where(lio == k, idxs[k].astype(jnp.int32), idx_out)
        gate_out = jnp.where(lio == k, exps[k] / tot, gate_out)
        rank_out = jnp.where(lio == k, rk.astype(jnp.int32), rank_out)
    idx_ref[...] = idx_out
    gate_ref[...] = gate_out
    rank_ref[...] = rank_out


def _out_proj(proj3, y_attn, x3, conv_w, conv_g, attn_g, w_out_bf16, ffn_g, w_router, b_router,
              tile=256):
    b, seq, d = x3.shape
    cw_cols = conv_w.shape[1]
    aw_cols = y_attn.shape[2]
    n_experts = w_router.shape[1]
    w_router_hi = w_router.astype(BF16)
    w_router_lo = (w_router - w_router_hi.astype(F32)).astype(BF16)
    chunks = d // LANES
    tiles = seq // tile
    n = b * seq
    halo_blocks = tile // SUBLANES
    row_blk = lambda width, col: pl.BlockSpec((None, tile, width), lambda i, j: (i, j, col))
    halo = lambda col: pl.BlockSpec(
        (None, SUBLANES, cw_cols), lambda i, j: (i, jnp.maximum(j * halo_blocks - 1, 0), col))
    const = lambda shape: pl.BlockSpec(shape, lambda i, j: (0,) * len(shape))
    flat = lambda width: pl.BlockSpec((tile, width), lambda i, j: (i * tiles + j, 0))
    outs = pl.pallas_call(
        functools.partial(_out_proj_kernel, tile=tile, n_experts=n_experts),
        grid=(b, tiles),
        in_specs=[
            row_blk(cw_cols, 0), row_blk(cw_cols, 1), row_blk(cw_cols, 2), halo(0), halo(2),
            row_blk(aw_cols, 0), row_blk(d, 0),
            const((CONV_K, cw_cols)), const((1, cw_cols)), const((1, aw_cols)),
            const((d, d)), const((1, d)), const((d, n_experts)), const((d, n_experts)),
            const((1, n_experts)),
        ],
        out_specs=[
            row_blk(d, 0),
            pl.BlockSpec((tile * chunks, LANES), lambda i, j: (i * tiles + j, 0)),
            flat(LANES), flat(LANES), flat(LANES),
            const((1, n_experts)),
        ],
        out_shape=[
            jax.ShapeDtypeStruct((b, seq, d), F32),
            jax.ShapeDtypeStruct((n * chunks, LANES), F32),
            jax.ShapeDtypeStruct((n, LANES), jnp.int32),
            jax.ShapeDtypeStruct((n, LANES), F32),
            jax.ShapeDtypeStruct((n, LANES), jnp.int32),
            jax.ShapeDtypeStruct((1, n_experts), jnp.int32),
        ],
        scratch_shapes=[pltpu.VMEM((1, n_experts), F32)],
        compiler_params=pltpu.CompilerParams(
            dimension_semantics=("arbitrary", "arbitrary"),
            vmem_limit_bytes=48 * 1024 * 1024),
        name="out_proj",
    )(proj3, proj3, proj3, proj3, proj3, y_attn, x3, conv_w, conv_g, attn_g, w_out_bf16,
      ffn_g, w_router_hi, w_router_lo, b_router)
    return outs


DMA_ISSUE_UNROLL = 4


def _dispatch_kernel(idx_ref, rank_ref, offs_ref, pad_start, pad_rows, used_blocks, h2_ref, xs_hbm,
                     dest_ref, zbuf, sem, zsem, *, tile, chunks, n_experts, n_row_blocks):
    gran = MOE_ROW_GRAN

    @pl.when(pl.program_id(0) == 0)
    def _():
        zbuf[...] = jnp.zeros_like(zbuf)

        def zero_copy(first_row, n_rows):
            return pltpu.make_async_copy(_token_rows(zbuf, 0, n_rows, chunks),
                                         _token_rows(xs_hbm, first_row, n_rows, chunks), zsem)

        def for_each_zero_run(act):
            def per_expert(e, carry):
                row = pad_start[e]
                left = pad_rows[e]
                for bit in reversed(range(gran.bit_length() - 1)):
                    take = (left >> bit) & 1

                    @pl.when(take == 1)
                    def _():
                        act(zero_copy(row, 1 << bit))

                    row = row + (take << bit)
                return carry

            lax.fori_loop(0, n_experts, per_expert, 0)

            def tail(blk, carry):
                act(zero_copy(blk * gran, gran))
                return carry

            lax.fori_loop(used_blocks[0], n_row_blocks, tail, 0)

        for_each_zero_run(lambda cp: cp.start())
        for_each_zero_run(lambda cp: cp.wait())

    def issue(g, carry):
        for u in range(DMA_ISSUE_UNROLL):
            t = g * DMA_ISSUE_UNROLL + u
            for k in range(TOP_K):
                a = t * TOP_K + k
                dest = offs_ref[idx_ref[0, 0, a]] + rank_ref[0, 0, a]
                dest_ref[0, 0, a] = dest
                pltpu.make_async_copy(_token_rows(h2_ref, t, 1, chunks),
                                      _token_rows(xs_hbm, dest, 1, chunks), sem).start()
        return carry

    lax.fori_loop(0, tile // DMA_ISSUE_UNROLL, issue, 0)
    counted = _token_rows(xs_hbm, 0, tile * TOP_K, chunks)
    pltpu.make_async_copy(counted, counted, sem).wait()


def _dispatch(idx_flat, rank_flat, offs, pad_start, pad_rows, used_blocks, h2_tm, n_rows, tile=512):
    steps = idx_flat.shape[0] // (tile * TOP_K)
    per_tile = lambda a: a.reshape(steps, 1, tile * TOP_K)
    chunks = h2_tm.shape[0] // (idx_flat.shape[0] // TOP_K)
    gran = MOE_ROW_GRAN
    smem = pl.BlockSpec(memory_space=pltpu.SMEM)
    tile_smem = pl.BlockSpec((1, 1, tile * TOP_K), lambda i: (i, 0, 0), memory_space=pltpu.SMEM)
    xs, dest = pl.pallas_call(
        functools.partial(_dispatch_kernel, tile=tile, chunks=chunks,
                          n_experts=pad_start.shape[0], n_row_blocks=n_rows // gran),
        grid=(steps,),
        in_specs=[
            tile_smem, tile_smem, smem, smem, smem, smem,
            pl.BlockSpec((tile * chunks, LANES), lambda i: (i, 0)),
        ],
        out_specs=[pl.BlockSpec(memory_space=pl.ANY), tile_smem],
        out_shape=[jax.ShapeDtypeStruct((n_rows * chunks, LANES), F32),
                   jax.ShapeDtypeStruct((steps, 1, tile * TOP_K), jnp.int32)],
        scratch_shapes=[pltpu.VMEM((gran * chunks, LANES), F32),
                        pltpu.SemaphoreType.DMA(()), pltpu.SemaphoreType.DMA(())],
        compiler_params=pltpu.CompilerParams(dimension_semantics=("arbitrary",)),
        name="dispatch",
    )(per_tile(idx_flat), per_tile(rank_flat), offs, pad_start, pad_rows, used_blocks, h2_tm)
    return xs, dest.reshape(-1)


MOE_WEIGHT_SLOTS = 3


def _moe_kernel(item_e, item_row0, item_nblk, used_blocks, live_items, xs_hbm, wgu_hbm, wdn_hbm,
                bgu_ref, bd_ref, ys_hbm, iobuf, xb, yacc, wgf, wuf, wdf, wgb, wub, wdb,
                sem_io, sem_w, *, chunks, n_col_steps, n_row_blocks):
    s = pl.program_id(0)
    nb = item_nblk[s]
    row0 = item_row0[s]
    gran = MOE_ROW_GRAN
    tn = MOE_COL_TILE
    de = n_col_steps * tn
    total_chunks = live_items[0] * n_col_steps

    def staged(slot):
        return iobuf.at[slot]

    def ys_block(blk):
        return _token_rows(ys_hbm, blk * gran, gran, chunks)

    def weight_copies(g):
        e = item_e[g // n_col_steps]
        col = pl.multiple_of((g % n_col_steps) * tn, tn)
        slot = g % MOE_WEIGHT_SLOTS
        return (
            pltpu.make_async_copy(wgu_hbm.at[e, :, pl.ds(col, tn)], wgf.at[slot], sem_w.at[slot]),
            pltpu.make_async_copy(wgu_hbm.at[e, :, pl.ds(pl.multiple_of(de + col, tn), tn)],
                                  wuf.at[slot], sem_w.at[slot]),
            pltpu.make_async_copy(wdn_hbm.at[e, pl.ds(col, tn), :], wdf.at[slot], sem_w.at[slot]),
        )

    def fetch(g):
        @pl.when(g < total_chunks)
        def _():
            for cp in weight_copies(g):
                cp.start()

    @pl.when(s == 0)
    def _():
        for g in range(MOE_WEIGHT_SLOTS - 1):
            fetch(g)

    @pl.when(nb > 0)
    def _():
        def io_slot(sb):
            return sb % MOE_IO_SLOTS

        def x_copy(sb):
            return pltpu.make_async_copy(
                _token_rows(xs_hbm, row0 + sb * gran, gran, chunks), staged(io_slot(sb)),
                sem_io.at[io_slot(sb)])

        def y_copy(sb):
            return pltpu.make_async_copy(staged(io_slot(sb)), ys_block(row0 // gran + sb),
                                         sem_io.at[io_slot(sb)])

        for sb in range(MOE_IO_SLOTS):
            @pl.when(sb < nb)
            def _():
                x_copy(sb).start()

        def block_rows(sb):
            return pl.ds(pl.multiple_of(sb * gran, gran), gran)

        def column_step(j, phase):
            g = s * n_col_steps + j
            slot = g % MOE_WEIGHT_SLOTS
            for cp in weight_copies(g):
                cp.wait()
            fetch(g + MOE_WEIGHT_SLOTS - 1)
            col = pl.multiple_of(j * tn, tn)
            bg = bgu_ref[:, pl.ds(col, tn)]
            bu = bgu_ref[:, pl.ds(pl.multiple_of(de + col, tn), tn)]

            def cast_weights():
                wgb[...] = wgf[slot].astype(BF16)
                wub[...] = wuf[slot].astype(BF16)
                wdb[...] = wdf[slot].astype(BF16)

            def partial_out(sb):
                x = xb[block_rows(sb), :]
                gate = jnp.dot(x, wgb[...], preferred_element_type=F32) + bg
                up = jnp.dot(x, wub[...], preferred_element_type=F32) + bu
                gate = jnp.minimum(gate, SWIGLU_LIMIT)
                up = jnp.clip(up, -SWIGLU_LIMIT, SWIGLU_LIMIT)
                act = (up + 1.0) * (gate * jax.nn.sigmoid(SWIGLU_ALPHA * gate))
                return jnp.dot(act.astype(BF16), wdb[...], preferred_element_type=F32)

            def trip(first, count):
                blocks = [first + u for u in range(count)]
                if phase == "first":
                    for sb in blocks:
                        x_copy(sb).wait()
                if phase == "last":
                    for sb in blocks:
                        @pl.when(sb >= MOE_IO_SLOTS)
                        def _():
                            y_copy(sb - MOE_IO_SLOTS).wait()
                parts = []
                for sb in blocks:
                    if phase == "first":
                        for c in range(chunks):
                            xb[block_rows(sb), c * LANES:(c + 1) * LANES] = (
                                iobuf[io_slot(sb), pl.ds(c, gran, stride=chunks), :].astype(BF16))
                    parts.append(partial_out(sb))
                for sb, part in zip(blocks, parts):
                    if phase == "first":
                        yacc[block_rows(sb), :] = part + bd_ref[...]
                    elif phase == "middle":
                        yacc[block_rows(sb), :] += part
                    else:
                        y = yacc[block_rows(sb), :] + part
                        for c in range(chunks):
                            iobuf[io_slot(sb), pl.ds(c, gran, stride=chunks), :] = (
                                y[:, c * LANES:(c + 1) * LANES])
                for sb in blocks:
                    if phase == "first":
                        @pl.when(sb + MOE_IO_SLOTS < nb)
                        def _():
                            x_copy(sb + MOE_IO_SLOTS).start()
                    if phase == "last":
                        y_copy(sb).start()

            full = MOE_TRIP_BLOCKS

            def tail(left):
                count = full // 2
                while count >= 1:
                    @pl.when(left % (2 * count) >= count)
                    def _():
                        trip(nb - left % (2 * count), count)

                    count //= 2

            @pl.when(nb >= full)
            def _():
                cast_weights()
                trip(0, full)

                def body(i, c):
                    trip(i * full, full)
                    return c

                lax.fori_loop(1, nb // full, body, 0)
                tail(nb % full)

            @pl.when(nb < full)
            def _():
                cast_weights()
                tail(nb)

        column_step(0, "first")

        def middle(j, carry):
            column_step(j, "middle")
            return carry

        lax.fori_loop(1, n_col_steps - 1, middle, 0)
        column_step(n_col_steps - 1, "last")

        for k in range(MOE_IO_SLOTS):
            @pl.when(nb > k)
            def _():
                y_copy(nb - 1 - k).wait()

    @pl.when(s == pl.num_programs(0) - 1)
    def _():
        iobuf[0] = jnp.zeros(iobuf.shape[1:], iobuf.dtype)

        def fill(blk, carry):
            cp = pltpu.make_async_copy(staged(0), ys_block(blk), sem_io.at[0])
            cp.start()
            cp.wait()
            return carry

        lax.fori_loop(used_blocks[0], n_row_blocks, fill, 0)


def _moe(item_e, item_row0, item_nblk, used_blocks, live_items, xs, w_gate_up, b_gate_up, w_down,
         b_down, n_rows):
    n_experts, d, two_de = w_gate_up.shape
    de = two_de // 2
    chunks = d // LANES
    tn = MOE_COL_TILE
    n_col_steps = de // tn
    assert n_col_steps >= 2
    n_items = item_e.shape[0]
    gran = MOE_ROW_GRAN
    slots = MOE_WEIGHT_SLOTS

    bgu3 = b_gate_up.reshape(n_experts, 1, two_de)
    bd3 = b_down.reshape(n_experts, 1, d)
    per_expert = lambda width: pl.BlockSpec((None, 1, width), lambda s, e, *_: (e[s], 0, 0))
    grid_spec = pltpu.PrefetchScalarGridSpec(
        num_scalar_prefetch=5,
        grid=(n_items,),
        in_specs=[
            pl.BlockSpec(memory_space=pl.ANY),
            pl.BlockSpec(memory_space=pl.ANY),
            pl.BlockSpec(memory_space=pl.ANY),
            per_expert(two_de),
            per_expert(d),
        ],
        out_specs=pl.BlockSpec(memory_space=pl.ANY),
        scratch_shapes=[
            pltpu.VMEM((MOE_IO_SLOTS, gran * chunks, LANES), F32),
            pltpu.VMEM((MOE_ITEM_ROWS, d), BF16),
            pltpu.VMEM((MOE_ITEM_ROWS, d), F32),
            pltpu.VMEM((slots, d, tn), F32),
            pltpu.VMEM((slots, d, tn), F32),
            pltpu.VMEM((slots, tn, d), F32),
            pltpu.VMEM((d, tn), BF16),
            pltpu.VMEM((d, tn), BF16),
            pltpu.VMEM((tn, d), BF16),
            pltpu.SemaphoreType.DMA((MOE_IO_SLOTS,)),
            pltpu.SemaphoreType.DMA((slots,)),
        ],
    )
    return pl.pallas_call(
        functools.partial(_moe_kernel, chunks=chunks, n_col_steps=n_col_steps,
                          n_row_blocks=n_rows // gran),
        grid_spec=grid_spec,
        out_shape=jax.ShapeDtypeStruct((n_rows * chunks, LANES), F32),
        compiler_params=pltpu.CompilerParams(
            dimension_semantics=("arbitrary",),
            vmem_limit_bytes=56 * 1024 * 1024),
        name="moe",
    )(item_e, item_row0, item_nblk, used_blocks, live_items, xs, w_gate_up, w_down, bgu3, bd3)


def _combine_kernel(dest_ref, ys_hbm, x1_ref, gate_ref, fg_ref, o_ref, gbuf, sem,
                    *, tile, chunks, final_norm):
    i = pl.program_id(0)
    slot = i % 2
    per_tile = tile * TOP_K
    stride = _token_row_stride(chunks)

    def gather(step, dst_slot):
        base = step * per_tile

        def issue(g, carry):
            for u in range(DMA_ISSUE_UNROLL):
                t = g * DMA_ISSUE_UNROLL + u
                for k in range(TOP_K):
                    pltpu.make_async_copy(
                        _token_rows(ys_hbm, dest_ref[base + t * TOP_K + k], 1, chunks),
                        gbuf.at[dst_slot, pl.ds(pl.multiple_of((k * tile + t) * stride, SUBLANES),
                                                chunks), :],
                        sem.at[dst_slot]).start()
            return carry

        lax.fori_loop(0, tile // DMA_ISSUE_UNROLL, issue, 0)

    @pl.when(i == 0)
    def _():
        gather(0, 0)

    @pl.when(i + 1 < pl.num_programs(0))
    def _():
        gather(i + 1, 1 - slot)

    counted = gbuf.at[slot, pl.ds(0, per_tile * chunks), :]
    pltpu.make_async_copy(counted, counted, sem.at[slot]).wait()

    gates = gate_ref[...]
    gk = [gates[:, k:k + 1] for k in range(TOP_K)]
    cols = []
    for c in range(chunks):
        acc = x1_ref[:, c * LANES:(c + 1) * LANES]
        for k in range(TOP_K):
            acc = acc + gk[k] * gbuf[slot, pl.ds(k * tile * stride + c, tile, stride=stride), :]
        cols.append(acc)
    out = jnp.concatenate(cols, axis=-1)
    if final_norm:
        out = _rms(out, fg_ref[...])
    o_ref[...] = out


def _combine(dest, ys, x1, gates, final_g, final_norm, tile=128):
    n, d = x1.shape
    chunks = d // LANES
    steps = n // tile
    grid_spec = pltpu.PrefetchScalarGridSpec(
        num_scalar_prefetch=1,
        grid=(steps,),
        in_specs=[
            pl.BlockSpec(memory_space=pl.ANY),
            pl.BlockSpec((tile, d), lambda i, *_: (i, 0)),
            pl.BlockSpec((tile, LANES), lambda i, *_: (i, 0)),
            pl.BlockSpec((1, d), lambda i, *_: (0, 0)),
        ],
        out_specs=pl.BlockSpec((tile, d), lambda i, *_: (i, 0)),
        scratch_shapes=[pltpu.VMEM((2, TOP_K * tile * _token_row_stride(chunks), LANES), F32),
                        pltpu.SemaphoreType.DMA((2,))],
    )
    return pl.pallas_call(
        functools.partial(_combine_kernel, tile=tile, chunks=chunks, final_norm=final_norm),
        grid_spec=grid_spec,
        out_shape=jax.ShapeDtypeStruct((n, d), F32),
        compiler_params=pltpu.CompilerParams(dimension_semantics=("arbitrary",),
                                             vmem_limit_bytes=40 * 1024 * 1024),
        name="combine",
    )(dest, ys, x1, gates, final_g)


def _routing_tables(counts, n_rows_cap):
    n_experts = counts.shape[0]
    gran, item_rows = MOE_ROW_GRAN, MOE_ITEM_ROWS
    padded = ((counts + gran - 1) // gran) * gran
    ends = jnp.cumsum(padded)
    offs = ends - padded
    items_per = (padded + item_rows - 1) // item_rows
    item_ends = jnp.cumsum(items_per)
    item_starts = item_ends - items_per
    n_items = n_experts + n_rows_cap // item_rows
    slot = jnp.arange(n_items, dtype=jnp.int32)
    total = item_ends[-1]
    live = slot < total
    live_slot = jnp.minimum(slot, total - 1)
    e_of = jnp.minimum(jnp.sum((live_slot[:, None] >= item_ends[None, :]).astype(jnp.int32), axis=1),
                       n_experts - 1).astype(jnp.int32)
    local = slot - item_starts[e_of]
    row0 = offs[e_of] + local * item_rows
    rows = jnp.clip(padded[e_of] - local * item_rows, 0, item_rows)
    nblk = jnp.where(live, rows // gran, 0).astype(jnp.int32)
    row0 = jnp.where(live, row0, 0).astype(jnp.int32)
    used_blocks = (ends[-1:] // gran).astype(jnp.int32)
    pad_start = (offs + counts).astype(jnp.int32)
    pad_rows = (padded - counts).astype(jnp.int32)
    live_items = total.reshape(1).astype(jnp.int32)
    return (offs.astype(jnp.int32), e_of, row0, nblk, used_blocks, live_items, pad_start,
            pad_rows)


def _layer(x3, mix_g, w_in, conv_w, conv_g, attn_g, w_out, ffn_g, w_router, b_router,
           w_gate_up, b_gate_up, w_down, b_down, final_g, final_norm):
    b, seq, d = x3.shape
    n = b * seq
    cw_cols = conv_w.shape[1]
    aw_cols = attn_g.shape[0]
    n_experts = w_router.shape[1]
    chunks = d // LANES

    proj = _in_proj(x3.reshape(n, d), mix_g.reshape(1, d), w_in.astype(BF16))
    proj3 = proj.reshape(b, seq, -1)
    y_attn = _attention(proj3, cw_cols, aw_cols)
    x1, h2_tm, idx, gates, rank, counts = _out_proj(
        proj3, y_attn, x3, conv_w, conv_g.reshape(1, -1), attn_g.reshape(1, -1),
        w_out.astype(BF16), ffn_g.reshape(1, d), w_router, b_router.reshape(1, -1))

    n_rows = n * TOP_K + n_experts * MOE_ROW_GRAN
    (offs, item_e, item_row0, item_nblk, used_blocks, live_items, pad_start,
     pad_rows) = _routing_tables(counts[0], n_rows)
    idx_flat = idx[:, :TOP_K].reshape(n * TOP_K)
    rank_flat = rank[:, :TOP_K].reshape(n * TOP_K)
    xs, dest = _dispatch(idx_flat, rank_flat, offs, pad_start, pad_rows, used_blocks, h2_tm,
                         n_rows)
    ys = _moe(item_e, item_row0, item_nblk, used_blocks, live_items, xs, w_gate_up, b_gate_up,
              w_down, b_down, n_rows)
    out = _combine(dest, ys, x1.reshape(n, d), gates, final_g.reshape(1, d), final_norm)
    return out.reshape(b, seq, d)


def kernel(x, mix_norm_g, w_in, conv_w, conv_norm_g, attn_norm_g, w_out, ffn_norm_g, w_router,
           b_router, w_gate_up, b_gate_up, w_down, b_down, final_norm_g):
    depth = w_in.shape[0]
    for layer in range(depth):
        x = _layer(x, mix_norm_g[layer], w_in[layer], conv_w[layer], conv_norm_g[layer],
                   attn_norm_g[layer], w_out[layer], ffn_norm_g[layer], w_router[layer],
                   b_router[layer], w_gate_up[layer], b_gate_up[layer], w_down[layer],
                   b_down[layer], final_norm_g, layer == depth - 1)
    return x
```

```python
import functools

import jax
import jax.numpy as jnp
from jax import lax
from jax.experimental import pallas as pl
from jax.experimental.pallas import tpu as pltpu

HEAD_DIM = 64
CONV_K = 3
ROT_DIM = HEAD_DIM // 4
ROPE_THETA = 500000.0
DILATIONS = (1, 4, 16)
ATTN_BLOCK = 128
TOP_K = 4
SWIGLU_LIMIT = 7.0
SWIGLU_ALPHA = 1.702
EPS = 1e-5

LANES = 128
SUBLANES = 8

MOE_ROW_GRAN = 256
MOE_ITEM_ROWS = 1536
MOE_COL_TILE = 256
MOE_TRIP_BLOCKS = 2
MOE_IO_SLOTS = 2 * MOE_TRIP_BLOCKS
BF16 = jnp.bfloat16
F32 = jnp.float32


def _token_row_stride(chunks):
    return chunks + SUBLANES


def _token_rows(ref, first_token, n_tokens, chunks):
    return ref.at[pl.ds(pl.multiple_of(first_token * chunks, chunks), n_tokens * chunks), :]


def _rms(x, g):
    return x * lax.rsqrt(jnp.mean(x * x, axis=-1, keepdims=True) + EPS) * g


def _in_proj_kernel(x_ref, g_ref, w_ref, o_ref, xn_ref):
    @pl.when(pl.program_id(1) == 0)
    def _():
        xn_ref[...] = _rms(x_ref[...], g_ref[...]).astype(BF16)

    o_ref[...] = jnp.dot(xn_ref[...], w_ref[...], preferred_element_type=F32)


def _in_proj(x2, g, w_bf16, tm=1024):
    n, d = x2.shape
    cols = w_bf16.shape[1]
    tn = 1024 if cols % 1024 == 0 else 512
    return pl.pallas_call(
        _in_proj_kernel,
        grid=(n // tm, cols // tn),
        in_specs=[
            pl.BlockSpec((tm, d), lambda i, j: (i, 0)),
            pl.BlockSpec((1, d), lambda i, j: (0, 0)),
            pl.BlockSpec((d, tn), lambda i, j: (0, j)),
        ],
        out_specs=pl.BlockSpec((tm, tn), lambda i, j: (i, j)),
        out_shape=jax.ShapeDtypeStruct((n, cols), F32),
        scratch_shapes=[pltpu.VMEM((tm, d), BF16)],
        compiler_params=pltpu.CompilerParams(
            dimension_semantics=("arbitrary", "arbitrary"),
            vmem_limit_bytes=48 * 1024 * 1024),
        name="in_proj",
    )(x2, g, w_bf16)


def _attn_kernel(q_ref, k_ref, v_ref, cos_ref, sa_ref, sb_ref, o_ref,
                 qn, kn, q4, q16, a1, d1, m1, a4, d4, m4, a16, d16, m16, acm, dcm, mcm,
                 k1, va1, vb1, k4, va4, vb4, k16, va16, vb16, bias_s, *, seq):
    nblk_total = seq // ATTN_BLOCK
    lane = lax.broadcasted_iota(jnp.int32, (1, LANES), 1)
    head0 = lane < HEAD_DIM

    qi = lax.broadcasted_iota(jnp.int32, (ATTN_BLOCK, 2 * ATTN_BLOCK), 0)
    kj = lax.broadcasted_iota(jnp.int32, (ATTN_BLOCK, 2 * ATTN_BLOCK), 1)
    band = (kj >= qi) & (kj <= qi + ATTN_BLOCK)
    neg = jnp.float32(-jnp.inf)
    bias_s[0] = jnp.where(band, 0.0, neg)
    bias_s[1] = jnp.where(band & (kj >= ATTN_BLOCK), 0.0, neg)

    def rot(x):
        return (x * cos_ref[...] + pltpu.roll(x, LANES - ROT_DIM // 2, 1) * sa_ref[...]
                + pltpu.roll(x, ROT_DIM // 2, 1) * sb_ref[...])

    qn[...] = rot(q_ref[...]) * (HEAD_DIM ** -0.5)
    kn[...] = rot(k_ref[...])

    def put_kv(dst_rows, kc, vac, vbc, k, v):
        kc[dst_rows, :] = k.astype(BF16)
        vac[dst_rows, :] = jnp.where(head0, v, 1.0).astype(BF16)
        vbc[dst_rows, :] = jnp.where(head0, 1.0, v).astype(BF16)

    put_kv(slice(None), k1, va1, vb1, kn[...], v_ref[...])
    for d, qc, kv in ((4, q4, (k4, va4, vb4)), (16, q16, (k16, va16, vb16))):
        ln = seq // d
        for r in range(d):
            rows = slice(r * ln, (r + 1) * ln)
            qc[rows, :] = qn[pl.ds(r, ln, stride=d), :]
            put_kv(rows, *kv, kn[pl.ds(r, ln, stride=d), :], v_ref[pl.ds(r, ln, stride=d), :])

    unroll = 16

    def run_branch(qsrc, kv, dsts, blocks_per_class):
        ksrc, vsrcs = kv[0], kv[1:]
        a_dst, d_dst, m_dst = dsts

        def attend(b, first):
            cur = pl.ds(pl.multiple_of(b * ATTN_BLOCK, ATTN_BLOCK), ATTN_BLOCK)
            qb = qsrc[cur, :]
            prv = pl.ds(pl.multiple_of(jnp.maximum(b - 1, 0) * ATTN_BLOCK, ATTN_BLOCK),
                        ATTN_BLOCK)
            bias = bias_s[int(first)] if isinstance(first, bool) else bias_s[first]
            kk = jnp.concatenate([ksrc[prv, :], ksrc[cur, :]], axis=0)
            res, mx = [], []
            for h in range(2):
                mine = head0 if h == 0 else jnp.logical_not(head0)
                qh = jnp.where(mine, qb, 0.0).astype(BF16)
                s = lax.dot_general(qh, kk, (((1,), (1,)), ((), ())), preferred_element_type=F32)
                s = s + bias
                m = jnp.max(s, axis=-1, keepdims=True)
                p = jnp.exp(s - m).astype(BF16)
                vh = jnp.concatenate([vsrcs[h][prv, :], vsrcs[h][cur, :]], axis=0)
                res.append(jnp.dot(p, vh, preferred_element_type=F32))
                mx.append(m)
            a_dst[cur, :] = jnp.where(head0, res[0], res[1])
            d_dst[cur, :] = pltpu.roll(jnp.where(head0, res[1], res[0]), HEAD_DIM, 1)
            m_dst[cur, :] = jnp.where(head0, mx[0], mx[1])

        def body(i, carry):
            for u in range(unroll):
                b = i * unroll + u
                if unroll % blocks_per_class == 0:
                    first = u % blocks_per_class == 0
                elif blocks_per_class % unroll == 0 and u != 0:
                    first = False
                else:
                    first = jnp.where(jnp.asarray(b % blocks_per_class == 0), 1, 0)
                attend(b, first)
            return carry

        lax.fori_loop(0, nblk_total // unroll, body, 0)

    res1, res4, res16, res_cm = (a1, d1, m1), (a4, d4, m4), (a16, d16, m16), (acm, dcm, mcm)
    run_branch(qn, (k1, va1, vb1), res1, nblk_total)
    for d, qc, kv, res in ((4, q4, (k4, va4, vb4), res4), (16, q16, (k16, va16, vb16), res16)):
        run_branch(qc, kv, res_cm, nblk_total // d)
        ln = seq // d
        for cm, nat in zip(res_cm, res):
            for r in range(d):
                nat[pl.ds(r, ln, stride=d), :] = cm[r * ln:(r + 1) * ln, :]

    ma, mb, mc = m1[...], m4[...], m16[...]
    mx = jnp.maximum(jnp.maximum(ma, mb), mc)
    ea, eb, ec = jnp.exp(ma - mx), jnp.exp(mb - mx), jnp.exp(mc - mx)
    o_ref[...] = ((ea * a1[...] + eb * a4[...] + ec * a16[...])
                  / (ea * d1[...] + eb * d4[...] + ec * d16[...]))


def _rotary_tables(seq):
    half = ROT_DIM // 2
    inv_freq = ROPE_THETA ** (-jnp.arange(0, ROT_DIM, 2, dtype=F32) / ROT_DIM)
    ang = jnp.arange(seq, dtype=F32)[:, None] * inv_freq[None, :]
    cos, sin = jnp.cos(ang), jnp.sin(ang)
    pos = jnp.arange(LANES) % HEAD_DIM
    fidx = pos % half
    in_lo = pos < half
    in_hi = (pos >= half) & (pos < ROT_DIM)
    cos_t = jnp.where((in_lo | in_hi)[None, :], cos[:, fidx], 1.0)
    sa = jnp.where(in_lo[None, :], -sin[:, fidx], 0.0)
    sb = jnp.where(in_hi[None, :], sin[:, fidx], 0.0)
    return cos_t.astype(F32), sa.astype(F32), sb.astype(F32)


def _attention(proj3, conv_w_cols, attn_w_cols):
    b, seq, _ = proj3.shape
    assert seq % (ATTN_BLOCK * DILATIONS[-1]) == 0
    n_pairs = attn_w_cols // LANES
    qoff = 3 * conv_w_cols // LANES
    cos_t, sa, sb = _rotary_tables(seq)
    blk = lambda off: pl.BlockSpec((None, seq, LANES), lambda i, j: (i, 0, off + j))
    tab = pl.BlockSpec((seq, LANES), lambda i, j: (0, 0))
    big = pltpu.VMEM((seq, LANES), F32)
    return pl.pallas_call(
        functools.partial(_attn_kernel, seq=seq),
        grid=(b, n_pairs),
        in_specs=[blk(qoff), blk(qoff + n_pairs), blk(qoff + 2 * n_pairs), tab, tab, tab],
        out_specs=pl.BlockSpec((None, seq, LANES), lambda i, j: (i, 0, j)),
        out_shape=jax.ShapeDtypeStruct((b, seq, attn_w_cols), F32),
        scratch_shapes=([big] * 16 + [pltpu.VMEM((seq, LANES), BF16)] * 9
                        + [pltpu.VMEM((2, ATTN_BLOCK, 2 * ATTN_BLOCK), F32)]),
        compiler_params=pltpu.CompilerParams(
            dimension_semantics=("arbitrary", "arbitrary"),
            vmem_limit_bytes=48 * 1024 * 1024),
        name="attention",
    )(proj3, proj3, proj3, cos_t, sa, sb)


def _out_proj_kernel(cx_ref, cb_ref, cc_ref, hx_ref, hc_ref, ya_ref, x_ref,
                     cw_ref, cg_ref, ag_ref, wo_ref, fg_ref, wrh_ref, wrl_ref, br_ref,
                     x1_ref, h2_ref, idx_ref, gate_ref, rank_ref, cnt_ref, carry,
                     *, tile, n_experts):
    first_tile_of_seq = pl.program_id(1) == 0
    first_step = (pl.program_id(0) == 0) & first_tile_of_seq

    @pl.when(first_step)
    def _():
        carry[...] = jnp.zeros_like(carry)

    u = cc_ref[...] * cx_ref[...]
    uh = jnp.where(first_tile_of_seq, 0.0, hc_ref[...] * hx_ref[...])
    row8 = lax.broadcasted_iota(jnp.int32, (SUBLANES, 1), 0)

    def shifted(k):
        r = pltpu.roll(u, k, 0)
        top = jnp.where(row8 < k, pltpu.roll(uh, k, 0), r[:SUBLANES])
        return jnp.concatenate([top, r[SUBLANES:]], axis=0)

    conv = cw_ref[2:3, :] * u + cw_ref[1:2, :] * shifted(1) + cw_ref[0:1, :] * shifted(2)
    y_conv = cb_ref[...] * conv

    mixed = jnp.concatenate(
        [_rms(y_conv, cg_ref[...]), _rms(ya_ref[...], ag_ref[...])], axis=-1).astype(BF16)
    x1 = x_ref[...] + jnp.dot(mixed, wo_ref[...], preferred_element_type=F32)
    x1_ref[...] = x1
    h2 = _rms(x1, fg_ref[...])
    d = h2.shape[-1]
    chunks = d // LANES
    for c in range(chunks):
        h2_ref[pl.ds(c, tile, stride=chunks), :] = h2[:, c * LANES:(c + 1) * LANES]

    h2_hi = h2.astype(BF16)
    h2_lo = (h2 - h2_hi.astype(F32)).astype(BF16)
    logits = (jnp.dot(h2_hi, wrh_ref[...], preferred_element_type=F32)
              + jnp.dot(h2_hi, wrl_ref[...], preferred_element_type=F32)
              + jnp.dot(h2_lo, wrh_ref[...], preferred_element_type=F32)) + br_ref[...]
    eio = lax.broadcasted_iota(jnp.int32, (tile, n_experts), 1).astype(F32)
    work = logits
    vals, idxs = [], []
    for _ in range(TOP_K):
        m = jnp.max(work, axis=1, keepdims=True)
        ik = jnp.min(jnp.where(work == m, eio, float(n_experts)), axis=1, keepdims=True)
        vals.append(m)
        idxs.append(ik)
        work = jnp.where(eio == ik, -jnp.inf, work)
    exps = [jnp.exp(v - vals[0]) for v in vals]
    tot = exps[0] + exps[1] + exps[2] + exps[3]

    onehot = jnp.zeros((tile, n_experts), F32)
    for ik in idxs:
        onehot = onehot + (eio == ik).astype(F32)
    ri = lax.broadcasted_iota(jnp.int32, (tile, tile), 0)
    ci = lax.broadcasted_iota(jnp.int32, (tile, tile), 1)
    tri = (ci < ri).astype(BF16)
    before = jnp.dot(tri, onehot.astype(BF16), preferred_element_type=F32) + carry[...]
    carry[...] = carry[...] + jnp.sum(onehot, axis=0, keepdims=True)
    cnt_ref[...] = carry[...].astype(jnp.int32)

    lio = lax.broadcasted_iota(jnp.int32, (tile, LANES), 1)
    idx_out = jnp.zeros((tile, LANES), jnp.int32)
    gate_out = jnp.zeros((tile, LANES), F32)
    rank_out = jnp.zeros((tile, LANES), jnp.int32)
    for k in range(TOP_K):
        rk = jnp.sum(jnp.where(eio == idxs[k], before, 0.0), axis=1, keepdims=True)
        idx_out = jnp.where(lio == k, idxs[k].astype(jnp.int32), idx_out)
        gate_out = jnp.where(lio == k, exps[k] / tot, gate_out)
        rank_out = jnp.where(lio == k, rk.astype(jnp.int32), rank_out)
    idx_ref[...] = idx_out
    gate_ref[...] = gate_out
    rank_ref[...] = rank_out


def _out_proj(proj3, y_attn, x3, conv_w, conv_g, attn_g, w_out_bf16, ffn_g, w_router, b_router,
              tile=256):
    b, seq, d = x3.shape
    cw_cols = conv_w.shape[1]
    aw_cols = y_attn.shape[2]
    n_experts = w_router.shape[1]
    w_router_hi = w_router.astype(BF16)
    w_router_lo = (w_router - w_router_hi.astype(F32)).astype(BF16)
    chunks = d // LANES
    tiles = seq // tile
    n = b * seq
    halo_blocks = tile // SUBLANES
    row_blk = lambda width, col: pl.BlockSpec((None, tile, width), lambda i, j: (i, j, col))
    halo = lambda col: pl.BlockSpec(
        (None, SUBLANES, cw_cols), lambda i, j: (i, jnp.maximum(j * halo_blocks - 1, 0), col))
    const = lambda shape: pl.BlockSpec(shape, lambda i, j: (0,) * len(shape))
    flat = lambda width: pl.BlockSpec((tile, width), lambda i, j: (i * tiles + j, 0))
    outs = pl.pallas_call(
        functools.partial(_out_proj_kernel, tile=tile, n_experts=n_experts),
        grid=(b, tiles),
        in_specs=[
            row_blk(cw_cols, 0), row_blk(cw_cols, 1), row_blk(cw_cols, 2), halo(0), halo(2),
            row_blk(aw_cols, 0), row_blk(d, 0),
            const((CONV_K, cw_cols)), const((1, cw_cols)), const((1, aw_cols)),
            const((d, d)), const((1, d)), const((d, n_experts)), const((d, n_experts)),
            const((1, n_experts)),
        ],
        out_specs=[
            row_blk(d, 0),
            pl.BlockSpec((tile * chunks, LANES), lambda i, j: (i * tiles + j, 0)),
            flat(LANES), flat(LANES), flat(LANES),
            const((1, n_experts)),
        ],
        out_shape=[
            jax.ShapeDtypeStruct((b, seq, d), F32),
            jax.ShapeDtypeStruct((n * chunks, LANES), F32),
            jax.ShapeDtypeStruct((n, LANES), jnp.int32),
            jax.ShapeDtypeStruct((n, LANES), F32),
            jax.ShapeDtypeStruct((n, LANES), jnp.int32),
            jax.ShapeDtypeStruct((1, n_experts), jnp.int32),
        ],
        scratch_shapes=[pltpu.VMEM((1, n_experts), F32)],
        compiler_params=pltpu.CompilerParams(
            dimension_semantics=("arbitrary", "arbitrary"),
            vmem_limit_bytes=48 * 1024 * 1024),
        name="out_proj",
    )(proj3, proj3, proj3, proj3, proj3, y_attn, x3, conv_w, conv_g, attn_g, w_out_bf16,
      ffn_g, w_router_hi, w_router_lo, b_router)
    return outs


DMA_ISSUE_UNROLL = 4


def _dispatch_kernel(idx_ref, rank_ref, offs_ref, pad_start, pad_rows, used_blocks, h2_ref, xs_hbm,
                     dest_ref, zbuf, sem, zsem, *, tile, chunks, n_experts, n_row_blocks):
    gran = MOE_ROW_GRAN

    @pl.when(pl.program_id(0) == 0)
    def _():
        zbuf[...] = jnp.zeros_like(zbuf)

        def zero_copy(first_row, n_rows):
            return pltpu.make_async_copy(_token_rows(zbuf, 0, n_rows, chunks),
                                         _token_rows(xs_hbm, first_row, n_rows, chunks), zsem)

        def for_each_zero_run(act):
            def per_expert(e, carry):
                row = pad_start[e]
                left = pad_rows[e]
                for bit in reversed(range(gran.bit_length() - 1)):
                    take = (left >> bit) & 1

                    @pl.when(take == 1)
                    def _():
                        act(zero_copy(row, 1 << bit))

                    row = row + (take << bit)
                return carry

            lax.fori_loop(0, n_experts, per_expert, 0)

            def tail(blk, carry):
                act(zero_copy(blk * gran, gran))
                return carry

            lax.fori_loop(used_blocks[0], n_row_blocks, tail, 0)

        for_each_zero_run(lambda cp: cp.start())
        for_each_zero_run(lambda cp: cp.wait())

    def issue(g, carry):
        for u in range(DMA_ISSUE_UNROLL):
            t = g * DMA_ISSUE_UNROLL + u
            for k in range(TOP_K):
                a = t * TOP_K + k
                dest = offs_ref[idx_ref[0, 0, a]] + rank_ref[0, 0, a]
                dest_ref[0, 0, a] = dest
                pltpu.make_async_copy(_token_rows(h2_ref, t, 1, chunks),
                                      _token_rows(xs_hbm, dest, 1, chunks),
                                      sem).start(priority=k % 2)
        return carry

    lax.fori_loop(0, tile // DMA_ISSUE_UNROLL, issue, 0)
    counted = _token_rows(xs_hbm, 0, tile * TOP_K, chunks)
    pltpu.make_async_copy(counted, counted, sem).wait()


def _dispatch(idx_flat, rank_flat, offs, pad_start, pad_rows, used_blocks, h2_tm, n_rows, tile=512):
    steps = idx_flat.shape[0] // (tile * TOP_K)
    per_tile = lambda a: a.reshape(steps, 1, tile * TOP_K)
    chunks = h2_tm.shape[0] // (idx_flat.shape[0] // TOP_K)
    gran = MOE_ROW_GRAN
    smem = pl.BlockSpec(memory_space=pltpu.SMEM)
    tile_smem = pl.BlockSpec((1, 1, tile * TOP_K), lambda i: (i, 0, 0), memory_space=pltpu.SMEM)
    xs, dest = pl.pallas_call(
        functools.partial(_dispatch_kernel, tile=tile, chunks=chunks,
                          n_experts=pad_start.shape[0], n_row_blocks=n_rows // gran),
        grid=(steps,),
        in_specs=[
            tile_smem, tile_smem, smem, smem, smem, smem,
            pl.BlockSpec((tile * chunks, LANES), lambda i: (i, 0)),
        ],
        out_specs=[pl.BlockSpec(memory_space=pl.ANY), tile_smem],
        out_shape=[jax.ShapeDtypeStruct((n_rows * chunks, LANES), F32),
                   jax.ShapeDtypeStruct((steps, 1, tile * TOP_K), jnp.int32)],
        scratch_shapes=[pltpu.VMEM((gran * chunks, LANES), F32),
                        pltpu.SemaphoreType.DMA(()), pltpu.SemaphoreType.DMA(())],
        compiler_params=pltpu.CompilerParams(dimension_semantics=("arbitrary",)),
        name="dispatch",
    )(per_tile(idx_flat), per_tile(rank_flat), offs, pad_start, pad_rows, used_blocks, h2_tm)
    return xs, dest.reshape(-1)


MOE_WEIGHT_SLOTS = 3


def _moe_kernel(item_e, item_row0, item_nblk, used_blocks, live_items, xs_hbm, wgu_hbm, wdn_hbm,
                bgu_ref, bd_ref, ys_hbm, iobuf, xb, yacc, wgf, wuf, wdf, wgb, wub, wdb,
                sem_io, sem_w, *, chunks, n_col_steps, n_row_blocks):
    s = pl.program_id(0)
    nb = item_nblk[s]
    row0 = item_row0[s]
    gran = MOE_ROW_GRAN
    tn = MOE_COL_TILE
    de = n_col_steps * tn
    total_chunks = live_items[0] * n_col_steps

    def staged(slot):
        return iobuf.at[slot]

    def ys_block(blk):
        return _token_rows(ys_hbm, blk * gran, gran, chunks)

    def weight_copies(g):
        e = item_e[g // n_col_steps]
        col = pl.multiple_of((g % n_col_steps) * tn, tn)
        slot = g % MOE_WEIGHT_SLOTS
        return (
            pltpu.make_async_copy(wgu_hbm.at[e, :, pl.ds(col, tn)], wgf.at[slot], sem_w.at[slot]),
            pltpu.make_async_copy(wgu_hbm.at[e, :, pl.ds(pl.multiple_of(de + col, tn), tn)],
                                  wuf.at[slot], sem_w.at[slot]),
            pltpu.make_async_copy(wdn_hbm.at[e, pl.ds(col, tn), :], wdf.at[slot], sem_w.at[slot]),
        )

    def fetch(g):
        @pl.when(g < total_chunks)
        def _():
            for cp in weight_copies(g):
                cp.start()

    @pl.when(s == 0)
    def _():
        for g in range(MOE_WEIGHT_SLOTS - 1):
            fetch(g)

    @pl.when(nb > 0)
    def _():
        def io_slot(sb):
            return sb % MOE_IO_SLOTS

        def x_copy(sb):
            return pltpu.make_async_copy(
                _token_rows(xs_hbm, row0 + sb * gran, gran, chunks), staged(io_slot(sb)),
                sem_io.at[io_slot(sb)])

        def y_copy(sb):
            return pltpu.make_async_copy(staged(io_slot(sb)), ys_block(row0 // gran + sb),
                                         sem_io.at[io_slot(sb)])

        for sb in range(MOE_IO_SLOTS):
            @pl.when(sb < nb)
            def _():
                x_copy(sb).start()

        def block_rows(sb):
            return pl.ds(pl.multiple_of(sb * gran, gran), gran)

        def column_step(j, phase):
            g = s * n_col_steps + j
            slot = g % MOE_WEIGHT_SLOTS
            for cp in weight_copies(g):
                cp.wait()
            fetch(g + MOE_WEIGHT_SLOTS - 1)
            col = pl.multiple_of(j * tn, tn)
            bg = bgu_ref[:, pl.ds(col, tn)]
            bu = bgu_ref[:, pl.ds(pl.multiple_of(de + col, tn), tn)]

            def cast_weights():
                wgb[...] = wgf[slot].astype(BF16)
                wub[...] = wuf[slot].astype(BF16)
                wdb[...] = wdf[slot].astype(BF16)

            def partial_out(sb):
                x = xb[block_rows(sb), :]
                gate = jnp.dot(x, wgb[...], preferred_element_type=F32) + bg
                up = jnp.dot(x, wub[...], preferred_element_type=F32) + bu
                gate = jnp.minimum(gate, SWIGLU_LIMIT)
                up = jnp.clip(up, -SWIGLU_LIMIT, SWIGLU_LIMIT)
                act = (up + 1.0) * (gate * jax.nn.sigmoid(SWIGLU_ALPHA * gate))
                return jnp.dot(act.astype(BF16), wdb[...], preferred_element_type=F32)

            def trip(first, count):
                blocks = [first + u for u in range(count)]
                if phase == "first":
                    for sb in blocks:
                        x_copy(sb).wait()
                if phase == "last":
                    for sb in blocks:
                        @pl.when(sb >= MOE_IO_SLOTS)
                        def _():
                            y_copy(sb - MOE_IO_SLOTS).wait()
                parts = []
                for sb in blocks:
                    if phase == "first":
                        for c in range(chunks):
                            xb[block_rows(sb), c * LANES:(c + 1) * LANES] = (
                                iobuf[io_slot(sb), pl.ds(c, gran, stride=chunks), :].astype(BF16))
                    parts.append(partial_out(sb))
                for sb, part in zip(blocks, parts):
                    if phase == "first":
                        yacc[block_rows(sb), :] = part + bd_ref[...]
                    elif phase == "middle":
                        yacc[block_rows(sb), :] += part
                    else:
                        y = yacc[block_rows(sb), :] + part
                        for c in range(chunks):
                            iobuf[io_slot(sb), pl.ds(c, gran, stride=chunks), :] = (
                                y[:, c * LANES:(c + 1) * LANES])
                for sb in blocks:
                    if phase == "first":
                        @pl.when(sb + MOE_IO_SLOTS < nb)
                        def _():
                            x_copy(sb + MOE_IO_SLOTS).start()
                    if phase == "last":
                        y_copy(sb).start()

            full = MOE_TRIP_BLOCKS

            def tail(left):
                count = full // 2
                while count >= 1:
                    @pl.when(left % (2 * count) >= count)
                    def _():
                        trip(nb - left % (2 * count), count)

                    count //= 2

            @pl.when(nb >= full)
            def _():
                cast_weights()
                trip(0, full)

                def body(i, c):
                    trip(i * full, full)
                    return c

                lax.fori_loop(1, nb // full, body, 0)
                tail(nb % full)

            @pl.when(nb < full)
            def _():
                cast_weights()
                tail(nb)

        column_step(0, "first")

        def middle(j, carry):
            column_step(j, "middle")
            return carry

        lax.fori_loop(1, n_col_steps - 1, middle, 0)
        column_step(n_col_steps - 1, "last")

        for k in range(MOE_IO_SLOTS):
            @pl.when(nb > k)
            def _():
                y_copy(nb - 1 - k).wait()

    @pl.when(s == pl.num_programs(0) - 1)
    def _():
        iobuf[0] = jnp.zeros(iobuf.shape[1:], iobuf.dtype)

        def fill(blk, carry):
            cp = pltpu.make_async_copy(staged(0), ys_block(blk), sem_io.at[0])
            cp.start()
            cp.wait()
            return carry

        lax.fori_loop(used_blocks[0], n_row_blocks, fill, 0)


def _moe(item_e, item_row0, item_nblk, used_blocks, live_items, xs, w_gate_up, b_gate_up, w_down,
         b_down, n_rows):
    n_experts, d, two_de = w_gate_up.shape
    de = two_de // 2
    chunks = d // LANES
    tn = MOE_COL_TILE
    n_col_steps = de // tn
    assert n_col_steps >= 2
    n_items = item_e.shape[0]
    gran = MOE_ROW_GRAN
    slots = MOE_WEIGHT_SLOTS

    bgu3 = b_gate_up.reshape(n_experts, 1, two_de)
    bd3 = b_down.reshape(n_experts, 1, d)
    per_expert = lambda width: pl.BlockSpec((None, 1, width), lambda s, e, *_: (e[s], 0, 0))
    grid_spec = pltpu.PrefetchScalarGridSpec(
        num_scalar_prefetch=5,
        grid=(n_items,),
        in_specs=[
            pl.BlockSpec(memory_space=pl.ANY),
            pl.BlockSpec(memory_space=pl.ANY),
            pl.BlockSpec(memory_space=pl.ANY),
            per_expert(two_de),
            per_expert(d),
        ],
        out_specs=pl.BlockSpec(memory_space=pl.ANY),
        scratch_shapes=[
            pltpu.VMEM((MOE_IO_SLOTS, gran * chunks, LANES), F32),
            pltpu.VMEM((MOE_ITEM_ROWS, d), BF16),
            pltpu.VMEM((MOE_ITEM_ROWS, d), F32),
            pltpu.VMEM((slots, d, tn), F32),
            pltpu.VMEM((slots, d, tn), F32),
            pltpu.VMEM((slots, tn, d), F32),
            pltpu.VMEM((d, tn), BF16),
            pltpu.VMEM((d, tn), BF16),
            pltpu.VMEM((tn, d), BF16),
            pltpu.SemaphoreType.DMA((MOE_IO_SLOTS,)),
            pltpu.SemaphoreType.DMA((slots,)),
        ],
    )
    return pl.pallas_call(
        functools.partial(_moe_kernel, chunks=chunks, n_col_steps=n_col_steps,
                          n_row_blocks=n_rows // gran),
        grid_spec=grid_spec,
        out_shape=jax.ShapeDtypeStruct((n_rows * chunks, LANES), F32),
        compiler_params=pltpu.CompilerParams(
            dimension_semantics=("arbitrary",),
            vmem_limit_bytes=56 * 1024 * 1024),
        name="moe",
    )(item_e, item_row0, item_nblk, used_blocks, live_items, xs, w_gate_up, w_down, bgu3, bd3)


def _combine_kernel(dest_ref, ys_hbm, x1_ref, gate_ref, fg_ref, o_ref, gbuf, sem,
                    *, tile, chunks, final_norm):
    i = pl.program_id(0)
    slot = i % 2
    per_tile = tile * TOP_K
    stride = _token_row_stride(chunks)

    def gather(step, dst_slot):
        base = step * per_tile

        def issue(g, carry):
            for u in range(DMA_ISSUE_UNROLL):
                t = g * DMA_ISSUE_UNROLL + u
                for k in range(TOP_K):
                    pltpu.make_async_copy(
                        _token_rows(ys_hbm, dest_ref[base + t * TOP_K + k], 1, chunks),
                        gbuf.at[dst_slot, pl.ds(pl.multiple_of((k * tile + t) * stride, SUBLANES),
                                                chunks), :],
                        sem.at[dst_slot]).start(priority=k % 2)
            return carry

        lax.fori_loop(0, tile // DMA_ISSUE_UNROLL, issue, 0)

    @pl.when(i == 0)
    def _():
        gather(0, 0)

    @pl.when(i + 1 < pl.num_programs(0))
    def _():
        gather(i + 1, 1 - slot)

    counted = gbuf.at[slot, pl.ds(0, per_tile * chunks), :]
    pltpu.make_async_copy(counted, counted, sem.at[slot]).wait()

    gates = gate_ref[...]
    gk = [gates[:, k:k + 1] for k in range(TOP_K)]
    cols = []
    for c in range(chunks):
        acc = x1_ref[:, c * LANES:(c + 1) * LANES]
        for k in range(TOP_K):
            acc = acc + gk[k] * gbuf[slot, pl.ds(k * tile * stride + c, tile, stride=stride), :]
        cols.append(acc)
    out = jnp.concatenate(cols, axis=-1)
    if final_norm:
        out = _rms(out, fg_ref[...])
    o_ref[...] = out


def _combine(dest, ys, x1, gates, final_g, final_norm, tile=128):
    n, d = x1.shape
    chunks = d // LANES
    steps = n // tile
    grid_spec = pltpu.PrefetchScalarGridSpec(
        num_scalar_prefetch=1,
        grid=(steps,),
        in_specs=[
            pl.BlockSpec(memory_space=pl.ANY),
            pl.BlockSpec((tile, d), lambda i, *_: (i, 0)),
            pl.BlockSpec((tile, LANES), lambda i, *_: (i, 0)),
            pl.BlockSpec((1, d), lambda i, *_: (0, 0)),
        ],
        out_specs=pl.BlockSpec((tile, d), lambda i, *_: (i, 0)),
        scratch_shapes=[pltpu.VMEM((2, TOP_K * tile * _token_row_stride(chunks), LANES), F32),
                        pltpu.SemaphoreType.DMA((2,))],
    )
    return pl.pallas_call(
        functools.partial(_combine_kernel, tile=tile, chunks=chunks, final_norm=final_norm),
        grid_spec=grid_spec,
        out_shape=jax.ShapeDtypeStruct((n, d), F32),
        compiler_params=pltpu.CompilerParams(dimension_semantics=("arbitrary",),
                                             vmem_limit_bytes=40 * 1024 * 1024),
        name="combine",
    )(dest, ys, x1, gates, final_g)


def _routing_tables(counts, n_rows_cap):
    n_experts = counts.shape[0]
    gran, item_rows = MOE_ROW_GRAN, MOE_ITEM_ROWS
    padded = ((counts + gran - 1) // gran) * gran
    ends = jnp.cumsum(padded)
    offs = ends - padded
    items_per = (padded + item_rows - 1) // item_rows
    item_ends = jnp.cumsum(items_per)
    item_starts = item_ends - items_per
    n_items = n_experts + n_rows_cap // item_rows
    slot = jnp.arange(n_items, dtype=jnp.int32)
    total = item_ends[-1]
    live = slot < total
    live_slot = jnp.minimum(slot, total - 1)
    e_of = jnp.minimum(jnp.sum((live_slot[:, None] >= item_ends[None, :]).astype(jnp.int32), axis=1),
                       n_experts - 1).astype(jnp.int32)
    local = slot - item_starts[e_of]
    row0 = offs[e_of] + local * item_rows
    rows = jnp.clip(padded[e_of] - local * item_rows, 0, item_rows)
    nblk = jnp.where(live, rows // gran, 0).astype(jnp.int32)
    row0 = jnp.where(live, row0, 0).astype(jnp.int32)
    used_blocks = (ends[-1:] // gran).astype(jnp.int32)
    pad_start = (offs + counts).astype(jnp.int32)
    pad_rows = (padded - counts).astype(jnp.int32)
    live_items = total.reshape(1).astype(jnp.int32)
    return (offs.astype(jnp.int32), e_of, row0, nblk, used_blocks, live_items, pad_start,
            pad_rows)


def _layer(x3, mix_g, w_in, conv_w, conv_g, attn_g, w_out, ffn_g, w_router, b_router,
           w_gate_up, b_gate_up, w_down, b_down, final_g, final_norm):
    b, seq, d = x3.shape
    n = b * seq
    cw_cols = conv_w.shape[1]
    aw_cols = attn_g.shape[0]
    n_experts = w_router.shape[1]
    chunks = d // LANES

    proj = _in_proj(x3.reshape(n, d), mix_g.reshape(1, d), w_in.astype(BF16))
    proj3 = proj.reshape(b, seq, -1)
    y_attn = _attention(proj3, cw_cols, aw_cols)
    x1, h2_tm, idx, gates, rank, counts = _out_proj(
        proj3, y_attn, x3, conv_w, conv_g.reshape(1, -1), attn_g.reshape(1, -1),
        w_out.astype(BF16), ffn_g.reshape(1, d), w_router, b_router.reshape(1, -1))

    n_rows = n * TOP_K + n_experts * MOE_ROW_GRAN
    (offs, item_e, item_row0, item_nblk, used_blocks, live_items, pad_start,
     pad_rows) = _routing_tables(counts[0], n_rows)
    idx_flat = idx[:, :TOP_K].reshape(n * TOP_K)
    rank_flat = rank[:, :TOP_K].reshape(n * TOP_K)
    xs, dest = _dispatch(idx_flat, rank_flat, offs, pad_start, pad_rows, used_blocks, h2_tm,
                         n_rows)
    ys = _moe(item_e, item_row0, item_nblk, used_blocks, live_items, xs, w_gate_up, b_gate_up,
              w_down, b_down, n_rows)
    out = _combine(dest, ys, x1.reshape(n, d), gates, final_g.reshape(1, d), final_norm)
    return out.reshape(b, seq, d)


def kernel(x, mix_norm_g, w_in, conv_w, conv_norm_g, attn_norm_g, w_out, ffn_norm_g, w_router,
           b_router, w_gate_up, b_gate_up, w_down, b_down, final_norm_g):
    depth = w_in.shape[0]
    for layer in range(depth):
        x = _layer(x, mix_norm_g[layer], w_in[layer], conv_w[layer], conv_norm_g[layer],
                   attn_norm_g[layer], w_out[layer], ffn_norm_g[layer], w_router[layer],
                   b_router[layer], w_gate_up[layer], b_gate_up[layer], w_down[layer],
                   b_down[layer], final_norm_g, layer == depth - 1)
    return x
```
